```python
import math
import jax
import jax.numpy as jnp
from jax import lax
import numpy as np

D_MODEL = 1024
BATCH = 8
SEQ = 2048
DEPTH = 2
DEC_BATCH = 128
DEC_SEQ = 4
PAST_LEN = 16384
PAGE_SIZE = 128

EPS = 1e-6
A_HEADS = D_MODEL // 256
A_DK = 128
A_DV = 128
A_QK_W = A_HEADS * A_DK
A_V_W = A_HEADS * A_DV
A_CONV = 4
A_CONV_CH = 2 * A_QK_W + A_V_W
A_CHUNK = 64
A_IN = A_CONV_CH + A_V_W + 2 * A_HEADS
B_N = 64
B_HEADS = D_MODEL // 128
B_W = B_HEADS * B_N
B_DECAY_LORA = 64
B_AAA_LORA = 64
B_GATE_LORA = 128
B_LN_EPS = 64e-5
B_IN = 3 * B_W + B_DECAY_LORA + B_AAA_LORA + B_GATE_LORA
C_WIDTH = D_MODEL // 2
C_BLOCKS = 8
C_BLOCK = C_WIDTH // C_BLOCKS
C_CONV = 4
C_POW = 8.0
C_IN = 2 * C_WIDTH
N_BRANCH = 3
BRANCH_W = A_V_W
G_IN = N_BRANCH * D_MODEL
N_IN = A_IN + B_IN + C_IN + G_IN
D_FF = 4 * D_MODEL

kernel_name = 'hybrid_gdn_rwkv7_rglru_decode_step'


def split_cols(z, widths):
    out, off = [], 0
    for w in widths:
        out.append(z[..., off:off + w])
        off += w
    return out


def rmsnorm(x, g):
    xf = x.astype(jnp.float32)
    y = xf * lax.rsqrt(jnp.mean(xf * xf, axis=-1, keepdims=True) + EPS)
    return (y * g.astype(jnp.float32)).astype(x.dtype)


def l2norm(x):
    return x * lax.rsqrt(jnp.sum(x * x, axis=-1, keepdims=True) + EPS)


def causal_dwconv(x, buf, w):
    T = x.shape[1]
    width = w.shape[0]
    xp = jnp.concatenate([buf.astype(x.dtype), x], axis=1)
    y = xp[:, 0:T] * w[0]
    for j in range(1, width):
        y = y + xp[:, j:j + T] * w[j]
    return y, xp[:, T:]


def gated_delta_chunked(q, k, v, g, beta, S0):
    Bsz, T, H, DK = q.shape
    DV = v.shape[-1]
    C = A_CHUNK
    n = -(-T // C)
    pad = n * C - T

    def padt(a):
        return jnp.pad(a, [(0, 0), (0, pad)] + [(0, 0)] * (a.ndim - 2))

    def to_chunks(a):
        a = a.reshape((Bsz, n, C) + a.shape[2:])
        return a.transpose((1, 0, 3, 2) + tuple(range(4, a.ndim)))

    q, k, v, g, beta = [to_chunks(padt(t)) for t in (q, k, v, g, beta)]
    gc = jnp.cumsum(g, axis=-1)
    idx = jnp.arange(C)
    causal = idx[:, None] >= idx[None, :]
    strict = idx[:, None] > idx[None, :]
    decay = jnp.exp(jnp.where(causal, gc[..., :, None] - gc[..., None, :], -jnp.inf))
    kb = k * beta[..., None]
    L = jnp.where(strict, jnp.einsum('nbhid,nbhjd->nbhij', kb, k) * decay, 0.0)
    A = jnp.eye(C, dtype=jnp.float32) + L
    rhs = jnp.concatenate([v * beta[..., None], kb * jnp.exp(gc)[..., None]], axis=-1)
    sol = lax.linalg.triangular_solve(A, rhs, left_side=True, lower=True, unit_diagonal=True)
    u, w = sol[..., :DV], sol[..., DV:]
    attn = jnp.einsum('nbhid,nbhjd->nbhij', q, k) * decay
    qg = q * jnp.exp(gc)[..., None]
    kdec = k * jnp.exp(gc[..., -1:] - gc)[..., None]
    glast = jnp.exp(gc[..., -1])

    def step(S, xs):
        u_c, w_c, attn_c, qg_c, kdec_c, gl_c = xs
        vnew = u_c - jnp.einsum('bhcd,bhde->bhce', w_c, S)
        o = jnp.einsum('bhcd,bhde->bhce', qg_c, S) + jnp.einsum('bhij,bhje->bhie', attn_c, vnew)
        S = S * gl_c[..., None, None] + jnp.einsum('bhcd,bhce->bhde', kdec_c, vnew)
        return S, o

    S, o = lax.scan(step, S0, (u, w, attn, qg, kdec, glast))
    o = o.transpose(1, 0, 3, 2, 4).reshape(Bsz, n * C, H, DV)[:, :T]
    return o, S


def gdn_branch(za, conv_buf, S0, p):
    Bsz, T, _ = za.shape
    f32 = jnp.float32
    qkv, z, b_raw, a_raw = split_cols(za, (A_CONV_CH, A_V_W, A_HEADS, A_HEADS))
    qkv, new_conv = causal_dwconv(qkv, conv_buf, p['a_conv_w'])
    qkv = jax.nn.silu(qkv.astype(f32))
    q, k, v = split_cols(qkv, (A_QK_W, A_QK_W, A_V_W))
    q = l2norm(q.reshape(Bsz, T, A_HEADS, A_DK)) * (A_DK ** -0.5)
    k = l2norm(k.reshape(Bsz, T, A_HEADS, A_DK))
    v = v.reshape(Bsz, T, A_HEADS, A_DV)
    beta = jax.nn.sigmoid(b_raw.astype(f32))
    g = -jnp.exp(p['a_A_log'].astype(f32)) * jax.nn.softplus(a_raw.astype(f32) + p['a_dt_bias'])
    o, S = gated_delta_chunked(q, k, v, g, beta, S0.astype(f32))
    o = o * lax.rsqrt(jnp.mean(o * o, axis=-1, keepdims=True) + EPS)
    o = o * p['a_norm_g'] * jax.nn.silu(z.astype(f32).reshape(Bsz, T, A_HEADS, A_DV))
    return o.reshape(Bsz, T, A_V_W), S, new_conv


def rwkv7_scan(r, w, k, v, kk, a, S0):
    def step(S, xs):
        r_t, w_t, k_t, v_t, kk_t, a_t = xs
        sa = jnp.einsum('bhvk,bhk->bhv', S, -kk_t)
        S = (S * w_t[:, :, None, :] + sa[..., None] * (kk_t * a_t)[:, :, None, :]
             + v_t[..., None] * k_t[:, :, None, :])
        o = jnp.einsum('bhvk,bhk->bhv', S, r_t)
        return S, o

    xs = tuple(jnp.moveaxis(t, 1, 0) for t in (r, w, k, v, kk, a))
    S, o = lax.scan(step, S0, xs)
    return jnp.moveaxis(o, 0, 1), S


def rwkv7_branch(zb, shift_buf, S0, p):
    Bsz, T, _ = zb.shape
    f32 = jnp.float32
    zb = zb.astype(f32)
    prev = jnp.concatenate([shift_buf.astype(f32), zb[:, :-1]], axis=1)
    zs = zb + (prev - zb) * p['b_mu']
    r, k, v, xw, xa, xg = split_cols(zs, (B_W, B_W, B_W, B_DECAY_LORA, B_AAA_LORA, B_GATE_LORA))
    w_log = -jax.nn.softplus(-(p['b_w0'] + jnp.tanh(xw) @ p['b_w_up'])) - 0.5
    decay = jnp.exp(-jnp.exp(w_log))
    a = jax.nn.sigmoid(p['b_a0'] + xa @ p['b_a_up'])
    g = jax.nn.sigmoid(xg) @ p['b_g_up']
    kk = k * p['b_k_k']
    k = k * (1.0 + (a - 1.0) * p['b_k_a'])
    r, k, v, kk, a, decay = [t.reshape(Bsz, T, B_HEADS, B_N) for t in (r, k, v, kk, a, decay)]
    kk = l2norm(kk)
    o, S = rwkv7_scan(r, decay, k, v, kk, a, S0.astype(f32))
    mu = jnp.mean(o, axis=-1, keepdims=True)
    var = jnp.mean(jnp.square(o - mu), axis=-1, keepdims=True)
    o = ((o - mu) * lax.rsqrt(var + B_LN_EPS)).reshape(Bsz, T, B_W) * p['b_ln_w'] + p['b_ln_b']
    bonus = jnp.sum(r * k * p['b_r_k'], axis=-1, keepdims=True) * v
    o = o + bonus.reshape(Bsz, T, B_W)
    return o * g, S, zb[:, -1:]


def diag_linear_scan(a, b, h0):
    b = b.at[:, 0].add(a[:, 0] * h0)

    def combine(lft, rgt):
        al, bl = lft
        ar, br = rgt
        return al * ar, ar * bl + br

    _, h = lax.associative_scan(combine, (a, b), axis=1)
    return h


def rglru_branch(zc, conv_buf, h0, p):
    Bsz, T, _ = zc.shape
    f32 = jnp.float32
    xb, gb = split_cols(zc, (C_WIDTH, C_WIDTH))
    xc, new_conv = causal_dwconv(xb, conv_buf, p['c_conv_w'])
    xc = (xc + p['c_conv_b']).astype(f32)
    xblk = xc.reshape(Bsz, T, C_BLOCKS, C_BLOCK)
    r = jax.nn.sigmoid(jnp.einsum('btgi,gij->btgj', xblk, p['c_wa']).reshape(Bsz, T, C_WIDTH) + p['c_ba'])
    i = jax.nn.sigmoid(jnp.einsum('btgi,gij->btgj', xblk, p['c_wx']).reshape(Bsz, T, C_WIDTH) + p['c_bx'])
    log_a = -C_POW * r * jax.nn.softplus(-p['c_L'].astype(f32))
    a = jnp.exp(log_a)
    b = jnp.sqrt(-jnp.expm1(2.0 * log_a)) * (i * xc)
    h = diag_linear_scan(a, b, h0.astype(f32))
    y = h * jax.nn.gelu(gb.astype(f32))
    return y, h[:, -1], new_conv


def hybrid_layer(x, a_S, a_conv, b_S, b_shift, c_h, c_conv, p):
    Bsz, T, _ = x.shape
    h = rmsnorm(x, p['norm1_g'])
    z = h @ p['w_in']
    za, zb, zc, zg = split_cols(z, (A_IN, B_IN, C_IN, G_IN))
    ya, a_S, a_conv = gdn_branch(za, a_conv, a_S, p)
    yb, b_S, b_shift = rwkv7_branch(zb, b_shift, b_S, p)
    yc, c_h, c_conv = rglru_branch(zc, c_conv, c_h, p)
    br = jnp.stack([ya, yb, yc], axis=2).astype(x.dtype)
    proj = jnp.einsum('btnc,ncd->btnd', br, p['w_branch'])
    gates = jax.nn.sigmoid(zg.reshape(Bsz, T, N_BRANCH, D_MODEL))
    x = x + (jnp.sum(gates * proj, axis=2) @ p['w_out']).astype(x.dtype)
    h2 = rmsnorm(x, p['norm2_g'])
    x = x + (jnp.square(jax.nn.relu(h2 @ p['w_up'])) @ p['w_down']).astype(x.dtype)
    return x, (a_S, a_conv, b_S, b_shift, c_h, c_conv)


def setup_inputs(seed: int = 0) -> dict:
    key = jax.random.key(seed)
    ks = iter(jax.random.split(key, 48))
    nrm = lambda shape, s: jax.random.normal(next(ks), shape, jnp.float32) * s
    uni = lambda shape, lo, hi: jax.random.uniform(next(ks), shape, jnp.float32, lo, hi)
    x_prompt = nrm((BATCH, SEQ, D_MODEL), 1.0)
    x_sample = nrm((DEC_BATCH, DEC_SEQ, D_MODEL), 1.0)
    state_a_S = nrm((DEPTH, DEC_BATCH, A_HEADS, A_DK, A_DV), 0.3)
    state_a_conv = nrm((DEPTH, DEC_BATCH, A_CONV - 1, A_CONV_CH), 1.0)
    state_b_S = nrm((DEPTH, DEC_BATCH, B_HEADS, B_N, B_N), 0.3)
    state_b_shift = nrm((DEPTH, DEC_BATCH, 1, B_IN), 1.0)
    state_c_h = nrm((DEPTH, DEC_BATCH, C_WIDTH), 0.5)
    state_c_conv = nrm((DEPTH, DEC_BATCH, C_CONV - 1, C_WIDTH), 1.0)
    norm1_g = 1.0 + nrm((DEPTH, D_MODEL), 0.01)
    w_in = nrm((DEPTH, D_MODEL, N_IN), D_MODEL ** -0.5)
    a_conv_w = nrm((DEPTH, A_CONV, A_CONV_CH), A_CONV ** -0.5)
    a_A_log = jnp.log(uni((DEPTH, A_HEADS), 1.0, 16.0))
    dt = jnp.exp(uni((DEPTH, A_HEADS), math.log(1e-3), math.log(1e-1)))
    a_dt_bias = dt + jnp.log(-jnp.expm1(-dt))
    a_norm_g = 1.0 + nrm((DEPTH, A_DV), 0.01)
    b_mu = uni((DEPTH, B_IN), 0.0, 1.0)
    b_w0 = uni((DEPTH, B_W), -6.0, -1.0)
    b_w_up = nrm((DEPTH, B_DECAY_LORA, B_W), 0.1 * B_DECAY_LORA ** -0.5)
    b_a0 = nrm((DEPTH, B_W), 0.1)
    b_a_up = nrm((DEPTH, B_AAA_LORA, B_W), 0.1 * B_AAA_LORA ** -0.5)
    b_g_up = nrm((DEPTH, B_GATE_LORA, B_W), B_GATE_LORA ** -0.5)
    b_k_k = 0.85 + nrm((DEPTH, B_W), 0.01)
    b_k_a = 1.0 + nrm((DEPTH, B_W), 0.01)
    b_r_k = nrm((DEPTH, B_HEADS, B_N), 0.1)
    b_ln_w = 1.0 + nrm((DEPTH, B_W), 0.01)
    b_ln_b = nrm((DEPTH, B_W), 0.01)
    c_conv_w = nrm((DEPTH, C_CONV, C_WIDTH), C_CONV ** -0.5)
    c_conv_b = nrm((DEPTH, C_WIDTH), 0.01)
    c_wa = nrm((DEPTH, C_BLOCKS, C_BLOCK, C_BLOCK), C_BLOCK ** -0.5)
    c_ba = nrm((DEPTH, C_WIDTH), 0.01)
    c_wx = nrm((DEPTH, C_BLOCKS, C_BLOCK, C_BLOCK), C_BLOCK ** -0.5)
    c_bx = nrm((DEPTH, C_WIDTH), 0.01)
    base = uni((DEPTH, C_WIDTH), 0.9, 0.999) ** (1.0 / C_POW)
    c_L = jnp.log(base) - jnp.log1p(-base)
    w_branch = nrm((DEPTH, N_BRANCH, BRANCH_W, D_MODEL), BRANCH_W ** -0.5)
    w_out = nrm((DEPTH, D_MODEL, D_MODEL), D_MODEL ** -0.5)
    norm2_g = 1.0 + nrm((DEPTH, D_MODEL), 0.01)
    w_up = nrm((DEPTH, D_MODEL, D_FF), D_MODEL ** -0.5)
    w_down = nrm((DEPTH, D_FF, D_MODEL), 0.5 * D_FF ** -0.5)
    final_norm_g = 1.0 + nrm((D_MODEL,), 0.01)
    return {'x_prompt': x_prompt, 'x_sample': x_sample,
            'state_a_S': state_a_S, 'state_a_conv': state_a_conv,
            'state_b_S': state_b_S, 'state_b_shift': state_b_shift,
            'state_c_h': state_c_h, 'state_c_conv': state_c_conv,
            'norm1_g': norm1_g, 'w_in': w_in,
            'a_conv_w': a_conv_w, 'a_A_log': a_A_log, 'a_dt_bias': a_dt_bias, 'a_norm_g': a_norm_g,
            'b_mu': b_mu, 'b_w0': b_w0, 'b_w_up': b_w_up, 'b_a0': b_a0, 'b_a_up': b_a_up,
            'b_g_up': b_g_up, 'b_k_k': b_k_k, 'b_k_a': b_k_a, 'b_r_k': b_r_k,
            'b_ln_w': b_ln_w, 'b_ln_b': b_ln_b,
            'c_conv_w': c_conv_w, 'c_conv_b': c_conv_b, 'c_wa': c_wa, 'c_ba': c_ba,
            'c_wx': c_wx, 'c_bx': c_bx, 'c_L': c_L,
            'w_branch': w_branch, 'w_out': w_out, 'norm2_g': norm2_g,
            'w_up': w_up, 'w_down': w_down, 'final_norm_g': final_norm_g}


def reference(x_prompt, x_sample, state_a_S, state_a_conv, state_b_S, state_b_shift, state_c_h, state_c_conv,
              norm1_g, w_in, a_conv_w, a_A_log, a_dt_bias, a_norm_g,
              b_mu, b_w0, b_w_up, b_a0, b_a_up, b_g_up, b_k_k, b_k_a, b_r_k, b_ln_w, b_ln_b,
              c_conv_w, c_conv_b, c_wa, c_ba, c_wx, c_bx, c_L,
              w_branch, w_out, norm2_g, w_up, w_down, final_norm_g):
    f32 = jnp.float32
    xp, xs = x_prompt, x_sample
    Bp = x_prompt.shape[0]
    p_new = [[] for _ in range(6)]
    s_new = [[] for _ in range(6)]
    for l in range(DEPTH):
        p = {'norm1_g': norm1_g[l], 'w_in': w_in[l],
             'a_conv_w': a_conv_w[l], 'a_A_log': a_A_log[l], 'a_dt_bias': a_dt_bias[l], 'a_norm_g': a_norm_g[l],
             'b_mu': b_mu[l], 'b_w0': b_w0[l], 'b_w_up': b_w_up[l], 'b_a0': b_a0[l], 'b_a_up': b_a_up[l],
             'b_g_up': b_g_up[l], 'b_k_k': b_k_k[l], 'b_k_a': b_k_a[l], 'b_r_k': b_r_k[l],
             'b_ln_w': b_ln_w[l], 'b_ln_b': b_ln_b[l],
             'c_conv_w': c_conv_w[l], 'c_conv_b': c_conv_b[l], 'c_wa': c_wa[l], 'c_ba': c_ba[l],
             'c_wx': c_wx[l], 'c_bx': c_bx[l], 'c_L': c_L[l],
             'w_branch': w_branch[l], 'w_out': w_out[l], 'norm2_g': norm2_g[l],
             'w_up': w_up[l], 'w_down': w_down[l]}
        xp, pst = hybrid_layer(
            xp,
            jnp.zeros((Bp, A_HEADS, A_DK, A_DV), f32),
            jnp.zeros((Bp, A_CONV - 1, A_CONV_CH), xp.dtype),
            jnp.zeros((Bp, B_HEADS, B_N, B_N), f32),
            jnp.zeros((Bp, 1, B_IN), xp.dtype),
            jnp.zeros((Bp, C_WIDTH), f32),
            jnp.zeros((Bp, C_CONV - 1, C_WIDTH), xp.dtype),
            p)
        xs, sst = hybrid_layer(xs, state_a_S[l], state_a_conv[l], state_b_S[l], state_b_shift[l],
                               state_c_h[l], state_c_conv[l], p)
        for j in range(6):
            p_new[j].append(pst[j])
            s_new[j].append(sst[j])
    y_prompt = rmsnorm(xp, final_norm_g)
    y_sample = rmsnorm(xs, final_norm_g)
    return (y_prompt, y_sample,
            jnp.stack(p_new[0]), jnp.stack(p_new[1]), jnp.stack(p_new[2]),
            jnp.stack(p_new[3]), jnp.stack(p_new[4]), jnp.stack(p_new[5]),
            jnp.stack(s_new[0]), jnp.stack(s_new[1]), jnp.stack(s_new[2]),
            jnp.stack(s_new[3]), jnp.stack(s_new[4]), jnp.stack(s_new[5]))
```

```python
import functools

import jax
import jax.numpy as jnp
from jax import lax
from jax.experimental import pallas as pl
from jax.experimental.pallas import tpu as pltpu

F32 = jnp.float32
BF16 = jnp.bfloat16
HIGHEST = lax.Precision.HIGHEST

EPS = 1e-6
D_MODEL = 1024
DEPTH = 2
A_HEADS = 4
A_DK = 128
A_DV = 128
A_QK_W = A_HEADS * A_DK
A_V_W = A_HEADS * A_DV
A_CONV_CH = 2 * A_QK_W + A_V_W
B_N = 64
B_HEADS = 8
B_W = B_HEADS * B_N
B_LORA_WA = 128
B_GATE_LORA = 128
B_IN = 3 * B_W + B_LORA_WA + B_GATE_LORA
B_LN_EPS = 64e-5
C_WIDTH = 512
C_BLOCKS = 8
C_POW = 8.0
N_BRANCH = 3
D_FF = 4 * D_MODEL

LANE = 128
SUBLANE = 8
CONV_TAPS = 4
CONV_PAD = SUBLANE

ZA_W = A_CONV_CH + A_V_W + LANE
ZA_BA = A_CONV_CH + A_V_W
ZB_W = B_IN
ZC_W = 2 * C_WIDTH
ZG_W = N_BRANCH * D_MODEL
Z_WIDTHS = (ZA_W, ZB_W, ZC_W, ZG_W)
N_IN_PAD = sum(Z_WIDTHS)

VMEM_LIMIT = 56 * 1024 * 1024


def _dot(a, b):
    return jnp.dot(a.astype(BF16), b.astype(BF16), preferred_element_type=F32)


def _dot_nt(a, b):
    return lax.dot_general(a.astype(BF16), b.astype(BF16), (((1,), (1,)), ((), ())),
                           preferred_element_type=F32)


def _dot_tn(a, b):
    return lax.dot_general(a.astype(BF16), b.astype(BF16), (((0,), (0,)), ((), ())),
                           preferred_element_type=F32)


def _hdot(a, b):
    return jnp.dot(a, b, precision=HIGHEST, preferred_element_type=F32)


def _hdot_nt(a, b):
    return lax.dot_general(a, b, (((1,), (1,)), ((), ())), precision=HIGHEST,
                           preferred_element_type=F32)


def _hdot_tn(a, b):
    return lax.dot_general(a, b, (((0,), (0,)), ((), ())), precision=HIGHEST,
                           preferred_element_type=F32)


def _sigmoid(x):
    return 1.0 / (1.0 + jnp.exp(-x))


def _silu(x):
    return x * _sigmoid(x)


def _softplus(x):
    return jnp.maximum(x, 0.0) + jnp.log1p(jnp.exp(-jnp.abs(x)))


def _gelu_tanh(x):
    return 0.5 * x * (1.0 + jnp.tanh(0.7978845608028654 * (x + 0.044715 * (x * x * x))))


def _rows(shape):
    return lax.broadcasted_iota(jnp.int32, shape, 0)


def _cols(shape):
    return lax.broadcasted_iota(jnp.int32, shape, 1)


def _shift_rows(x, s, fill):
    return jnp.where(_rows(x.shape) >= s, pltpu.roll(x, s, 0), fill)


def _cumsum_rows(x):
    s = 1
    while s < x.shape[0]:
        x = x + _shift_rows(x, s, 0.0)
        s *= 2
    return x


def _unit_lower_inverse(n):
    c = n.shape[0]
    eye = (_rows(n.shape) == _cols(n.shape)).astype(F32)
    t = eye + n
    p = n
    s = 2
    while s < c:
        p = _hdot(p, p)
        t = t + _hdot(t, p)
        s *= 2
    return t


def _const_spec(shape):
    nd = len(shape)
    return pl.BlockSpec(shape, lambda *_: (0,) * nd, pipeline_mode=pl.Buffered(1))


def _in_proj_kernel(x_ref, g_ref, w_ref, za_ref, zb_ref, zc_ref, zg_ref):
    x = x_ref[...]
    h = x * lax.rsqrt(jnp.mean(x * x, axis=-1, keepdims=True) + EPS) * g_ref[...]
    h = h.astype(BF16)
    off = 0
    for ref, width in zip((za_ref, zb_ref, zc_ref, zg_ref), Z_WIDTHS):
        ref[...] = jnp.dot(h, w_ref[:, off:off + width], preferred_element_type=F32)
        off += width


def _in_proj(x, g, w, tm):
    m = x.shape[0]
    return pl.pallas_call(
        _in_proj_kernel,
        grid=(m // tm,),
        in_specs=[pl.BlockSpec((tm, D_MODEL), lambda i: (i, 0)),
                  _const_spec((1, D_MODEL)),
                  _const_spec((D_MODEL, N_IN_PAD))],
        out_specs=[pl.BlockSpec((tm, wd), lambda i: (i, 0)) for wd in Z_WIDTHS],
        out_shape=[jax.ShapeDtypeStruct((m, wd), F32) for wd in Z_WIDTHS],
        compiler_params=pltpu.CompilerParams(dimension_semantics=("arbitrary",),
                                             vmem_limit_bytes=VMEM_LIMIT),
        name="in_proj",
    )(x, g, w)


def _causal_conv(x, xpad, w_ref, first, hist_ref):
    c = x.shape[0]

    @pl.when(first)
    def _():
        xpad[CONV_PAD - 3:CONV_PAD, :] = hist_ref[...]

    xpad[CONV_PAD:CONV_PAD + c, :] = x
    y = x * w_ref[CONV_TAPS - 1:CONV_TAPS, :]
    for j in range(CONV_TAPS - 1):
        lo = CONV_PAD - 3 + j
        y = y + xpad[lo:lo + c, :] * w_ref[j:j + 1, :]
    return y


def _conv_carry(xpad, c, valid_rows, last, out_ref):
    @pl.when(last)
    def _():
        out_ref[...] = xpad[CONV_PAD - 3 + valid_rows:CONV_PAD + valid_rows, :]

    nxt = xpad[CONV_PAD - 3 + c:CONV_PAD + c, :]
    xpad[CONV_PAD - 3:CONV_PAD, :] = nxt


def _gdn_kernel(za_ref, conv0_ref, s0_ref, convw_ref, alog_ref, dt_ref, ng_ref,
                y_ref, sout_ref, convout_ref, xpad, s_scr, *, c, nt, valid_rows):
    t = pl.program_id(1)
    first = t == 0
    last = t == nt - 1

    @pl.when(first)
    def _():
        s_scr[...] = s0_ref[...]

    qkv = _silu(_causal_conv(za_ref[:, 0:A_CONV_CH], xpad, convw_ref, first, conv0_ref))
    _conv_carry(xpad, c, valid_rows, last, convout_ref)

    ba = za_ref[:, ZA_BA:ZA_BA + LANE]
    beta_all = _sigmoid(ba)
    g_all = -jnp.exp(alog_ref[...]) * _softplus(ba + dt_ref[...])
    masked = valid_rows < c
    if masked:
        valid = _rows((c, LANE)) < valid_rows
        beta_all = jnp.where(valid, beta_all, 0.0)
        g_all = jnp.where(valid, g_all, 0.0)
    gc_all = _cumsum_rows(g_all)

    rr = _rows((c, c))
    cc = _cols((c, c))
    causal = rr >= cc
    strict = rr > cc
    ng = ng_ref[...]

    for h in range(A_HEADS):
        q = qkv[:, h * A_DK:(h + 1) * A_DK]
        k = qkv[:, A_QK_W + h * A_DK:A_QK_W + (h + 1) * A_DK]
        v = qkv[:, 2 * A_QK_W + h * A_DV:2 * A_QK_W + (h + 1) * A_DV]
        q = q * lax.rsqrt(jnp.sum(q * q, axis=-1, keepdims=True) + EPS) * (A_DK ** -0.5)
        k = k * lax.rsqrt(jnp.sum(k * k, axis=-1, keepdims=True) + EPS)
        if masked:
            k = jnp.where(valid, k, 0.0)
            v = jnp.where(valid, v, 0.0)
        beta = beta_all[:, h:h + 1]
        gc = gc_all[:, A_HEADS + h:A_HEADS + h + 1]
        pick = (_cols((c, LANE)) == A_HEADS + h).astype(F32)
        gc_row = _hdot_nt(pick, gc_all)
        dec = jnp.where(causal, jnp.exp(jnp.where(causal, gc - gc_row, 0.0)), 0.0)
        kb = k * beta
        lmat = jnp.where(strict, _dot_nt(kb, k) * dec, 0.0)
        tinv = _unit_lower_inverse(-lmat)
        egc = jnp.exp(gc)
        sol = _hdot(tinv, jnp.concatenate([v * beta, kb * egc], axis=1))
        u = sol[:, :A_DV]
        w = sol[:, A_DV:]
        attn = _dot_nt(q, k) * dec
        s = s_scr[h]
        vnew = u - _dot(w, s)
        o = _dot(q * egc, s) + _dot(attn, vnew)
        gc_last = gc[c - 1:c, :]
        kdec = k * jnp.exp(gc_last - gc)
        s_scr[h] = s * jnp.exp(gc_last) + _dot_tn(kdec, vnew)
        o = o * lax.rsqrt(jnp.mean(o * o, axis=-1, keepdims=True) + EPS)
        z = za_ref[:, A_CONV_CH + h * A_DV:A_CONV_CH + (h + 1) * A_DV]
        y_ref[:, h * A_DV:(h + 1) * A_DV] = o * ng * _silu(z)

    @pl.when(last)
    def _():
        sout_ref[...] = s_scr[...]


def _gdn(za, conv0, s0, convw, alog_row, dt_row, ng, *, bsz, nt, c, valid_rows):
    kern = functools.partial(_gdn_kernel, c=c, nt=nt, valid_rows=valid_rows)
    return pl.pallas_call(
        kern,
        grid=(bsz, nt),
        in_specs=[pl.BlockSpec((c, ZA_W), lambda b, t: (b * nt + t, 0)),
                  pl.BlockSpec((None, 3, A_CONV_CH), lambda b, t: (b, 0, 0)),
                  pl.BlockSpec((None, A_HEADS, A_DK, A_DV), lambda b, t: (b, 0, 0, 0)),
                  _const_spec((CONV_TAPS, A_CONV_CH)),
                  _const_spec((1, LANE)),
                  _const_spec((1, LANE)),
                  _const_spec((1, A_DV))],
        out_specs=[pl.BlockSpec((c, A_V_W), lambda b, t: (b * nt + t, 0)),
                   pl.BlockSpec((None, A_HEADS, A_DK, A_DV), lambda b, t: (b, 0, 0, 0)),
                   pl.BlockSpec((None, 3, A_CONV_CH), lambda b, t: (b, 0, 0))],
        out_shape=[jax.ShapeDtypeStruct((bsz * nt * c, A_V_W), F32),
                   jax.ShapeDtypeStruct((bsz, A_HEADS, A_DK, A_DV), F32),
                   jax.ShapeDtypeStruct((bsz, 3, A_CONV_CH), F32)],
        scratch_shapes=[pltpu.VMEM((CONV_PAD + c, A_CONV_CH), F32),
                        pltpu.VMEM((A_HEADS, A_DK, A_DV), F32)],
        compiler_params=pltpu.CompilerParams(dimension_semantics=("arbitrary", "arbitrary"),
                                             vmem_limit_bytes=VMEM_LIMIT),
        name="gdn",
    )(za, conv0, s0, convw, alog_row, dt_row, ng)


def _rwkv_kernel(zb_ref, shift0_ref, s0_ref, mu_ref, w0_ref, wup_ref, a0_ref, aup_ref, gup_ref,
                 kk_ref, ka_ref, rk_ref, lnw_ref, lnb_ref,
                 y_ref, sout_ref, shiftout_ref, xpad, s_scr, *, c, nt, valid_rows):
    t = pl.program_id(1)
    first = t == 0
    last = t == nt - 1

    @pl.when(first)
    def _():
        s_scr[...] = s0_ref[...]
        xpad[CONV_PAD - 1:CONV_PAD, :] = shift0_ref[...]

    zb = zb_ref[...]
    xpad[CONV_PAD:CONV_PAD + c, :] = zb
    prev = xpad[CONV_PAD - 1:CONV_PAD - 1 + c, :]
    zs = zb + (prev - zb) * mu_ref[...]

    @pl.when(last)
    def _():
        shiftout_ref[...] = xpad[CONV_PAD - 1 + valid_rows:CONV_PAD + valid_rows, :]

    nxt = xpad[CONV_PAD - 1 + c:CONV_PAD + c, :]
    xpad[CONV_PAD - 1:CONV_PAD, :] = nxt

    r_all = zs[:, 0:B_W]
    k_all = zs[:, B_W:2 * B_W]
    v_all = zs[:, 2 * B_W:3 * B_W]
    xwa = zs[:, 3 * B_W:3 * B_W + B_LORA_WA]
    xg = zs[:, 3 * B_W + B_LORA_WA:B_IN]
    w_log = -_softplus(-(w0_ref[...] + _dot(jnp.tanh(xwa), wup_ref[...]))) - 0.5
    lw = -jnp.exp(w_log)
    a_all = _sigmoid(a0_ref[...] + _dot(xwa, aup_ref[...]))
    g_all = _dot(_sigmoid(xg), gup_ref[...])
    kk_all = k_all * kk_ref[...]
    k_all = k_all * (1.0 + (a_all - 1.0) * ka_ref[...])
    masked = valid_rows < c
    if masked:
        valid = _rows((c, B_W)) < valid_rows
        lw = jnp.where(valid, lw, 0.0)
        kk_all = jnp.where(valid, kk_all, 0.0)
        k_all = jnp.where(valid, k_all, 0.0)
        v_all = jnp.where(valid, v_all, 0.0)

    gc = _cumsum_rows(lw)
    gmid = gc[c // 2:c // 2 + 1, :]
    glast = gc[c - 1:c, :]
    e_abs = jnp.exp(gc)
    e_abs_prev = jnp.exp(gc - lw)
    e_mid = jnp.exp(gc - gmid)
    e_mid_prev = jnp.exp(gc - lw - gmid)
    e_inv = jnp.exp(gmid - gc)
    e_tail = jnp.exp(glast - gc)
    e_last = jnp.exp(glast)

    rr = _rows((c, c))
    cc = _cols((c, c))
    incl = rr >= cc
    strict = rr > cc

    for h in range(B_HEADS):
        sl = slice(h * B_N, (h + 1) * B_N)
        r = r_all[:, sl]
        k = k_all[:, sl]
        v = v_all[:, sl]
        kk = kk_all[:, sl]
        kk = kk * lax.rsqrt(jnp.sum(kk * kk, axis=-1, keepdims=True) + EPS)
        alpha = -kk
        bet = kk * a_all[:, sl]
        lhs = jnp.concatenate([alpha * e_mid_prev[:, sl], r * e_mid[:, sl]], axis=0)
        rhs = jnp.concatenate([bet * e_inv[:, sl], k * e_inv[:, sl]], axis=0)
        amat = _hdot_nt(lhs, rhs)
        a_ab = jnp.where(strict, amat[:c, :c], 0.0)
        a_ak = jnp.where(strict, amat[:c, c:], 0.0)
        a_rb = jnp.where(incl, amat[c:, :c], 0.0)
        a_rk = jnp.where(incl, amat[c:, c:], 0.0)
        s = s_scr[h]
        tinv = _unit_lower_inverse(a_ab)
        u = _hdot(tinv, _hdot_nt(alpha * e_abs_prev[:, sl], s) + _hdot(a_ak, v))
        o = _hdot_nt(r * e_abs[:, sl], s) + _hdot(a_rb, u) + _hdot(a_rk, v)
        s_scr[h] = (s * e_last[:, sl] + _hdot_tn(u, bet * e_tail[:, sl])
                    + _hdot_tn(v, k * e_tail[:, sl]))
        mean = jnp.mean(o, axis=-1, keepdims=True)
        var = jnp.mean(jnp.square(o - mean), axis=-1, keepdims=True)
        o = (o - mean) * lax.rsqrt(var + B_LN_EPS) * lnw_ref[:, sl] + lnb_ref[:, sl]
        o = o + jnp.sum(r * k * rk_ref[:, sl], axis=-1, keepdims=True) * v
        y_ref[:, sl] = o * g_all[:, sl]

    @pl.when(last)
    def _():
        sout_ref[...] = s_scr[...]


def _rwkv(zb, shift0, s0, prm, *, bsz, nt, c, valid_rows):
    kern = functools.partial(_rwkv_kernel, c=c, nt=nt, valid_rows=valid_rows)
    row = lambda wd: _const_spec((1, wd))
    return pl.pallas_call(
        kern,
        grid=(bsz, nt),
        in_specs=[pl.BlockSpec((c, ZB_W), lambda b, t: (b * nt + t, 0)),
                  pl.BlockSpec((None, 1, B_IN), lambda b, t: (b, 0, 0)),
                  pl.BlockSpec((None, B_HEADS, B_N, B_N), lambda b, t: (b, 0, 0, 0)),
                  row(B_IN), row(B_W), _const_spec((B_LORA_WA, B_W)), row(B_W),
                  _const_spec((B_LORA_WA, B_W)), _const_spec((B_GATE_LORA, B_W)),
                  row(B_W), row(B_W), row(B_W), row(B_W), row(B_W)],
        out_specs=[pl.BlockSpec((c, B_W), lambda b, t: (b * nt + t, 0)),
                   pl.BlockSpec((None, B_HEADS, B_N, B_N), lambda b, t: (b, 0, 0, 0)),
                   pl.BlockSpec((None, 1, B_IN), lambda b, t: (b, 0, 0))],
        out_shape=[jax.ShapeDtypeStruct((bsz * nt * c, B_W), F32),
                   jax.ShapeDtypeStruct((bsz, B_HEADS, B_N, B_N), F32),
                   jax.ShapeDtypeStruct((bsz, 1, B_IN), F32)],
        scratch_shapes=[pltpu.VMEM((CONV_PAD + c, B_IN), F32),
                        pltpu.VMEM((B_HEADS, B_N, B_N), F32)],
        compiler_params=pltpu.CompilerParams(dimension_semantics=("arbitrary", "arbitrary"),
                                             vmem_limit_bytes=VMEM_LIMIT),
        name="rwkv",
    )(zb, shift0, s0, *prm)


def _rglru_kernel(zc_ref, conv0_ref, h0_ref, convw_ref, convb_ref, wg_ref, bg_ref, l_ref,
                  y_ref, hout_ref, convout_ref, xpad, h_scr, *, c, nt, valid_rows):
    t = pl.program_id(1)
    first = t == 0
    last = t == nt - 1

    @pl.when(first)
    def _():
        h_scr[...] = h0_ref[...]

    xc = _causal_conv(zc_ref[:, 0:C_WIDTH], xpad, convw_ref, first, conv0_ref) + convb_ref[...]
    _conv_carry(xpad, c, valid_rows, last, convout_ref)

    gates = _sigmoid(_dot(xc, wg_ref[...]) + bg_ref[...])
    r = gates[:, :C_WIDTH]
    i = gates[:, C_WIDTH:]
    log_a = -C_POW * r * _softplus(-l_ref[...])
    a = jnp.exp(log_a)
    b = jnp.sqrt(-jnp.tanh(log_a) * (a * a + 1.0)) * (i * xc)
    if valid_rows < c:
        valid = _rows((c, C_WIDTH)) < valid_rows
        a = jnp.where(valid, a, 1.0)
        b = jnp.where(valid, b, 0.0)
    s = 1
    while s < c:
        b = a * _shift_rows(b, s, 0.0) + b
        a = a * _shift_rows(a, s, 1.0)
        s *= 2
    hseq = a * h_scr[...] + b
    h_scr[...] = hseq[c - 1:c, :]
    y_ref[...] = hseq * _gelu_tanh(zc_ref[:, C_WIDTH:])

    @pl.when(last)
    def _():
        hout_ref[...] = hseq[c - 1:c, :]


def _rglru(zc, conv0, h0, prm, *, bsz, nt, c, valid_rows):
    kern = functools.partial(_rglru_kernel, c=c, nt=nt, valid_rows=valid_rows)
    return pl.pallas_call(
        kern,
        grid=(bsz, nt),
        in_specs=[pl.BlockSpec((c, ZC_W), lambda b, t: (b * nt + t, 0)),
                  pl.BlockSpec((None, 3, C_WIDTH), lambda b, t: (b, 0, 0)),
                  pl.BlockSpec((None, 1, C_WIDTH), lambda b, t: (b, 0, 0)),
                  _const_spec((CONV_TAPS, C_WIDTH)), _const_spec((1, C_WIDTH)),
                  _const_spec((C_WIDTH, 2 * C_WIDTH)), _const_spec((1, 2 * C_WIDTH)),
                  _const_spec((1, C_WIDTH))],
        out_specs=[pl.BlockSpec((c, C_WIDTH), lambda b, t: (b * nt + t, 0)),
                   pl.BlockSpec((None, 1, C_WIDTH), lambda b, t: (b, 0, 0)),
                   pl.BlockSpec((None, 3, C_WIDTH), lambda b, t: (b, 0, 0))],
        out_shape=[jax.ShapeDtypeStruct((bsz * nt * c, C_WIDTH), F32),
                   jax.ShapeDtypeStruct((bsz, 1, C_WIDTH), F32),
                   jax.ShapeDtypeStruct((bsz, 3, C_WIDTH), F32)],
        scratch_shapes=[pltpu.VMEM((CONV_PAD + c, C_WIDTH), F32),
                        pltpu.VMEM((1, C_WIDTH), F32)],
        compiler_params=pltpu.CompilerParams(dimension_semantics=("arbitrary", "arbitrary"),
                                             vmem_limit_bytes=VMEM_LIMIT),
        name="rglru",
    )(zc, conv0, h0, *prm)


FF_CHUNK = 1024


def _merge_ffn_kernel(x_ref, ya_ref, yb_ref, yc_ref, zg_ref, wbr_ref, wout_ref, g2_ref,
                      wup_ref, wdown_ref, gf_ref, o_ref, *, final):
    merged = None
    for n, y_ref in enumerate((ya_ref, yb_ref, yc_ref)):
        proj = _dot(y_ref[...], wbr_ref[n])
        term = _sigmoid(zg_ref[:, n * D_MODEL:(n + 1) * D_MODEL]) * proj
        merged = term if merged is None else merged + term
    x = x_ref[...] + _dot(merged, wout_ref[...])
    h = x * lax.rsqrt(jnp.mean(x * x, axis=-1, keepdims=True) + EPS) * g2_ref[...]
    h = h.astype(BF16)
    acc = x
    for j in range(0, D_FF, FF_CHUNK):
        up = jnp.dot(h, wup_ref[:, j:j + FF_CHUNK], preferred_element_type=F32)
        acc = acc + _dot(jnp.square(jnp.maximum(up, 0.0)), wdown_ref[j:j + FF_CHUNK, :])
    if final:
        acc = acc * lax.rsqrt(jnp.mean(acc * acc, axis=-1, keepdims=True) + EPS) * gf_ref[...]
    o_ref[...] = acc


def _merge_ffn(x, ya, yb, yc, zg, wbr, wout, g2, wup, wdown, gf, tm, final):
    m = x.shape[0]
    tile = lambda wd: pl.BlockSpec((tm, wd), lambda i: (i, 0))
    return pl.pallas_call(
        functools.partial(_merge_ffn_kernel, final=final),
        grid=(m // tm,),
        in_specs=[tile(D_MODEL), tile(A_V_W), tile(B_W), tile(C_WIDTH), tile(ZG_W),
                  _const_spec((N_BRANCH, A_V_W, D_MODEL)), _const_spec((D_MODEL, D_MODEL)),
                  _const_spec((1, D_MODEL)), _const_spec((D_MODEL, D_FF)),
                  _const_spec((D_FF, D_MODEL)), _const_spec((1, D_MODEL))],
        out_specs=tile(D_MODEL),
        out_shape=jax.ShapeDtypeStruct((m, D_MODEL), F32),
        compiler_params=pltpu.CompilerParams(dimension_semantics=("arbitrary",),
                                             vmem_limit_bytes=VMEM_LIMIT),
        name="merge_ffn",
    )(x, ya, yb, yc, zg, wbr, wout, g2, wup, wdown, gf)


def _block_diag(w):
    g, n, _ = w.shape
    eye = jnp.eye(g, dtype=w.dtype)
    return (eye[:, None, :, None] * w[:, :, None, :]).reshape(g * n, g * n)


def _layer_params(l, p):
    row = lambda a: a[l].reshape(1, -1).astype(F32)
    w_in = p['w_in'][l]
    a_in = A_CONV_CH + A_V_W + 2 * A_HEADS
    pad = jnp.zeros((D_MODEL, LANE - 2 * A_HEADS), w_in.dtype)
    w_in = jnp.concatenate([w_in[:, :a_in], pad, w_in[:, a_in:]], axis=1).astype(BF16)
    lane_row = lambda a: jnp.zeros((1, LANE), F32).at[0, A_HEADS:2 * A_HEADS].set(a[l])
    zeros_lora = jnp.zeros((B_LORA_WA // 2, B_W), F32)
    gdn = (p['a_conv_w'][l], lane_row(p['a_A_log']), lane_row(p['a_dt_bias']), row(p['a_norm_g']))
    rwkv = (row(p['b_mu']), row(p['b_w0']),
            jnp.concatenate([p['b_w_up'][l], zeros_lora], axis=0).astype(BF16),
            row(p['b_a0']),
            jnp.concatenate([zeros_lora, p['b_a_up'][l]], axis=0).astype(BF16),
            p['b_g_up'][l].astype(BF16),
            row(p['b_k_k']), row(p['b_k_a']), row(p['b_r_k']), row(p['b_ln_w']), row(p['b_ln_b']))
    rglru = (p['c_conv_w'][l], row(p['c_conv_b']),
             jnp.concatenate([_block_diag(p['c_wa'][l]), _block_diag(p['c_wx'][l])],
                             axis=1).astype(BF16),
             jnp.concatenate([row(p['c_ba']), row(p['c_bx'])], axis=1), row(p['c_L']))
    ffn = (p['w_branch'][l].astype(BF16), p['w_out'][l].astype(BF16), row(p['norm2_g']),
           p['w_up'][l].astype(BF16), p['w_down'][l].astype(BF16))
    return dict(norm1_g=row(p['norm1_g']), w_in=w_in, gdn=gdn, rwkv=rwkv, rglru=rglru, ffn=ffn)


def _layer(x, states, lp, gf, *, bsz, t_pad, c, valid_rows, tm, final):
    a_s, a_conv, b_s, b_shift, c_h, c_conv = states
    nt = t_pad // c
    kw = dict(bsz=bsz, nt=nt, c=c, valid_rows=valid_rows)
    za, zb, zc, zg = _in_proj(x, lp['norm1_g'], lp['w_in'], tm)
    ya, a_s, a_conv = _gdn(za, a_conv, a_s, *lp['gdn'], **kw)
    yb, b_s, b_shift = _rwkv(zb, b_shift, b_s, lp['rwkv'], **kw)
    yc, c_h, c_conv = _rglru(zc, c_conv, c_h.reshape(bsz, 1, C_WIDTH), lp['rglru'], **kw)
    x = _merge_ffn(x, ya, yb, yc, zg, *lp['ffn'], gf, tm, final)
    return x, (a_s, a_conv, b_s, b_shift, c_h.reshape(bsz, C_WIDTH), c_conv)


def kernel(x_prompt, x_sample, state_a_S, state_a_conv, state_b_S, state_b_shift, state_c_h, state_c_conv, norm1_g, w_in, a_conv_w, a_A_log, a_dt_bias, a_norm_g, b_mu, b_w0, b_w_up, b_a0, b_a_up, b_g_up, b_k_k, b_k_a, b_r_k, b_ln_w, b_ln_b, c_conv_w, c_conv_b, c_wa, c_ba, c_wx, c_bx, c_L, w_branch, w_out, norm2_g, w_up, w_down, final_norm_g):
    p = dict(norm1_g=norm1_g, w_in=w_in, a_conv_w=a_conv_w, a_A_log=a_A_log, a_dt_bias=a_dt_bias,
             a_norm_g=a_norm_g, b_mu=b_mu, b_w0=b_w0, b_w_up=b_w_up, b_a0=b_a0, b_a_up=b_a_up,
             b_g_up=b_g_up, b_k_k=b_k_k, b_k_a=b_k_a, b_r_k=b_r_k, b_ln_w=b_ln_w, b_ln_b=b_ln_b,
             c_conv_w=c_conv_w, c_conv_b=c_conv_b, c_wa=c_wa, c_ba=c_ba, c_wx=c_wx, c_bx=c_bx,
             c_L=c_L, w_branch=w_branch, w_out=w_out, norm2_g=norm2_g, w_up=w_up, w_down=w_down)
    bp, tp, _ = x_prompt.shape
    bs, ts, _ = x_sample.shape
    ts_pad = -(-ts // SUBLANE) * SUBLANE
    gf = final_norm_g.reshape(1, D_MODEL)

    xp = x_prompt.reshape(bp * tp, D_MODEL)
    xs = jnp.pad(x_sample, ((0, 0), (0, ts_pad - ts), (0, 0))).reshape(bs * ts_pad, D_MODEL)
    zero = lambda *shape: jnp.zeros(shape, F32)
    p_new = [[] for _ in range(6)]
    s_new = [[] for _ in range(6)]
    for l in range(DEPTH):
        lp = _layer_params(l, p)
        final = l == DEPTH - 1
        p_states = (zero(bp, A_HEADS, A_DK, A_DV), zero(bp, 3, A_CONV_CH),
                    zero(bp, B_HEADS, B_N, B_N), zero(bp, 1, B_IN),
                    zero(bp, C_WIDTH), zero(bp, 3, C_WIDTH))
        xp, pst = _layer(xp, p_states, lp, gf, bsz=bp, t_pad=tp, c=64, valid_rows=64,
                         tm=256, final=final)
        s_states = (state_a_S[l], state_a_conv[l], state_b_S[l], state_b_shift[l],
                    state_c_h[l], state_c_conv[l])
        xs, sst = _layer(xs, s_states, lp, gf, bsz=bs, t_pad=ts_pad, c=ts_pad, valid_rows=ts,
                         tm=256, final=final)
        for j in range(6):
            p_new[j].append(pst[j])
            s_new[j].append(sst[j])
    y_prompt = xp.reshape(bp, tp, D_MODEL)
    y_sample = xs.reshape(bs, ts_pad, D_MODEL)[:, :ts]
    return (y_prompt, y_sample,
            jnp.stack(p_new[0]), jnp.stack(p_new[1]), jnp.stack(p_new[2]),
            jnp.stack(p_new[3]), jnp.stack(p_new[4]), jnp.stack(p_new[5]),
            jnp.stack(s_new[0]), jnp.stack(s_new[1]), jnp.stack(s_new[2]),
            jnp.stack(s_new[3]), jnp.stack(s_new[4]), jnp.stack(s_new[5]))
```

```python
import functools

import jax
import jax.numpy as jnp
from jax import lax
from jax.experimental import pallas as pl
from jax.experimental.pallas import tpu as pltpu

F32 = jnp.float32
BF16 = jnp.bfloat16
HIGHEST = lax.Precision.HIGHEST

EPS = 1e-6
D_MODEL = 1024
DEPTH = 2
A_HEADS = 4
A_DK = 128
A_DV = 128
A_QK_W = A_HEADS * A_DK
A_V_W = A_HEADS * A_DV
A_CONV_CH = 2 * A_QK_W + A_V_W
B_N = 64
B_HEADS = 8
B_W = B_HEADS * B_N
B_LORA_WA = 128
B_GATE_LORA = 128
B_IN = 3 * B_W + B_LORA_WA + B_GATE_LORA
B_LN_EPS = 64e-5
C_WIDTH = 512
C_BLOCKS = 8
C_POW = 8.0
N_BRANCH = 3
D_FF = 4 * D_MODEL

LANE = 128
SUBLANE = 8
CONV_TAPS = 4
CONV_PAD = SUBLANE

ZA_W = A_CONV_CH + A_V_W + LANE
ZA_BA = A_CONV_CH + A_V_W
ZB_W = B_IN
ZC_W = 2 * C_WIDTH
ZG_W = N_BRANCH * D_MODEL
Z_WIDTHS = (ZA_W, ZB_W, ZC_W, ZG_W)
N_IN_PAD = sum(Z_WIDTHS)

VMEM_LIMIT = 56 * 1024 * 1024


def _dot(a, b):
    return jnp.dot(a.astype(BF16), b.astype(BF16), preferred_element_type=F32)


def _dot_nt(a, b):
    return lax.dot_general(a.astype(BF16), b.astype(BF16), (((1,), (1,)), ((), ())),
                           preferred_element_type=F32)


def _dot_tn(a, b):
    return lax.dot_general(a.astype(BF16), b.astype(BF16), (((0,), (0,)), ((), ())),
                           preferred_element_type=F32)


def _sigmoid(x):
    return 1.0 / (1.0 + jnp.exp(-x))


def _silu(x):
    return x * _sigmoid(x)


def _softplus(x):
    return jnp.maximum(x, 0.0) + jnp.log1p(jnp.exp(-jnp.abs(x)))


def _gelu_tanh(x):
    return 0.5 * x * (1.0 + jnp.tanh(0.7978845608028654 * (x + 0.044715 * (x * x * x))))


def _rows(shape):
    return lax.broadcasted_iota(jnp.int32, shape, 0)


def _cols(shape):
    return lax.broadcasted_iota(jnp.int32, shape, 1)


def _shift_rows(x, s, fill):
    return jnp.where(_rows(x.shape) >= s, pltpu.roll(x, s, 0), fill)


def _cumsum_rows(x):
    s = 1
    while s < x.shape[0]:
        x = x + _shift_rows(x, s, 0.0)
        s *= 2
    return x


def _unit_lower_inverses(ns):
    c = ns[0].shape[0]
    eye = (_rows(ns[0].shape) == _cols(ns[0].shape)).astype(F32)
    ts = [eye + n for n in ns]
    ps = list(ns)
    s = 2
    while s < c:
        ps = [_dot(p, p) for p in ps]
        ts = [t + _dot(t, p) for t, p in zip(ts, ps)]
        s *= 2
    return ts


def _const_spec(shape):
    nd = len(shape)
    return pl.BlockSpec(shape, lambda *_: (0,) * nd, pipeline_mode=pl.Buffered(1))


def _in_proj_kernel(x_ref, g_ref, w_ref, za_ref, zb_ref, zc_ref, zg_ref):
    x = x_ref[...]
    h = x * lax.rsqrt(jnp.mean(x * x, axis=-1, keepdims=True) + EPS) * g_ref[...]
    h = h.astype(BF16)
    off = 0
    for ref, width in zip((za_ref, zb_ref, zc_ref, zg_ref), Z_WIDTHS):
        ref[...] = jnp.dot(h, w_ref[:, off:off + width], preferred_element_type=F32)
        off += width


def _in_proj(x, g, w, tm):
    m = x.shape[0]
    return pl.pallas_call(
        _in_proj_kernel,
        grid=(m // tm,),
        in_specs=[pl.BlockSpec((tm, D_MODEL), lambda i: (i, 0)),
                  _const_spec((1, D_MODEL)),
                  _const_spec((D_MODEL, N_IN_PAD))],
        out_specs=[pl.BlockSpec((tm, wd), lambda i: (i, 0)) for wd in Z_WIDTHS],
        out_shape=[jax.ShapeDtypeStruct((m, wd), F32) for wd in Z_WIDTHS],
        compiler_params=pltpu.CompilerParams(dimension_semantics=("arbitrary",),
                                             vmem_limit_bytes=VMEM_LIMIT),
        name="in_proj",
    )(x, g, w)


def _causal_conv(x, xpad, w_ref, first, hist_ref):
    c = x.shape[0]

    @pl.when(first)
    def _():
        xpad[CONV_PAD - 3:CONV_PAD, :] = hist_ref[...]

    xpad[CONV_PAD:CONV_PAD + c, :] = x
    y = x * w_ref[CONV_TAPS - 1:CONV_TAPS, :]
    for j in range(CONV_TAPS - 1):
        lo = CONV_PAD - 3 + j
        y = y + xpad[lo:lo + c, :] * w_ref[j:j + 1, :]
    return y


def _conv_carry(xpad, c, valid_rows, last, out_ref):
    @pl.when(last)
    def _():
        out_ref[...] = xpad[CONV_PAD - 3 + valid_rows:CONV_PAD + valid_rows, :]

    nxt = xpad[CONV_PAD - 3 + c:CONV_PAD + c, :]
    xpad[CONV_PAD - 3:CONV_PAD, :] = nxt


def _gdn_kernel(za_ref, conv0_ref, s0_ref, convw_ref, alog_ref, dt_ref, ng_ref,
                y_ref, sout_ref, convout_ref, xpad, s_scr, *, c, nt, valid_rows):
    t = pl.program_id(1)
    first = t == 0
    last = t == nt - 1

    @pl.when(first)
    def _():
        s_scr[...] = s0_ref[...]

    qkv = _silu(_causal_conv(za_ref[:, 0:A_CONV_CH], xpad, convw_ref, first, conv0_ref))
    _conv_carry(xpad, c, valid_rows, last, convout_ref)

    ba = za_ref[:, ZA_BA:ZA_BA + LANE]
    beta_all = _sigmoid(ba)
    g_all = -jnp.exp(alog_ref[...]) * _softplus(ba + dt_ref[...])
    masked = valid_rows < c
    if masked:
        valid = _rows((c, LANE)) < valid_rows
        beta_all = jnp.where(valid, beta_all, 0.0)
        g_all = jnp.where(valid, g_all, 0.0)
    gc_all = _cumsum_rows(g_all)

    rr = _rows((c, c))
    cc = _cols((c, c))
    causal = rr >= cc
    strict = rr > cc
    ng = ng_ref[...]
    pick = (_cols((A_HEADS * c, LANE)) == A_HEADS + _rows((A_HEADS * c, LANE)) // c).astype(F32)
    gc_rows = lax.dot_general(pick, gc_all, (((1,), (1,)), ((), ())), precision=HIGHEST,
                              preferred_element_type=F32)

    heads = range(A_HEADS)
    ks, vs, gcs, lower, from_state, scores = [], [], [], [], [], []
    for h in heads:
        q = qkv[:, h * A_DK:(h + 1) * A_DK]
        k = qkv[:, A_QK_W + h * A_DK:A_QK_W + (h + 1) * A_DK]
        v = qkv[:, 2 * A_QK_W + h * A_DV:2 * A_QK_W + (h + 1) * A_DV]
        q = q * lax.rsqrt(jnp.sum(q * q, axis=-1, keepdims=True) + EPS) * (A_DK ** -0.5)
        k = k * lax.rsqrt(jnp.sum(k * k, axis=-1, keepdims=True) + EPS)
        if masked:
            k = jnp.where(valid, k, 0.0)
            v = jnp.where(valid, v, 0.0)
        beta = beta_all[:, h:h + 1]
        gc = gc_all[:, A_HEADS + h:A_HEADS + h + 1]
        gc_row = gc_rows[h * c:(h + 1) * c, :]
        dec = jnp.where(causal, jnp.exp(jnp.where(causal, gc - gc_row, 0.0)), 0.0)
        kb = k * beta
        egc = jnp.exp(gc)
        sc = _dot_nt(jnp.concatenate([kb, q], axis=0), k)
        from_state.append(_dot(jnp.concatenate([kb * egc, q * egc], axis=0), s_scr[h]))
        lower.append(jnp.where(strict, -sc[:c] * dec, 0.0))
        scores.append(sc[c:] * dec)
        ks.append(k)
        vs.append(v * beta)
        gcs.append(gc)
    tinvs = _unit_lower_inverses(lower)
    vnews = [_dot(tinvs[h], vs[h] - from_state[h][:c]) for h in heads]
    outs = [from_state[h][c:] + _dot(scores[h], vnews[h]) for h in heads]
    for h in heads:
        gc_last = gcs[h][c - 1:c, :]
        kdec = ks[h] * jnp.exp(gc_last - gcs[h])
        s_scr[h] = s_scr[h] * jnp.exp(gc_last) + _dot_tn(kdec, vnews[h])
    for h in heads:
        o = outs[h]
        o = o * lax.rsqrt(jnp.mean(o * o, axis=-1, keepdims=True) + EPS)
        z = za_ref[:, A_CONV_CH + h * A_DV:A_CONV_CH + (h + 1) * A_DV]
        y_ref[:, h * A_DV:(h + 1) * A_DV] = o * ng * _silu(z)

    @pl.when(last)
    def _():
        sout_ref[...] = s_scr[...]


def _gdn(za, conv0, s0, convw, alog_row, dt_row, ng, *, bsz, nt, c, valid_rows):
    kern = functools.partial(_gdn_kernel, c=c, nt=nt, valid_rows=valid_rows)
    return pl.pallas_call(
        kern,
        grid=(bsz, nt),
        in_specs=[pl.BlockSpec((c, ZA_W), lambda b, t: (b * nt + t, 0)),
                  pl.BlockSpec((None, 3, A_CONV_CH), lambda b, t: (b, 0, 0)),
                  pl.BlockSpec((None, A_HEADS, A_DK, A_DV), lambda b, t: (b, 0, 0, 0)),
                  _const_spec((CONV_TAPS, A_CONV_CH)),
                  _const_spec((1, LANE)),
                  _const_spec((1, LANE)),
                  _const_spec((1, A_DV))],
        out_specs=[pl.BlockSpec((c, A_V_W), lambda b, t: (b * nt + t, 0)),
                   pl.BlockSpec((None, A_HEADS, A_DK, A_DV), lambda b, t: (b, 0, 0, 0)),
                   pl.BlockSpec((None, 3, A_CONV_CH), lambda b, t: (b, 0, 0))],
        out_shape=[jax.ShapeDtypeStruct((bsz * nt * c, A_V_W), F32),
                   jax.ShapeDtypeStruct((bsz, A_HEADS, A_DK, A_DV), F32),
                   jax.ShapeDtypeStruct((bsz, 3, A_CONV_CH), F32)],
        scratch_shapes=[pltpu.VMEM((CONV_PAD + c, A_CONV_CH), F32),
                        pltpu.VMEM((A_HEADS, A_DK, A_DV), F32)],
        compiler_params=pltpu.CompilerParams(dimension_semantics=("arbitrary", "arbitrary"),
                                             vmem_limit_bytes=VMEM_LIMIT),
        name="gdn",
    )(za, conv0, s0, convw, alog_row, dt_row, ng)


def _rwkv_kernel(zb_ref, shift0_ref, s0_ref, mu_ref, w0_ref, wup_ref, a0_ref, aup_ref, gup_ref,
                 kk_ref, ka_ref, rk_ref, lnw_ref, lnb_ref,
                 y_ref, sout_ref, shiftout_ref, xpad, s_scr, *, c, nt, valid_rows):
    t = pl.program_id(1)
    first = t == 0
    last = t == nt - 1

    @pl.when(first)
    def _():
        s_scr[...] = s0_ref[...]
        xpad[CONV_PAD - 1:CONV_PAD, :] = shift0_ref[...]

    zb = zb_ref[...]
    xpad[CONV_PAD:CONV_PAD + c, :] = zb
    prev = xpad[CONV_PAD - 1:CONV_PAD - 1 + c, :]
    zs = zb + (prev - zb) * mu_ref[...]

    @pl.when(last)
    def _():
        shiftout_ref[...] = xpad[CONV_PAD - 1 + valid_rows:CONV_PAD + valid_rows, :]

    nxt = xpad[CONV_PAD - 1 + c:CONV_PAD + c, :]
    xpad[CONV_PAD - 1:CONV_PAD, :] = nxt

    r_all = zs[:, 0:B_W]
    k_all = zs[:, B_W:2 * B_W]
    v_all = zs[:, 2 * B_W:3 * B_W]
    xwa = zs[:, 3 * B_W:3 * B_W + B_LORA_WA]
    xg = zs[:, 3 * B_W + B_LORA_WA:B_IN]
    w_log = -_softplus(-(w0_ref[...] + _dot(jnp.tanh(xwa), wup_ref[...]))) - 0.5
    lw = -jnp.exp(w_log)
    a_all = _sigmoid(a0_ref[...] + _dot(xwa, aup_ref[...]))
    g_all = _dot(_sigmoid(xg), gup_ref[...])
    kk_all = k_all * kk_ref[...]
    k_all = k_all * (1.0 + (a_all - 1.0) * ka_ref[...])
    masked = valid_rows < c
    if masked:
        valid = _rows((c, B_W)) < valid_rows
        lw = jnp.where(valid, lw, 0.0)
        kk_all = jnp.where(valid, kk_all, 0.0)
        k_all = jnp.where(valid, k_all, 0.0)
        v_all = jnp.where(valid, v_all, 0.0)

    gc = _cumsum_rows(lw)
    gmid = gc[c // 2:c // 2 + 1, :]
    glast = gc[c - 1:c, :]
    e_abs = jnp.exp(gc)
    e_abs_prev = jnp.exp(gc - lw)
    e_mid = jnp.exp(gc - gmid)
    e_mid_prev = jnp.exp(gc - lw - gmid)
    e_inv = jnp.exp(gmid - gc)
    e_tail = jnp.exp(glast - gc)
    e_last = jnp.exp(glast)

    rr = _rows((2 * c, 2 * c))
    cc = _cols((2 * c, 2 * c))
    tt = jnp.where(rr >= c, rr - c, rr)
    ss = jnp.where(cc >= c, cc - c, cc)
    mask2 = jnp.logical_or(tt > ss, jnp.logical_and(rr >= c, tt == ss))

    heads = range(B_HEADS)
    sls = [slice(h * B_N, (h + 1) * B_N) for h in heads]
    amats, from_state, tails, bonus = [], [], [], []
    for h in heads:
        sl = sls[h]
        r = r_all[:, sl]
        k = k_all[:, sl]
        kk = kk_all[:, sl]
        kk = kk * lax.rsqrt(jnp.sum(kk * kk, axis=-1, keepdims=True) + EPS)
        alpha = -kk
        bet = kk * a_all[:, sl]
        lhs = jnp.concatenate([alpha * e_mid_prev[:, sl], r * e_mid[:, sl]], axis=0)
        rhs = jnp.concatenate([bet * e_inv[:, sl], k * e_inv[:, sl]], axis=0)
        amats.append(jnp.where(mask2, _dot_nt(lhs, rhs), 0.0))
        from_state.append(_dot_nt(
            jnp.concatenate([alpha * e_abs_prev[:, sl], r * e_abs[:, sl]], axis=0), s_scr[h]))
        tails.append(jnp.concatenate([bet * e_tail[:, sl], k * e_tail[:, sl]], axis=0))
        bonus.append(jnp.sum(r * k * rk_ref[:, sl], axis=-1, keepdims=True))
    from_v = [_dot(amats[h][:, c:], v_all[:, sls[h]]) for h in heads]
    tinvs = _unit_lower_inverses([amats[h][:c, :c] for h in heads])
    us = [_dot(tinvs[h], from_state[h][:c] + from_v[h][:c]) for h in heads]
    outs = [from_state[h][c:] + from_v[h][c:] + _dot(amats[h][c:, :c], us[h]) for h in heads]
    for h in heads:
        sl = sls[h]
        s_scr[h] = (s_scr[h] * e_last[:, sl]
                    + _dot_tn(jnp.concatenate([us[h], v_all[:, sl]], axis=0), tails[h]))
    for h in heads:
        sl = sls[h]
        o = outs[h]
        mean = jnp.mean(o, axis=-1, keepdims=True)
        var = jnp.mean(jnp.square(o - mean), axis=-1, keepdims=True)
        o = (o - mean) * lax.rsqrt(var + B_LN_EPS) * lnw_ref[:, sl] + lnb_ref[:, sl]
        o = o + bonus[h] * v_all[:, sl]
        y_ref[:, sl] = o * g_all[:, sl]

    @pl.when(last)
    def _():
        sout_ref[...] = s_scr[...]


def _rwkv(zb, shift0, s0, prm, *, bsz, nt, c, valid_rows):
    kern = functools.partial(_rwkv_kernel, c=c, nt=nt, valid_rows=valid_rows)
    row = lambda wd: _const_spec((1, wd))
    return pl.pallas_call(
        kern,
        grid=(bsz, nt),
        in_specs=[pl.BlockSpec((c, ZB_W), lambda b, t: (b * nt + t, 0)),
                  pl.BlockSpec((None, 1, B_IN), lambda b, t: (b, 0, 0)),
                  pl.BlockSpec((None, B_HEADS, B_N, B_N), lambda b, t: (b, 0, 0, 0)),
                  row(B_IN), row(B_W), _const_spec((B_LORA_WA, B_W)), row(B_W),
                  _const_spec((B_LORA_WA, B_W)), _const_spec((B_GATE_LORA, B_W)),
                  row(B_W), row(B_W), row(B_W), row(B_W), row(B_W)],
        out_specs=[pl.BlockSpec((c, B_W), lambda b, t: (b * nt + t, 0)),
                   pl.BlockSpec((None, B_HEADS, B_N, B_N), lambda b, t: (b, 0, 0, 0)),
                   pl.BlockSpec((None, 1, B_IN), lambda b, t: (b, 0, 0))],
        out_shape=[jax.ShapeDtypeStruct((bsz * nt * c, B_W), F32),
                   jax.ShapeDtypeStruct((bsz, B_HEADS, B_N, B_N), F32),
                   jax.ShapeDtypeStruct((bsz, 1, B_IN), F32)],
        scratch_shapes=[pltpu.VMEM((CONV_PAD + c, B_IN), F32),
                        pltpu.VMEM((B_HEADS, B_N, B_N), F32)],
        compiler_params=pltpu.CompilerParams(dimension_semantics=("arbitrary", "arbitrary"),
                                             vmem_limit_bytes=VMEM_LIMIT),
        name="rwkv",
    )(zb, shift0, s0, *prm)


def _rglru_kernel(zc_ref, conv0_ref, h0_ref, convw_ref, convb_ref, wg_ref, bg_ref, l_ref,
                  y_ref, hout_ref, convout_ref, xpad, h_scr, *, c, nt, valid_rows):
    t = pl.program_id(1)
    first = t == 0
    last = t == nt - 1

    @pl.when(first)
    def _():
        h_scr[...] = h0_ref[...]

    xc = _causal_conv(zc_ref[:, 0:C_WIDTH], xpad, convw_ref, first, conv0_ref) + convb_ref[...]
    _conv_carry(xpad, c, valid_rows, last, convout_ref)

    gates = _sigmoid(_dot(xc, wg_ref[...]) + bg_ref[...])
    r = gates[:, :C_WIDTH]
    i = gates[:, C_WIDTH:]
    log_a = -C_POW * r * _softplus(-l_ref[...])
    a = jnp.exp(log_a)
    b = jnp.sqrt(-jnp.tanh(log_a) * (a * a + 1.0)) * (i * xc)
    if valid_rows < c:
        valid = _rows((c, C_WIDTH)) < valid_rows
        a = jnp.where(valid, a, 1.0)
        b = jnp.where(valid, b, 0.0)
    s = 1
    while s < c:
        b = a * _shift_rows(b, s, 0.0) + b
        a = a * _shift_rows(a, s, 1.0)
        s *= 2
    hseq = a * h_scr[...] + b
    h_scr[...] = hseq[c - 1:c, :]
    y_ref[...] = hseq * _gelu_tanh(zc_ref[:, C_WIDTH:])

    @pl.when(last)
    def _():
        hout_ref[...] = hseq[c - 1:c, :]


def _rglru(zc, conv0, h0, prm, *, bsz, nt, c, valid_rows):
    kern = functools.partial(_rglru_kernel, c=c, nt=nt, valid_rows=valid_rows)
    return pl.pallas_call(
        kern,
        grid=(bsz, nt),
        in_specs=[pl.BlockSpec((c, ZC_W), lambda b, t: (b * nt + t, 0)),
                  pl.BlockSpec((None, 3, C_WIDTH), lambda b, t: (b, 0, 0)),
                  pl.BlockSpec((None, 1, C_WIDTH), lambda b, t: (b, 0, 0)),
                  _const_spec((CONV_TAPS, C_WIDTH)), _const_spec((1, C_WIDTH)),
                  _const_spec((C_WIDTH, 2 * C_WIDTH)), _const_spec((1, 2 * C_WIDTH)),
                  _const_spec((1, C_WIDTH))],
        out_specs=[pl.BlockSpec((c, C_WIDTH), lambda b, t: (b * nt + t, 0)),
                   pl.BlockSpec((None, 1, C_WIDTH), lambda b, t: (b, 0, 0)),
                   pl.BlockSpec((None, 3, C_WIDTH), lambda b, t: (b, 0, 0))],
        out_shape=[jax.ShapeDtypeStruct((bsz * nt * c, C_WIDTH), F32),
                   jax.ShapeDtypeStruct((bsz, 1, C_WIDTH), F32),
                   jax.ShapeDtypeStruct((bsz, 3, C_WIDTH), F32)],
        scratch_shapes=[pltpu.VMEM((CONV_PAD + c, C_WIDTH), F32),
                        pltpu.VMEM((1, C_WIDTH), F32)],
        compiler_params=pltpu.CompilerParams(dimension_semantics=("arbitrary", "arbitrary"),
                                             vmem_limit_bytes=VMEM_LIMIT),
        name="rglru",
    )(zc, conv0, h0, *prm)


FF_CHUNK = 1024


def _merge_ffn_kernel(x_ref, ya_ref, yb_ref, yc_ref, zg_ref, wbr_ref, wout_ref, g2_ref,
                      wup_ref, wdown_ref, gf_ref, o_ref, *, final):
    merged = None
    for n, y_ref in enumerate((ya_ref, yb_ref, yc_ref)):
        proj = _dot(y_ref[...], wbr_ref[n])
        term = _sigmoid(zg_ref[:, n * D_MODEL:(n + 1) * D_MODEL]) * proj
        merged = term if merged is None else merged + term
    x = x_ref[...] + _dot(merged, wout_ref[...])
    h = x * lax.rsqrt(jnp.mean(x * x, axis=-1, keepdims=True) + EPS) * g2_ref[...]
    h = h.astype(BF16)
    acc = x
    for j in range(0, D_FF, FF_CHUNK):
        up = jnp.dot(h, wup_ref[:, j:j + FF_CHUNK], preferred_element_type=F32)
        acc = acc + _dot(jnp.square(jnp.maximum(up, 0.0)), wdown_ref[j:j + FF_CHUNK, :])
    if final:
        acc = acc * lax.rsqrt(jnp.mean(acc * acc, axis=-1, keepdims=True) + EPS) * gf_ref[...]
    o_ref[...] = acc


def _merge_ffn(x, ya, yb, yc, zg, wbr, wout, g2, wup, wdown, gf, tm, final):
    m = x.shape[0]
    tile = lambda wd: pl.BlockSpec((tm, wd), lambda i: (i, 0))
    return pl.pallas_call(
        functools.partial(_merge_ffn_kernel, final=final),
        grid=(m // tm,),
        in_specs=[tile(D_MODEL), tile(A_V_W), tile(B_W), tile(C_WIDTH), tile(ZG_W),
                  _const_spec((N_BRANCH, A_V_W, D_MODEL)), _const_spec((D_MODEL, D_MODEL)),
                  _const_spec((1, D_MODEL)), _const_spec((D_MODEL, D_FF)),
                  _const_spec((D_FF, D_MODEL)), _const_spec((1, D_MODEL))],
        out_specs=tile(D_MODEL),
        out_shape=jax.ShapeDtypeStruct((m, D_MODEL), F32),
        compiler_params=pltpu.CompilerParams(dimension_semantics=("arbitrary",),
                                             vmem_limit_bytes=VMEM_LIMIT),
        name="merge_ffn",
    )(x, ya, yb, yc, zg, wbr, wout, g2, wup, wdown, gf)


def _block_diag(w):
    g, n, _ = w.shape
    eye = jnp.eye(g, dtype=w.dtype)
    return (eye[:, None, :, None] * w[:, :, None, :]).reshape(g * n, g * n)


def _layer_params(l, p):
    row = lambda a: a[l].reshape(1, -1).astype(F32)
    w_in = p['w_in'][l]
    a_in = A_CONV_CH + A_V_W + 2 * A_HEADS
    pad = jnp.zeros((D_MODEL, LANE - 2 * A_HEADS), w_in.dtype)
    w_in = jnp.concatenate([w_in[:, :a_in], pad, w_in[:, a_in:]], axis=1).astype(BF16)
    lane_row = lambda a: jnp.zeros((1, LANE), F32).at[0, A_HEADS:2 * A_HEADS].set(a[l])
    zeros_lora = jnp.zeros((B_LORA_WA // 2, B_W), F32)
    gdn = (p['a_conv_w'][l], lane_row(p['a_A_log']), lane_row(p['a_dt_bias']), row(p['a_norm_g']))
    rwkv = (row(p['b_mu']), row(p['b_w0']),
            jnp.concatenate([p['b_w_up'][l], zeros_lora], axis=0).astype(BF16),
            row(p['b_a0']),
            jnp.concatenate([zeros_lora, p['b_a_up'][l]], axis=0).astype(BF16),
            p['b_g_up'][l].astype(BF16),
            row(p['b_k_k']), row(p['b_k_a']), row(p['b_r_k']), row(p['b_ln_w']), row(p['b_ln_b']))
    rglru = (p['c_conv_w'][l], row(p['c_conv_b']),
             jnp.concatenate([_block_diag(p['c_wa'][l]), _block_diag(p['c_wx'][l])],
                             axis=1).astype(BF16),
             jnp.concatenate([row(p['c_ba']), row(p['c_bx'])], axis=1), row(p['c_L']))
    ffn = (p['w_branch'][l].astype(BF16), p['w_out'][l].astype(BF16), row(p['norm2_g']),
           p['w_up'][l].astype(BF16), p['w_down'][l].astype(BF16))
    return dict(norm1_g=row(p['norm1_g']), w_in=w_in, gdn=gdn, rwkv=rwkv, rglru=rglru, ffn=ffn)


def _layer(x, states, lp, gf, *, bsz, t_pad, c, valid_rows, tm, final):
    a_s, a_conv, b_s, b_shift, c_h, c_conv = states
    nt = t_pad // c
    kw = dict(bsz=bsz, nt=nt, c=c, valid_rows=valid_rows)
    za, zb, zc, zg = _in_proj(x, lp['norm1_g'], lp['w_in'], tm)
    ya, a_s, a_conv = _gdn(za, a_conv, a_s, *lp['gdn'], **kw)
    yb, b_s, b_shift = _rwkv(zb, b_shift, b_s, lp['rwkv'], **kw)
    yc, c_h, c_conv = _rglru(zc, c_conv, c_h.reshape(bsz, 1, C_WIDTH), lp['rglru'], **kw)
    x = _merge_ffn(x, ya, yb, yc, zg, *lp['ffn'], gf, tm, final)
    return x, (a_s, a_conv, b_s, b_shift, c_h.reshape(bsz, C_WIDTH), c_conv)


def kernel(x_prompt, x_sample, state_a_S, state_a_conv, state_b_S, state_b_shift, state_c_h, state_c_conv, norm1_g, w_in, a_conv_w, a_A_log, a_dt_bias, a_norm_g, b_mu, b_w0, b_w_up, b_a0, b_a_up, b_g_up, b_k_k, b_k_a, b_r_k, b_ln_w, b_ln_b, c_conv_w, c_conv_b, c_wa, c_ba, c_wx, c_bx, c_L, w_branch, w_out, norm2_g, w_up, w_down, final_norm_g):
    p = dict(norm1_g=norm1_g, w_in=w_in, a_conv_w=a_conv_w, a_A_log=a_A_log, a_dt_bias=a_dt_bias,
             a_norm_g=a_norm_g, b_mu=b_mu, b_w0=b_w0, b_w_up=b_w_up, b_a0=b_a0, b_a_up=b_a_up,
             b_g_up=b_g_up, b_k_k=b_k_k, b_k_a=b_k_a, b_r_k=b_r_k, b_ln_w=b_ln_w, b_ln_b=b_ln_b,
             c_conv_w=c_conv_w, c_conv_b=c_conv_b, c_wa=c_wa, c_ba=c_ba, c_wx=c_wx, c_bx=c_bx,
             c_L=c_L, w_branch=w_branch, w_out=w_out, norm2_g=norm2_g, w_up=w_up, w_down=w_down)
    bp, tp, _ = x_prompt.shape
    bs, ts, _ = x_sample.shape
    ts_pad = -(-ts // SUBLANE) * SUBLANE
    gf = final_norm_g.reshape(1, D_MODEL)

    xp = x_prompt.reshape(bp * tp, D_MODEL)
    xs = jnp.pad(x_sample, ((0, 0), (0, ts_pad - ts), (0, 0))).reshape(bs * ts_pad, D_MODEL)
    zero = lambda *shape: jnp.zeros(shape, F32)
    p_new = [[] for _ in range(6)]
    s_new = [[] for _ in range(6)]
    for l in range(DEPTH):
        lp = _layer_params(l, p)
        final = l == DEPTH - 1
        p_states = (zero(bp, A_HEADS, A_DK, A_DV), zero(bp, 3, A_CONV_CH),
                    zero(bp, B_HEADS, B_N, B_N), zero(bp, 1, B_IN),
                    zero(bp, C_WIDTH), zero(bp, 3, C_WIDTH))
        xp, pst = _layer(xp, p_states, lp, gf, bsz=bp, t_pad=tp, c=64, valid_rows=64,
                         tm=256, final=final)
        s_states = (state_a_S[l], state_a_conv[l], state_b_S[l], state_b_shift[l],
                    state_c_h[l], state_c_conv[l])
        xs, sst = _layer(xs, s_states, lp, gf, bsz=bs, t_pad=ts_pad, c=ts_pad, valid_rows=ts,
                         tm=256, final=final)
        for j in range(6):
            p_new[j].append(pst[j])
            s_new[j].append(sst[j])
    y_prompt = xp.reshape(bp, tp, D_MODEL)
    y_sample = xs.reshape(bs, ts_pad, D_MODEL)[:, :ts]
    return (y_prompt, y_sample,
            jnp.stack(p_new[0]), jnp.stack(p_new[1]), jnp.stack(p_new[2]),
            jnp.stack(p_new[3]), jnp.stack(p_new[4]), jnp.stack(p_new[5]),
            jnp.stack(s_new[0]), jnp.stack(s_new[1]), jnp.stack(s_new[2]),
            jnp.stack(s_new[3]), jnp.stack(s_new[4]), jnp.stack(s_new[5]))
```

```python
import functools

import jax
import jax.numpy as jnp
from jax import lax
from jax.experimental import pallas as pl
from jax.experimental.pallas import tpu as pltpu

F32 = jnp.float32
BF16 = jnp.bfloat16
HIGHEST = lax.Precision.HIGHEST

EPS = 1e-6
D_MODEL = 1024
DEPTH = 2
A_HEADS = 4
A_DK = 128
A_DV = 128
A_QK_W = A_HEADS * A_DK
A_V_W = A_HEADS * A_DV
A_CONV_CH = 2 * A_QK_W + A_V_W
B_N = 64
B_HEADS = 8
B_W = B_HEADS * B_N
B_LORA_WA = 128
B_GATE_LORA = 128
B_IN = 3 * B_W + B_LORA_WA + B_GATE_LORA
B_LN_EPS = 64e-5
C_WIDTH = 512
C_BLOCKS = 8
C_POW = 8.0
N_BRANCH = 3
D_FF = 4 * D_MODEL

LANE = 128
SUBLANE = 8
CONV_TAPS = 4
CONV_PAD = SUBLANE

ZA_W = A_CONV_CH + A_V_W + LANE
ZA_BA = A_CONV_CH + A_V_W
ZB_W = B_IN
ZC_W = 2 * C_WIDTH
ZG_W = N_BRANCH * D_MODEL
Z_WIDTHS = (ZA_W, ZB_W, ZC_W, ZG_W)
N_IN_PAD = sum(Z_WIDTHS)

VMEM_LIMIT = 56 * 1024 * 1024


def _dot(a, b):
    return jnp.dot(a.astype(BF16), b.astype(BF16), preferred_element_type=F32)


def _dot_nt(a, b):
    return lax.dot_general(a.astype(BF16), b.astype(BF16), (((1,), (1,)), ((), ())),
                           preferred_element_type=F32)


def _dot_tn(a, b):
    return lax.dot_general(a.astype(BF16), b.astype(BF16), (((0,), (0,)), ((), ())),
                           preferred_element_type=F32)


def _sigmoid(x):
    return 1.0 / (1.0 + jnp.exp(-x))


def _silu(x):
    return x * _sigmoid(x)


def _softplus(x):
    return jnp.maximum(x, 0.0) + jnp.log1p(jnp.exp(-jnp.abs(x)))


def _gelu_tanh(x):
    return 0.5 * x * (1.0 + jnp.tanh(0.7978845608028654 * (x + 0.044715 * (x * x * x))))


def _rows(shape):
    return lax.broadcasted_iota(jnp.int32, shape, 0)


def _cols(shape):
    return lax.broadcasted_iota(jnp.int32, shape, 1)


def _rows_in(shape, period):
    r = _rows(shape)
    return r if period >= shape[0] else r & (period - 1)


def _shift_rows(x, s, fill, period):
    return jnp.where(_rows_in(x.shape, period) >= s, pltpu.roll(x, s, 0), fill)


def _cumsum_rows(x, period):
    s = 1
    while s < period:
        x = x + _shift_rows(x, s, 0.0, period)
        s *= 2
    return x


def _unit_lower_inverses(ns):
    c = ns[0].shape[0]
    eye = (_rows(ns[0].shape) == _cols(ns[0].shape)).astype(F32)
    ts = [eye + n for n in ns]
    ps = list(ns)
    s = 2
    while s < c:
        ps = [_dot(p, p) for p in ps]
        ts = [t + _dot(t, p) for t, p in zip(ts, ps)]
        s *= 2
    return ts


def _unit_lower_solves(ns, xs):
    c = ns[0].shape[0]
    xs = [x + _dot(n, x) for n, x in zip(ns, xs)]
    ps = list(ns)
    s = 2
    while s < c:
        ps = [_dot(p, p) for p in ps]
        xs = [x + _dot(p, x) for p, x in zip(ps, xs)]
        s *= 2
    return xs


def _const_spec(shape):
    nd = len(shape)
    return pl.BlockSpec(shape, lambda *_: (0,) * nd, pipeline_mode=pl.Buffered(1))


def _in_proj_kernel(x_ref, g_ref, w_ref, za_ref, zb_ref, zc_ref, zg_ref):
    x = x_ref[...]
    h = x * lax.rsqrt(jnp.mean(x * x, axis=-1, keepdims=True) + EPS) * g_ref[...]
    h = h.astype(BF16)
    off = 0
    for ref, width in zip((za_ref, zb_ref, zc_ref, zg_ref), Z_WIDTHS):
        ref[...] = jnp.dot(h, w_ref[:, off:off + width], preferred_element_type=F32)
        off += width


def _in_proj(x, g, w, tm):
    m = x.shape[0]
    return pl.pallas_call(
        _in_proj_kernel,
        grid=(m // tm,),
        in_specs=[pl.BlockSpec((tm, D_MODEL), lambda i: (i, 0)),
                  _const_spec((1, D_MODEL)),
                  _const_spec((D_MODEL, N_IN_PAD))],
        out_specs=[pl.BlockSpec((tm, wd), lambda i: (i, 0)) for wd in Z_WIDTHS],
        out_shape=[jax.ShapeDtypeStruct((m, wd), F32) for wd in Z_WIDTHS],
        compiler_params=pltpu.CompilerParams(dimension_semantics=("arbitrary",),
                                             vmem_limit_bytes=VMEM_LIMIT),
        name="in_proj",
    )(x, g, w)


def _with_history(x, xpad, first, hist_ref, ns, l, depth):
    @pl.when(first)
    def _():
        xpad[:, CONV_PAD - depth:CONV_PAD, :] = hist_ref[...]

    for s in range(ns):
        xpad[s, CONV_PAD:CONV_PAD + l, :] = x[s * l:(s + 1) * l]
    views = []
    for j in range(depth, 0, -1):
        parts = [xpad[s, CONV_PAD - j:CONV_PAD - j + l, :] for s in range(ns)]
        views.append(parts[0] if ns == 1 else jnp.concatenate(parts, axis=0))
    return views


def _carry_history(xpad, l, valid_rows, last, out_ref, depth):
    @pl.when(last)
    def _():
        out_ref[...] = xpad[:, CONV_PAD - depth + valid_rows:CONV_PAD + valid_rows, :]

    nxt = xpad[:, CONV_PAD - depth + l:CONV_PAD + l, :]
    xpad[:, CONV_PAD - depth:CONV_PAD, :] = nxt


def _causal_conv(x, xpad, w_ref, first, hist_ref, ns, l):
    y = x * w_ref[CONV_TAPS - 1:CONV_TAPS, :]
    for j, view in enumerate(_with_history(x, xpad, first, hist_ref, ns, l, CONV_TAPS - 1)):
        y = y + view * w_ref[j:j + 1, :]
    return y


def _gdn_kernel(za_ref, conv0_ref, s0_ref, convw_ref, alog_ref, dt_ref, ng_ref,
                y_ref, sout_ref, convout_ref, xpad, s_scr, *, c, nc, ns, nt, valid_rows, group):
    t = pl.program_id(1)
    first = t == 0
    last = t == nt - 1
    l = nc * c
    rows = ns * l

    @pl.when(first)
    def _():
        s_scr[...] = s0_ref[...]

    qkv = _silu(_causal_conv(za_ref[:, 0:A_CONV_CH], xpad, convw_ref, first, conv0_ref, ns, l))
    _carry_history(xpad, l, valid_rows, last, convout_ref, CONV_TAPS - 1)

    ba = za_ref[:, ZA_BA:ZA_BA + LANE]
    beta_all = _sigmoid(ba)
    g_all = -jnp.exp(alog_ref[...]) * _softplus(ba + dt_ref[...])
    masked = valid_rows < l
    if masked:
        valid = _rows_in((rows, LANE), l) < valid_rows
        beta_all = jnp.where(valid, beta_all, 0.0)
        g_all = jnp.where(valid, g_all, 0.0)
    gc_all = _cumsum_rows(g_all, c)

    rr = _rows((c, c))
    cc = _cols((c, c))
    causal = rr >= cc
    strict = rr > cc
    ng = ng_ref[...]
    pick = (_cols((SUBLANE * A_HEADS, LANE))
            == A_HEADS + _rows((SUBLANE * A_HEADS, LANE)) // SUBLANE).astype(F32)

    all_chunks = [(s, n) for n in range(nc) for s in range(ns)]
    heads = range(A_HEADS)
    state = {(s, h): s_scr[s, h] for s in range(ns) for h in heads}

    def advance(chunks):
        inst = [(s, n, h) for (s, n) in chunks for h in heads]
        lower, attn, rhs2, kdec, qg, glast = {}, {}, {}, {}, {}, {}
        for (s, n) in chunks:
            r0 = s * l + n * c
            rs = slice(r0, r0 + c)
            gc_rows = lax.dot_general(pick, gc_all[rs], (((1,), (1,)), ((), ())), precision=HIGHEST,
                                      preferred_element_type=F32)
            for h in heads:
                q = qkv[rs, h * A_DK:(h + 1) * A_DK]
                k = qkv[rs, A_QK_W + h * A_DK:A_QK_W + (h + 1) * A_DK]
                v = qkv[rs, 2 * A_QK_W + h * A_DV:2 * A_QK_W + (h + 1) * A_DV]
                q = q * lax.rsqrt(jnp.sum(q * q, axis=-1, keepdims=True) + EPS) * (A_DK ** -0.5)
                k = k * lax.rsqrt(jnp.sum(k * k, axis=-1, keepdims=True) + EPS)
                if masked:
                    valid_c = _rows((c, A_DK)) < valid_rows - n * c
                    k = jnp.where(valid_c, k, 0.0)
                    v = jnp.where(valid_c, v, 0.0)
                beta = beta_all[rs, h:h + 1]
                gc = gc_all[rs, A_HEADS + h:A_HEADS + h + 1]
                gc_row = gc_rows[SUBLANE * h:SUBLANE * h + 1, :]
                dec = jnp.where(causal, jnp.exp(jnp.where(causal, gc - gc_row, 0.0)), 0.0)
                kb = k * beta
                egc = jnp.exp(gc)
                sc = _dot_nt(jnp.concatenate([kb, q], axis=0), k)
                key = (s, n, h)
                lower[key] = jnp.where(strict, -sc[:c] * dec, 0.0)
                attn[key] = sc[c:] * dec
                rhs2[key] = jnp.concatenate([v * beta, kb * egc], axis=1)
                gc_last = gc[c - 1:c, :]
                kdec[key] = k * jnp.exp(gc_last - gc)
                glast[key] = jnp.exp(gc_last)
                qg[key] = q * egc
        tinv = dict(zip(inst, _unit_lower_inverses([lower[i] for i in inst])))
        uw = {i: _dot(tinv[i], rhs2[i]) for i in inst}
        ao = {i: _dot(attn[i], uw[i]) for i in inst}
        bm = {i: _dot_tn(kdec[i], uw[i]) for i in inst}
        for (s, n) in chunks:
            r0 = s * l + n * c
            for h in heads:
                i = (s, n, h)
                st = state[(s, h)]
                o = _dot(qg[i] - ao[i][:, A_DV:], st) + ao[i][:, :A_DV]
                state[(s, h)] = st * glast[i] - _dot(bm[i][:, A_DV:], st) + bm[i][:, :A_DV]
                o = o * lax.rsqrt(jnp.mean(o * o, axis=-1, keepdims=True) + EPS)
                z = za_ref[r0:r0 + c, A_CONV_CH + h * A_DV:A_CONV_CH + (h + 1) * A_DV]
                y_ref[r0:r0 + c, h * A_DV:(h + 1) * A_DV] = o * ng * _silu(z)

    for g in range(0, len(all_chunks), group):
        advance(all_chunks[g:g + group])
    for s in range(ns):
        for h in heads:
            s_scr[s, h] = state[(s, h)]

    @pl.when(last)
    def _():
        sout_ref[...] = s_scr[...]


def _mixer_grid(bsz, t_pad, t_valid, l, ns):
    assert t_pad % l == 0 and bsz % ns == 0 and (ns == 1 or t_pad == l)
    nt = t_pad // l
    valid_rows = t_valid - (nt - 1) * l
    assert 0 < valid_rows <= l and (nt == 1 or valid_rows == l)
    return (bsz // ns, nt), ns * l, valid_rows


def _gdn(za, conv0, s0, convw, alog_row, dt_row, ng, *, bsz, t_pad, t_valid, c, nc, ns, group):
    grid, rows, valid_rows = _mixer_grid(bsz, t_pad, t_valid, nc * c, ns)
    nt = grid[1]
    kern = functools.partial(_gdn_kernel, c=c, nc=nc, ns=ns, nt=nt, valid_rows=valid_rows,
                             group=group)
    return pl.pallas_call(
        kern,
        grid=grid,
        in_specs=[pl.BlockSpec((rows, ZA_W), lambda b, t: (b * nt + t, 0)),
                  pl.BlockSpec((ns, 3, A_CONV_CH), lambda b, t: (b, 0, 0)),
                  pl.BlockSpec((ns, A_HEADS, A_DK, A_DV), lambda b, t: (b, 0, 0, 0)),
                  _const_spec((CONV_TAPS, A_CONV_CH)),
                  _const_spec((1, LANE)),
                  _const_spec((1, LANE)),
                  _const_spec((1, A_DV))],
        out_specs=[pl.BlockSpec((rows, A_V_W), lambda b, t: (b * nt + t, 0)),
                   pl.BlockSpec((ns, A_HEADS, A_DK, A_DV), lambda b, t: (b, 0, 0, 0)),
                   pl.BlockSpec((ns, 3, A_CONV_CH), lambda b, t: (b, 0, 0))],
        out_shape=[jax.ShapeDtypeStruct((bsz * t_pad, A_V_W), F32),
                   jax.ShapeDtypeStruct((bsz, A_HEADS, A_DK, A_DV), F32),
                   jax.ShapeDtypeStruct((bsz, 3, A_CONV_CH), F32)],
        scratch_shapes=[pltpu.VMEM((ns, CONV_PAD + nc * c, A_CONV_CH), F32),
                        pltpu.VMEM((ns, A_HEADS, A_DK, A_DV), F32)],
        compiler_params=pltpu.CompilerParams(dimension_semantics=("arbitrary", "arbitrary"),
                                             vmem_limit_bytes=VMEM_LIMIT),
        name="gdn",
    )(za, conv0, s0, convw, alog_row, dt_row, ng)


def _rwkv_kernel(zb_ref, shift0_ref, s0_ref, mu_ref, w0_ref, wup_ref, a0_ref, aup_ref, gup_ref,
                 kk_ref, ka_ref, rk_ref, lnw_ref, lnb_ref, hsum_ref,
                 y_ref, sout_ref, shiftout_ref, xpad, s_scr, *, c, nc, ns, nt, valid_rows, group):
    t = pl.program_id(1)
    first = t == 0
    last = t == nt - 1
    l = nc * c
    rows = ns * l

    @pl.when(first)
    def _():
        s_scr[...] = s0_ref[...]

    zb = zb_ref[...]
    (prev,) = _with_history(zb, xpad, first, shift0_ref, ns, l, 1)
    _carry_history(xpad, l, valid_rows, last, shiftout_ref, 1)
    zs = zb + (prev - zb) * mu_ref[...]

    r_all = zs[:, 0:B_W]
    k_all = zs[:, B_W:2 * B_W]
    v_all = zs[:, 2 * B_W:3 * B_W]
    xwa = zs[:, 3 * B_W:3 * B_W + B_LORA_WA]
    xg = zs[:, 3 * B_W + B_LORA_WA:B_IN]
    w_log = -_softplus(-(w0_ref[...] + _dot(jnp.tanh(xwa), wup_ref[...]))) - 0.5
    lw = -jnp.exp(w_log)
    a_all = _sigmoid(a0_ref[...] + _dot(xwa, aup_ref[...]))
    g_all = _dot(_sigmoid(xg), gup_ref[...])
    kk_all = k_all * kk_ref[...]
    k_all = k_all * (1.0 + (a_all - 1.0) * ka_ref[...])
    masked = valid_rows < l
    if masked:
        valid = _rows_in((rows, B_W), l) < valid_rows
        lw = jnp.where(valid, lw, 0.0)
        kk_all = jnp.where(valid, kk_all, 0.0)
        k_all = jnp.where(valid, k_all, 0.0)
        v_all = jnp.where(valid, v_all, 0.0)
    gc_all = _cumsum_rows(lw, c)

    def head_sums(x):
        hi = x.astype(BF16)
        lo = (x - hi.astype(F32)).astype(BF16)
        ones = hsum_ref[...]
        return (jnp.dot(hi, ones, preferred_element_type=F32)
                + jnp.dot(lo, ones, preferred_element_type=F32))

    kkn_all = kk_all * lax.rsqrt(head_sums(kk_all * kk_all) + EPS)
    bonus_all = head_sums(r_all * k_all * rk_ref[...]) * v_all

    rr = _rows((2 * c, 2 * c))
    cc = _cols((2 * c, 2 * c))
    tt = jnp.where(rr >= c, rr - c, rr)
    ss = jnp.where(cc >= c, cc - c, cc)
    mask2 = jnp.logical_or(tt > ss, jnp.logical_and(rr >= c, tt == ss))

    all_chunks = [(s, n) for n in range(nc) for s in range(ns)]
    heads = range(B_HEADS)
    sls = [slice(h * B_N, (h + 1) * B_N) for h in heads]
    state = {(s, h): s_scr[s, h] for s in range(ns) for h in heads}

    def v_of(i):
        s, n, h = i
        return v_all[s * l + n * c:s * l + (n + 1) * c, sls[h]]

    def advance(chunks):
        inst = [(s, n, h) for (s, n) in chunks for h in heads]
        amat, rhs_uw, r_abs, tails, e_last = {}, {}, {}, {}, {}
        for (s, n) in chunks:
            r0 = s * l + n * c
            rs = slice(r0, r0 + c)
            gc = gc_all[rs]
            lwc = lw[rs]
            gmid = gc[c // 2:c // 2 + 1, :]
            glast = gc[c - 1:c, :]
            e_last[(s, n)] = jnp.exp(glast)
            r = r_all[rs]
            k = k_all[rs]
            alpha = -kkn_all[rs]
            bet = kkn_all[rs] * a_all[rs]
            e_inv = jnp.exp(gmid - gc)
            e_tail = jnp.exp(glast - gc)
            lhs_a = alpha * jnp.exp(gc - lwc - gmid)
            lhs_r = r * jnp.exp(gc - gmid)
            rhs_b = bet * e_inv
            rhs_k = k * e_inv
            abs_a = alpha * jnp.exp(gc - lwc)
            abs_r = r * jnp.exp(gc)
            tail_b = bet * e_tail
            tail_k = k * e_tail
            for h in heads:
                sl = sls[h]
                lhs = jnp.concatenate([lhs_a[:, sl], lhs_r[:, sl]], axis=0)
                rhs = jnp.concatenate([rhs_b[:, sl], rhs_k[:, sl]], axis=0)
                key = (s, n, h)
                amat[key] = jnp.where(mask2, _dot_nt(lhs, rhs), 0.0)
                rhs_uw[key] = abs_a[:, sl]
                r_abs[key] = abs_r[:, sl]
                tails[key] = jnp.concatenate([tail_b[:, sl], tail_k[:, sl]], axis=0)

        from_v = {i: _dot(amat[i][:, c:], v_of(i)) for i in inst}
        uw = dict(zip(inst, _unit_lower_solves(
            [amat[i][:c, :c] for i in inst],
            [jnp.concatenate([from_v[i][:c], rhs_uw[i]], axis=1) for i in inst])))
        ro = {i: _dot(amat[i][c:, :c], uw[i]) for i in inst}
        mp = {i: _dot_tn(uw[i][:, B_N:], tails[i][:c]) for i in inst}
        bb = {i: _dot_tn(jnp.concatenate([uw[i][:, :B_N], v_of(i)], axis=0), tails[i]) for i in inst}
        outs = []
        for (s, n) in chunks:
            per_head = []
            for h in heads:
                i = (s, n, h)
                st = state[(s, h)]
                per_head.append(_dot_nt(r_abs[i] + ro[i][:, B_N:], st) + from_v[i][c:] + ro[i][:, :B_N])
                state[(s, h)] = st * e_last[(s, n)][:, sls[h]] + _dot(st, mp[i]) + bb[i]
            outs.append(jnp.concatenate(per_head, axis=1))
        return outs

    def finish(chunks, outs):
        for (s, n), o in zip(chunks, outs):
            rs = slice(s * l + n * c, s * l + (n + 1) * c)
            d = o - head_sums(o) * (1.0 / B_N)
            var = head_sums(d * d) * (1.0 / B_N)
            o = d * lax.rsqrt(var + B_LN_EPS) * lnw_ref[...] + lnb_ref[...]
            y_ref[rs, :] = (o + bonus_all[rs]) * g_all[rs]

    pending = None
    for g in range(0, len(all_chunks), group):
        chunks = all_chunks[g:g + group]
        outs = advance(chunks)
        if pending is not None:
            finish(*pending)
        pending = (chunks, outs)
    finish(*pending)
    for s in range(ns):
        for h in heads:
            s_scr[s, h] = state[(s, h)]

    @pl.when(last)
    def _():
        sout_ref[...] = s_scr[...]


def _rwkv(zb, shift0, s0, prm, *, bsz, t_pad, t_valid, c, nc, ns, group):
    grid, rows, valid_rows = _mixer_grid(bsz, t_pad, t_valid, nc * c, ns)
    nt = grid[1]
    kern = functools.partial(_rwkv_kernel, c=c, nc=nc, ns=ns, nt=nt, valid_rows=valid_rows,
                             group=group)
    row = lambda wd: _const_spec((1, wd))
    return pl.pallas_call(
        kern,
        grid=grid,
        in_specs=[pl.BlockSpec((rows, ZB_W), lambda b, t: (b * nt + t, 0)),
                  pl.BlockSpec((ns, 1, B_IN), lambda b, t: (b, 0, 0)),
                  pl.BlockSpec((ns, B_HEADS, B_N, B_N), lambda b, t: (b, 0, 0, 0)),
                  row(B_IN), row(B_W), _const_spec((B_LORA_WA, B_W)), row(B_W),
                  _const_spec((B_LORA_WA, B_W)), _const_spec((B_GATE_LORA, B_W)),
                  row(B_W), row(B_W), row(B_W), row(B_W), row(B_W), _const_spec((B_W, B_W))],
        out_specs=[pl.BlockSpec((rows, B_W), lambda b, t: (b * nt + t, 0)),
                   pl.BlockSpec((ns, B_HEADS, B_N, B_N), lambda b, t: (b, 0, 0, 0)),
                   pl.BlockSpec((ns, 1, B_IN), lambda b, t: (b, 0, 0))],
        out_shape=[jax.ShapeDtypeStruct((bsz * t_pad, B_W), F32),
                   jax.ShapeDtypeStruct((bsz, B_HEADS, B_N, B_N), F32),
                   jax.ShapeDtypeStruct((bsz, 1, B_IN), F32)],
        scratch_shapes=[pltpu.VMEM((ns, CONV_PAD + nc * c, B_IN), F32),
                        pltpu.VMEM((ns, B_HEADS, B_N, B_N), F32)],
        compiler_params=pltpu.CompilerParams(dimension_semantics=("arbitrary", "arbitrary"),
                                             vmem_limit_bytes=VMEM_LIMIT),
        name="rwkv",
    )(zb, shift0, s0, *prm)


def _rglru_kernel(zc_ref, conv0_ref, h0_ref, convw_ref, convb_ref, wg_ref, bg_ref, l_ref,
                  y_ref, hout_ref, convout_ref, xpad, h_scr, *, l, ns, nt, valid_rows):
    t = pl.program_id(1)
    first = t == 0
    last = t == nt - 1
    rows = ns * l

    @pl.when(first)
    def _():
        h_scr[...] = h0_ref[...]

    xc = (_causal_conv(zc_ref[:, 0:C_WIDTH], xpad, convw_ref, first, conv0_ref, ns, l)
          + convb_ref[...])
    _carry_history(xpad, l, valid_rows, last, convout_ref, CONV_TAPS - 1)

    gates = _sigmoid(_dot(xc, wg_ref[...]) + bg_ref[...])
    r = gates[:, :C_WIDTH]
    i = gates[:, C_WIDTH:]
    log_a = -C_POW * r * _softplus(-l_ref[...])
    a = jnp.exp(log_a)
    b = jnp.sqrt(-jnp.tanh(log_a) * (a * a + 1.0)) * (i * xc)
    if valid_rows < l:
        valid = _rows_in((rows, C_WIDTH), l) < valid_rows
        a = jnp.where(valid, a, 1.0)
        b = jnp.where(valid, b, 0.0)
    s = 1
    while s < l:
        b = a * _shift_rows(b, s, 0.0, l) + b
        a = a * _shift_rows(a, s, 1.0, l)
        s *= 2
    gate = _gelu_tanh(zc_ref[:, C_WIDTH:])
    for q in range(ns):
        rs = slice(q * l, (q + 1) * l)
        hseq = a[rs] * h_scr[q] + b[rs]
        h_scr[q] = hseq[l - 1:l, :]
        y_ref[rs, :] = hseq * gate[rs]

    @pl.when(last)
    def _():
        hout_ref[...] = h_scr[...]


def _rglru(zc, conv0, h0, prm, *, bsz, t_pad, t_valid, l, ns):
    grid, rows, valid_rows = _mixer_grid(bsz, t_pad, t_valid, l, ns)
    nt = grid[1]
    kern = functools.partial(_rglru_kernel, l=l, ns=ns, nt=nt, valid_rows=valid_rows)
    return pl.pallas_call(
        kern,
        grid=grid,
        in_specs=[pl.BlockSpec((rows, ZC_W), lambda b, t: (b * nt + t, 0)),
                  pl.BlockSpec((ns, 3, C_WIDTH), lambda b, t: (b, 0, 0)),
                  pl.BlockSpec((ns, 1, C_WIDTH), lambda b, t: (b, 0, 0)),
                  _const_spec((CONV_TAPS, C_WIDTH)), _const_spec((1, C_WIDTH)),
                  _const_spec((C_WIDTH, 2 * C_WIDTH)), _const_spec((1, 2 * C_WIDTH)),
                  _const_spec((1, C_WIDTH))],
        out_specs=[pl.BlockSpec((rows, C_WIDTH), lambda b, t: (b * nt + t, 0)),
                   pl.BlockSpec((ns, 1, C_WIDTH), lambda b, t: (b, 0, 0)),
                   pl.BlockSpec((ns, 3, C_WIDTH), lambda b, t: (b, 0, 0))],
        out_shape=[jax.ShapeDtypeStruct((bsz * t_pad, C_WIDTH), F32),
                   jax.ShapeDtypeStruct((bsz, 1, C_WIDTH), F32),
                   jax.ShapeDtypeStruct((bsz, 3, C_WIDTH), F32)],
        scratch_shapes=[pltpu.VMEM((ns, CONV_PAD + l, C_WIDTH), F32),
                        pltpu.VMEM((ns, 1, C_WIDTH), F32)],
        compiler_params=pltpu.CompilerParams(dimension_semantics=("arbitrary", "arbitrary"),
                                             vmem_limit_bytes=VMEM_LIMIT),
        name="rglru",
    )(zc, conv0, h0, *prm)


FF_CHUNK = 1024


def _merge_ffn_kernel(x_ref, ya_ref, yb_ref, yc_ref, zg_ref, wbr_ref, wout_ref, g2_ref,
                      wup_ref, wdown_ref, gf_ref, o_ref, *, final):
    merged = None
    for n, y_ref in enumerate((ya_ref, yb_ref, yc_ref)):
        proj = _dot(y_ref[...], wbr_ref[n])
        term = _sigmoid(zg_ref[:, n * D_MODEL:(n + 1) * D_MODEL]) * proj
        merged = term if merged is None else merged + term
    x = x_ref[...] + _dot(merged, wout_ref[...])
    h = x * lax.rsqrt(jnp.mean(x * x, axis=-1, keepdims=True) + EPS) * g2_ref[...]
    h = h.astype(BF16)
    acc = x
    for j in range(0, D_FF, FF_CHUNK):
        up = jnp.dot(h, wup_ref[:, j:j + FF_CHUNK], preferred_element_type=F32)
        acc = acc + _dot(jnp.square(jnp.maximum(up, 0.0)), wdown_ref[j:j + FF_CHUNK, :])
    if final:
        acc = acc * lax.rsqrt(jnp.mean(acc * acc, axis=-1, keepdims=True) + EPS) * gf_ref[...]
    o_ref[...] = acc


def _merge_ffn(x, ya, yb, yc, zg, wbr, wout, g2, wup, wdown, gf, tm, final):
    m = x.shape[0]
    tile = lambda wd: pl.BlockSpec((tm, wd), lambda i: (i, 0))
    return pl.pallas_call(
        functools.partial(_merge_ffn_kernel, final=final),
        grid=(m // tm,),
        in_specs=[tile(D_MODEL), tile(A_V_W), tile(B_W), tile(C_WIDTH), tile(ZG_W),
                  _const_spec((N_BRANCH, A_V_W, D_MODEL)), _const_spec((D_MODEL, D_MODEL)),
                  _const_spec((1, D_MODEL)), _const_spec((D_MODEL, D_FF)),
                  _const_spec((D_FF, D_MODEL)), _const_spec((1, D_MODEL))],
        out_specs=tile(D_MODEL),
        out_shape=jax.ShapeDtypeStruct((m, D_MODEL), F32),
        compiler_params=pltpu.CompilerParams(dimension_semantics=("arbitrary",),
                                             vmem_limit_bytes=VMEM_LIMIT),
        name="merge_ffn",
    )(x, ya, yb, yc, zg, wbr, wout, g2, wup, wdown, gf)


def _block_diag(w):
    g, n, _ = w.shape
    eye = jnp.eye(g, dtype=w.dtype)
    return (eye[:, None, :, None] * w[:, :, None, :]).reshape(g * n, g * n)


def _layer_params(l, p):
    row = lambda a: a[l].reshape(1, -1).astype(F32)
    w_in = p['w_in'][l]
    a_in = A_CONV_CH + A_V_W + 2 * A_HEADS
    pad = jnp.zeros((D_MODEL, LANE - 2 * A_HEADS), w_in.dtype)
    w_in = jnp.concatenate([w_in[:, :a_in], pad, w_in[:, a_in:]], axis=1).astype(BF16)
    lane_row = lambda a: jnp.zeros((1, LANE), F32).at[0, A_HEADS:2 * A_HEADS].set(a[l])
    zeros_lora = jnp.zeros((B_LORA_WA // 2, B_W), F32)
    gdn = (p['a_conv_w'][l], lane_row(p['a_A_log']), lane_row(p['a_dt_bias']), row(p['a_norm_g']))
    rwkv = (row(p['b_mu']), row(p['b_w0']),
            jnp.concatenate([p['b_w_up'][l], zeros_lora], axis=0).astype(BF16),
            row(p['b_a0']),
            jnp.concatenate([zeros_lora, p['b_a_up'][l]], axis=0).astype(BF16),
            p['b_g_up'][l].astype(BF16),
            row(p['b_k_k']), row(p['b_k_a']), row(p['b_r_k']), row(p['b_ln_w']), row(p['b_ln_b']),
            jnp.kron(jnp.eye(B_HEADS, dtype=F32), jnp.ones((B_N, B_N), F32)).astype(BF16))
    rglru = (p['c_conv_w'][l], row(p['c_conv_b']),
             jnp.concatenate([_block_diag(p['c_wa'][l]), _block_diag(p['c_wx'][l])],
                             axis=1).astype(BF16),
             jnp.concatenate([row(p['c_ba']), row(p['c_bx'])], axis=1), row(p['c_L']))
    ffn = (p['w_branch'][l].astype(BF16), p['w_out'][l].astype(BF16), row(p['norm2_g']),
           p['w_up'][l].astype(BF16), p['w_down'][l].astype(BF16))
    return dict(norm1_g=row(p['norm1_g']), w_in=w_in, gdn=gdn, rwkv=rwkv, rglru=rglru, ffn=ffn)


PROMPT_BLOCKING = dict(c=64, nc=4, ns=1, gdn_group=4, rwkv_group=1, lru_rows=512, lru_ns=1)
SAMPLE_BLOCKING = dict(c=SUBLANE, nc=1, ns=8, gdn_group=8, rwkv_group=4, lru_rows=SUBLANE, lru_ns=16)
TOKEN_TILE = 256


def _layer(x, states, lp, gf, *, bsz, t_pad, t_valid, blocking, final):
    a_s, a_conv, b_s, b_shift, c_h, c_conv = states
    kw = dict(bsz=bsz, t_pad=t_pad, t_valid=t_valid)
    chunked = dict(c=blocking['c'], nc=blocking['nc'], ns=blocking['ns'], **kw)
    za, zb, zc, zg = _in_proj(x, lp['norm1_g'], lp['w_in'], TOKEN_TILE)
    ya, a_s, a_conv = _gdn(za, a_conv, a_s, *lp['gdn'], group=blocking['gdn_group'], **chunked)
    yb, b_s, b_shift = _rwkv(zb, b_shift, b_s, lp['rwkv'], group=blocking['rwkv_group'], **chunked)
    yc, c_h, c_conv = _rglru(zc, c_conv, c_h.reshape(bsz, 1, C_WIDTH), lp['rglru'],
                             l=blocking['lru_rows'], ns=blocking['lru_ns'], **kw)
    x = _merge_ffn(x, ya, yb, yc, zg, *lp['ffn'], gf, TOKEN_TILE, final)
    return x, (a_s, a_conv, b_s, b_shift, c_h.reshape(bsz, C_WIDTH), c_conv)


def kernel(x_prompt, x_sample, state_a_S, state_a_conv, state_b_S, state_b_shift, state_c_h, state_c_conv, norm1_g, w_in, a_conv_w, a_A_log, a_dt_bias, a_norm_g, b_mu, b_w0, b_w_up, b_a0, b_a_up, b_g_up, b_k_k, b_k_a, b_r_k, b_ln_w, b_ln_b, c_conv_w, c_conv_b, c_wa, c_ba, c_wx, c_bx, c_L, w_branch, w_out, norm2_g, w_up, w_down, final_norm_g):
    p = dict(norm1_g=norm1_g, w_in=w_in, a_conv_w=a_conv_w, a_A_log=a_A_log, a_dt_bias=a_dt_bias,
             a_norm_g=a_norm_g, b_mu=b_mu, b_w0=b_w0, b_w_up=b_w_up, b_a0=b_a0, b_a_up=b_a_up,
             b_g_up=b_g_up, b_k_k=b_k_k, b_k_a=b_k_a, b_r_k=b_r_k, b_ln_w=b_ln_w, b_ln_b=b_ln_b,
             c_conv_w=c_conv_w, c_conv_b=c_conv_b, c_wa=c_wa, c_ba=c_ba, c_wx=c_wx, c_bx=c_bx,
             c_L=c_L, w_branch=w_branch, w_out=w_out, norm2_g=norm2_g, w_up=w_up, w_down=w_down)
    bp, tp, _ = x_prompt.shape
    bs, ts, _ = x_sample.shape
    ts_pad = -(-ts // SUBLANE) * SUBLANE
    assert ts_pad == SAMPLE_BLOCKING['c'] and tp % (PROMPT_BLOCKING['c'] * PROMPT_BLOCKING['nc']) == 0
    gf = final_norm_g.reshape(1, D_MODEL)

    xp = x_prompt.reshape(bp * tp, D_MODEL)
    xs = jnp.pad(x_sample, ((0, 0), (0, ts_pad - ts), (0, 0))).reshape(bs * ts_pad, D_MODEL)
    zero = lambda *shape: jnp.zeros(shape, F32)
    p_new = [[] for _ in range(6)]
    s_new = [[] for _ in range(6)]
    for l in range(DEPTH):
        lp = _layer_params(l, p)
        final = l == DEPTH - 1
        p_states = (zero(bp, A_HEADS, A_DK, A_DV), zero(bp, 3, A_CONV_CH),
                    zero(bp, B_HEADS, B_N, B_N), zero(bp, 1, B_IN),
                    zero(bp, C_WIDTH), zero(bp, 3, C_WIDTH))
        xp, pst = _layer(xp, p_states, lp, gf, bsz=bp, t_pad=tp, t_valid=tp,
                         blocking=PROMPT_BLOCKING, final=final)
        s_states = (state_a_S[l], state_a_conv[l], state_b_S[l], state_b_shift[l],
                    state_c_h[l], state_c_conv[l])
        xs, sst = _layer(xs, s_states, lp, gf, bsz=bs, t_pad=ts_pad, t_valid=ts,
                         blocking=SAMPLE_BLOCKING, final=final)
        for j in range(6):
            p_new[j].append(pst[j])
            s_new[j].append(sst[j])
    y_prompt = xp.reshape(bp, tp, D_MODEL)
    y_sample = xs.reshape(bs, ts_pad, D_MODEL)[:, :ts]
    return (y_prompt, y_sample,
            jnp.stack(p_new[0]), jnp.stack(p_new[1]), jnp.stack(p_new[2]),
            jnp.stack(p_new[3]), jnp.stack(p_new[4]), jnp.stack(p_new[5]),
            jnp.stack(s_new[0]), jnp.stack(s_new[1]), jnp.stack(s_new[2]),
            jnp.stack(s_new[3]), jnp.stack(s_new[4]), jnp.stack(s_new[5]))
```

```python
import functools

import jax
import jax.numpy as jnp
from jax import lax
from jax.experimental import pallas as pl
from jax.experimental.pallas import tpu as pltpu

F32 = jnp.float32
BF16 = jnp.bfloat16
HIGHEST = lax.Precision.HIGHEST

EPS = 1e-6
D_MODEL = 1024
DEPTH = 2
A_HEADS = 4
A_DK = 128
A_DV = 128
A_QK_W = A_HEADS * A_DK
A_V_W = A_HEADS * A_DV
A_CONV_CH = 2 * A_QK_W + A_V_W
B_N = 64
B_HEADS = 8
B_W = B_HEADS * B_N
B_LORA_WA = 128
B_GATE_LORA = 128
B_IN = 3 * B_W + B_LORA_WA + B_GATE_LORA
B_LN_EPS = 64e-5
C_WIDTH = 512
C_BLOCKS = 8
C_POW = 8.0
N_BRANCH = 3
D_FF = 4 * D_MODEL

LANE = 128
SUBLANE = 8
CONV_TAPS = 4
CONV_PAD = SUBLANE

ZA_W = A_CONV_CH + A_V_W + LANE
ZA_BA = A_CONV_CH + A_V_W
ZB_W = B_IN
ZC_W = 2 * C_WIDTH
ZG_W = N_BRANCH * D_MODEL
Z_WIDTHS = (ZA_W, ZB_W, ZC_W, ZG_W)
N_IN_PAD = sum(Z_WIDTHS)

VMEM_LIMIT = 56 * 1024 * 1024


def _dot(a, b):
    return jnp.dot(a.astype(BF16), b.astype(BF16), preferred_element_type=F32)


def _dot_nt(a, b):
    return lax.dot_general(a.astype(BF16), b.astype(BF16), (((1,), (1,)), ((), ())),
                           preferred_element_type=F32)


def _dot_tn(a, b):
    return lax.dot_general(a.astype(BF16), b.astype(BF16), (((0,), (0,)), ((), ())),
                           preferred_element_type=F32)


def _sigmoid(x):
    return 1.0 / (1.0 + jnp.exp(-x))


def _silu(x):
    return x * _sigmoid(x)


def _softplus(x):
    return jnp.maximum(x, 0.0) + jnp.log1p(jnp.exp(-jnp.abs(x)))


def _gelu_tanh(x):
    return 0.5 * x * (1.0 + jnp.tanh(0.7978845608028654 * (x + 0.044715 * (x * x * x))))


def _rows(shape):
    return lax.broadcasted_iota(jnp.int32, shape, 0)


def _cols(shape):
    return lax.broadcasted_iota(jnp.int32, shape, 1)


def _rows_in(shape, period):
    r = _rows(shape)
    return r if period >= shape[0] else r & (period - 1)


def _shift_rows(x, s, fill, period):
    return jnp.where(_rows_in(x.shape, period) >= s, pltpu.roll(x, s, 0), fill)


def _cumsum_rows(x, period):
    s = 1
    while s < period:
        x = x + _shift_rows(x, s, 0.0, period)
        s *= 2
    return x


def _unit_lower_inverses(ns):
    c = ns[0].shape[0]
    eye = (_rows(ns[0].shape) == _cols(ns[0].shape)).astype(F32)
    ts = [eye + n for n in ns]
    ps = list(ns)
    s = 2
    while s < c:
        ps = [_dot(p, p) for p in ps]
        ts = [t + _dot(t, p) for t, p in zip(ts, ps)]
        s *= 2
    return ts


def _unit_lower_solves(ns, xs):
    c = ns[0].shape[0]
    xs = [x + _dot(n, x) for n, x in zip(ns, xs)]
    ps = list(ns)
    s = 2
    while s < c:
        ps = [_dot(p, p) for p in ps]
        xs = [x + _dot(p, x) for p, x in zip(ps, xs)]
        s *= 2
    return xs


def _param_spec(shape, layer):
    nd = len(shape)
    return pl.BlockSpec((None,) + shape, lambda *_: (layer,) + (0,) * nd, pipeline_mode=pl.Buffered(1))


def _state_spec(shape, layer):
    nd = len(shape) - 1
    return pl.BlockSpec((None,) + shape, lambda b, t: (layer, b) + (0,) * nd)


def _in_proj_kernel(x_ref, g_ref, w_ref, za_ref, zb_ref, zc_ref, zg_ref):
    x = x_ref[...]
    h = x * lax.rsqrt(jnp.mean(x * x, axis=-1, keepdims=True) + EPS) * g_ref[...]
    h = h.astype(BF16)
    off = 0
    for ref, width in zip((za_ref, zb_ref, zc_ref, zg_ref), Z_WIDTHS):
        ref[...] = jnp.dot(h, w_ref[:, off:off + width], preferred_element_type=F32)
        off += width


def _in_proj(x, g, w, tm, layer):
    m = x.shape[0]
    spec = functools.partial(_param_spec, layer=layer)
    return pl.pallas_call(
        _in_proj_kernel,
        grid=(m // tm,),
        in_specs=[pl.BlockSpec((tm, D_MODEL), lambda i: (i, 0)),
                  spec((1, D_MODEL)),
                  spec((D_MODEL, N_IN_PAD))],
        out_specs=[pl.BlockSpec((tm, wd), lambda i: (i, 0)) for wd in Z_WIDTHS],
        out_shape=[jax.ShapeDtypeStruct((m, wd), F32) for wd in Z_WIDTHS],
        compiler_params=pltpu.CompilerParams(dimension_semantics=("arbitrary",),
                                             vmem_limit_bytes=VMEM_LIMIT),
        name="in_proj",
    )(x, g, w)


def _with_history(x, xpad, first, hist_ref, ns, l, depth):
    @pl.when(first)
    def _():
        xpad[:, CONV_PAD - depth:CONV_PAD, :] = hist_ref[...]

    for s in range(ns):
        xpad[s, CONV_PAD:CONV_PAD + l, :] = x[s * l:(s + 1) * l]
    views = []
    for j in range(depth, 0, -1):
        parts = [xpad[s, CONV_PAD - j:CONV_PAD - j + l, :] for s in range(ns)]
        views.append(parts[0] if ns == 1 else jnp.concatenate(parts, axis=0))
    return views


def _carry_history(xpad, l, valid_rows, last, out_ref, depth):
    @pl.when(last)
    def _():
        out_ref[...] = xpad[:, CONV_PAD - depth + valid_rows:CONV_PAD + valid_rows, :]

    nxt = xpad[:, CONV_PAD - depth + l:CONV_PAD + l, :]
    xpad[:, CONV_PAD - depth:CONV_PAD, :] = nxt


def _causal_conv(x, xpad, w_ref, first, hist_ref, ns, l):
    y = x * w_ref[CONV_TAPS - 1:CONV_TAPS, :]
    for j, view in enumerate(_with_history(x, xpad, first, hist_ref, ns, l, CONV_TAPS - 1)):
        y = y + view * w_ref[j:j + 1, :]
    return y


def _gdn_kernel(za_ref, conv0_ref, s0_ref, convw_ref, alog_ref, dt_ref, ng_ref,
                y_ref, sout_ref, convout_ref, xpad, s_scr, *, c, nc, ns, nt, valid_rows, group):
    t = pl.program_id(1)
    first = t == 0
    last = t == nt - 1
    l = nc * c
    rows = ns * l

    @pl.when(first)
    def _():
        s_scr[...] = s0_ref[...]

    qkv = _silu(_causal_conv(za_ref[:, 0:A_CONV_CH], xpad, convw_ref, first, conv0_ref, ns, l))
    _carry_history(xpad, l, valid_rows, last, convout_ref, CONV_TAPS - 1)

    ba = za_ref[:, ZA_BA:ZA_BA + LANE]
    beta_all = _sigmoid(ba)
    g_all = -jnp.exp(alog_ref[...]) * _softplus(ba + dt_ref[...])
    masked = valid_rows < l
    if masked:
        valid = _rows_in((rows, LANE), l) < valid_rows
        beta_all = jnp.where(valid, beta_all, 0.0)
        g_all = jnp.where(valid, g_all, 0.0)
    gc_all = _cumsum_rows(g_all, c)

    rr = _rows((c, c))
    cc = _cols((c, c))
    causal = rr >= cc
    strict = rr > cc
    ng = ng_ref[...]
    pick = (_cols((SUBLANE * A_HEADS, LANE))
            == A_HEADS + _rows((SUBLANE * A_HEADS, LANE)) // SUBLANE).astype(F32)

    all_chunks = [(s, n) for n in range(nc) for s in range(ns)]
    heads = range(A_HEADS)
    state = {(s, h): s_scr[s, h] for s in range(ns) for h in heads}

    def advance(chunks):
        inst = [(s, n, h) for (s, n) in chunks for h in heads]
        lower, attn, rhs2, kdec, qg, glast = {}, {}, {}, {}, {}, {}
        for (s, n) in chunks:
            r0 = s * l + n * c
            rs = slice(r0, r0 + c)
            gc_rows = lax.dot_general(pick, gc_all[rs], (((1,), (1,)), ((), ())), precision=HIGHEST,
                                      preferred_element_type=F32)
            for h in heads:
                q = qkv[rs, h * A_DK:(h + 1) * A_DK]
                k = qkv[rs, A_QK_W + h * A_DK:A_QK_W + (h + 1) * A_DK]
                v = qkv[rs, 2 * A_QK_W + h * A_DV:2 * A_QK_W + (h + 1) * A_DV]
                q = q * lax.rsqrt(jnp.sum(q * q, axis=-1, keepdims=True) + EPS) * (A_DK ** -0.5)
                k = k * lax.rsqrt(jnp.sum(k * k, axis=-1, keepdims=True) + EPS)
                if masked:
                    valid_c = _rows((c, A_DK)) < valid_rows - n * c
                    k = jnp.where(valid_c, k, 0.0)
                    v = jnp.where(valid_c, v, 0.0)
                beta = beta_all[rs, h:h + 1]
                gc = gc_all[rs, A_HEADS + h:A_HEADS + h + 1]
                gc_row = gc_rows[SUBLANE * h:SUBLANE * h + 1, :]
                dec = jnp.where(causal, jnp.exp(jnp.where(causal, gc - gc_row, 0.0)), 0.0)
                kb = k * beta
                egc = jnp.exp(gc)
                sc = _dot_nt(jnp.concatenate([kb, q], axis=0), k)
                key = (s, n, h)
                lower[key] = jnp.where(strict, -sc[:c] * dec, 0.0)
                attn[key] = sc[c:] * dec
                rhs2[key] = jnp.concatenate([v * beta, kb * egc], axis=1)
                gc_last = gc[c - 1:c, :]
                kdec[key] = k * jnp.exp(gc_last - gc)
                glast[key] = jnp.exp(gc_last)
                qg[key] = q * egc
        tinv = dict(zip(inst, _unit_lower_inverses([lower[i] for i in inst])))
        uw = {i: _dot(tinv[i], rhs2[i]) for i in inst}
        ao = {i: _dot(attn[i], uw[i]) for i in inst}
        bm = {i: _dot_tn(kdec[i], uw[i]) for i in inst}
        for (s, n) in chunks:
            r0 = s * l + n * c
            for h in heads:
                i = (s, n, h)
                st = state[(s, h)]
                o = _dot(qg[i] - ao[i][:, A_DV:], st) + ao[i][:, :A_DV]
                state[(s, h)] = st * glast[i] - _dot(bm[i][:, A_DV:], st) + bm[i][:, :A_DV]
                o = o * lax.rsqrt(jnp.mean(o * o, axis=-1, keepdims=True) + EPS)
                z = za_ref[r0:r0 + c, A_CONV_CH + h * A_DV:A_CONV_CH + (h + 1) * A_DV]
                y_ref[r0:r0 + c, h * A_DV:(h + 1) * A_DV] = o * ng * _silu(z)

    for g in range(0, len(all_chunks), group):
        advance(all_chunks[g:g + group])
    for s in range(ns):
        for h in heads:
            s_scr[s, h] = state[(s, h)]

    @pl.when(last)
    def _():
        sout_ref[...] = s_scr[...]


def _mixer_grid(bsz, t_pad, t_valid, l, ns):
    assert t_pad % l == 0 and bsz % ns == 0 and (ns == 1 or t_pad == l)
    nt = t_pad // l
    valid_rows = t_valid - (nt - 1) * l
    assert 0 < valid_rows <= l and (nt == 1 or valid_rows == l)
    return (bsz // ns, nt), ns * l, valid_rows


def _gdn(za, conv0, s0, convw, alog_row, dt_row, ng, *, bsz, t_pad, t_valid, c, nc, ns, group,
         layer, state_layer):
    grid, rows, valid_rows = _mixer_grid(bsz, t_pad, t_valid, nc * c, ns)
    spec = functools.partial(_param_spec, layer=layer)
    nt = grid[1]
    kern = functools.partial(_gdn_kernel, c=c, nc=nc, ns=ns, nt=nt, valid_rows=valid_rows,
                             group=group)
    return pl.pallas_call(
        kern,
        grid=grid,
        in_specs=[pl.BlockSpec((rows, ZA_W), lambda b, t: (b * nt + t, 0)),
                  _state_spec((ns, 3, A_CONV_CH), state_layer),
                  _state_spec((ns, A_HEADS, A_DK, A_DV), state_layer),
                  spec((CONV_TAPS, A_CONV_CH)),
                  spec((1, LANE)),
                  spec((1, LANE)),
                  spec((1, A_DV))],
        out_specs=[pl.BlockSpec((rows, A_V_W), lambda b, t: (b * nt + t, 0)),
                   pl.BlockSpec((ns, A_HEADS, A_DK, A_DV), lambda b, t: (b, 0, 0, 0)),
                   pl.BlockSpec((ns, 3, A_CONV_CH), lambda b, t: (b, 0, 0))],
        out_shape=[jax.ShapeDtypeStruct((bsz * t_pad, A_V_W), F32),
                   jax.ShapeDtypeStruct((bsz, A_HEADS, A_DK, A_DV), F32),
                   jax.ShapeDtypeStruct((bsz, 3, A_CONV_CH), F32)],
        scratch_shapes=[pltpu.VMEM((ns, CONV_PAD + nc * c, A_CONV_CH), F32),
                        pltpu.VMEM((ns, A_HEADS, A_DK, A_DV), F32)],
        compiler_params=pltpu.CompilerParams(dimension_semantics=("arbitrary", "arbitrary"),
                                             vmem_limit_bytes=VMEM_LIMIT),
        name="gdn",
    )(za, conv0, s0, convw, alog_row, dt_row, ng)


def _rwkv_kernel(zb_ref, shift0_ref, s0_ref, mu_ref, w0_ref, wup_ref, a0_ref, aup_ref, gup_ref,
                 kk_ref, ka_ref, rk_ref, lnw_ref, lnb_ref, hsum_ref,
                 y_ref, sout_ref, shiftout_ref, xpad, s_scr, prep, *, c, nc, ns, nt, valid_rows, group):
    t = pl.program_id(1)
    first = t == 0
    last = t == nt - 1
    l = nc * c
    rows = ns * l

    @pl.when(first)
    def _():
        s_scr[...] = s0_ref[...]

    zb = zb_ref[...]
    (prev,) = _with_history(zb, xpad, first, shift0_ref, ns, l, 1)
    _carry_history(xpad, l, valid_rows, last, shiftout_ref, 1)
    zs = zb + (prev - zb) * mu_ref[...]

    r_all = zs[:, 0:B_W]
    k_all = zs[:, B_W:2 * B_W]
    v_all = zs[:, 2 * B_W:3 * B_W]
    xwa = zs[:, 3 * B_W:3 * B_W + B_LORA_WA]
    xg = zs[:, 3 * B_W + B_LORA_WA:B_IN]
    w_log = -_softplus(-(w0_ref[...] + _dot(jnp.tanh(xwa), wup_ref[...]))) - 0.5
    lw = -jnp.exp(w_log)
    a_all = _sigmoid(a0_ref[...] + _dot(xwa, aup_ref[...]))
    g_all = _dot(_sigmoid(xg), gup_ref[...])
    kk_all = k_all * kk_ref[...]
    k_all = k_all * (1.0 + (a_all - 1.0) * ka_ref[...])
    masked = valid_rows < l
    if masked:
        valid = _rows_in((rows, B_W), l) < valid_rows
        lw = jnp.where(valid, lw, 0.0)
        kk_all = jnp.where(valid, kk_all, 0.0)
        k_all = jnp.where(valid, k_all, 0.0)
        v_all = jnp.where(valid, v_all, 0.0)
    gc_all = _cumsum_rows(lw, c)

    def head_sums(x):
        hi = x.astype(BF16)
        lo = (x - hi.astype(F32)).astype(BF16)
        ones = hsum_ref[...]
        return (jnp.dot(hi, ones, preferred_element_type=F32)
                + jnp.dot(lo, ones, preferred_element_type=F32))

    kkn_all = kk_all * lax.rsqrt(head_sums(kk_all * kk_all) + EPS)
    bonus_all = head_sums(r_all * k_all * rk_ref[...]) * v_all

    rr = _rows((2 * c, 2 * c))
    cc = _cols((2 * c, 2 * c))
    tt = jnp.where(rr >= c, rr - c, rr)
    ss = jnp.where(cc >= c, cc - c, cc)
    mask2 = jnp.logical_or(tt > ss, jnp.logical_and(rr >= c, tt == ss))

    prep_names = ('r', 'k', 'v', 'kkn', 'a', 'lw', 'gc', 'bonus', 'g')
    for idx, val in enumerate((r_all, k_all, v_all, kkn_all, a_all, lw, gc_all, bonus_all, g_all)):
        prep[idx] = val
    heads = range(B_HEADS)
    sls = [slice(h * B_N, (h + 1) * B_N) for h in heads]

    def at_rows(name, rs):
        return prep[prep_names.index(name), rs, :]

    def advance(g, carry):
        members = range(group)
        inst = [(j, h) for j in members for h in heads]
        amat, rhs_uw, r_abs, tails, e_last, vs, rows_of, seq_of = {}, {}, {}, {}, {}, {}, {}, {}
        for j in members:
            q = g * group + j
            rs = pl.ds(pl.multiple_of(q * c, c), c)
            rows_of[j] = rs
            seq_of[j] = 0 if ns == 1 else q
            gc = at_rows('gc', rs)
            lwc = at_rows('lw', rs)
            gmid = gc[c // 2:c // 2 + 1, :]
            glast = gc[c - 1:c, :]
            e_last[j] = jnp.exp(glast)
            r = at_rows('r', rs)
            k = at_rows('k', rs)
            v = at_rows('v', rs)
            alpha = -at_rows('kkn', rs)
            bet = at_rows('kkn', rs) * at_rows('a', rs)
            e_inv = jnp.exp(gmid - gc)
            e_tail = jnp.exp(glast - gc)
            lhs_a = alpha * jnp.exp(gc - lwc - gmid)
            lhs_r = r * jnp.exp(gc - gmid)
            rhs_b = bet * e_inv
            rhs_k = k * e_inv
            abs_a = alpha * jnp.exp(gc - lwc)
            abs_r = r * jnp.exp(gc)
            tail_b = bet * e_tail
            tail_k = k * e_tail
            for h in heads:
                sl = sls[h]
                lhs = jnp.concatenate([lhs_a[:, sl], lhs_r[:, sl]], axis=0)
                rhs = jnp.concatenate([rhs_b[:, sl], rhs_k[:, sl]], axis=0)
                key = (j, h)
                amat[key] = jnp.where(mask2, _dot_nt(lhs, rhs), 0.0)
                rhs_uw[key] = abs_a[:, sl]
                r_abs[key] = abs_r[:, sl]
                tails[key] = jnp.concatenate([tail_b[:, sl], tail_k[:, sl]], axis=0)
                vs[key] = v[:, sl]

        from_v = {i: _dot(amat[i][:, c:], vs[i]) for i in inst}
        uw = dict(zip(inst, _unit_lower_solves(
            [amat[i][:c, :c] for i in inst],
            [jnp.concatenate([from_v[i][:c], rhs_uw[i]], axis=1) for i in inst])))
        ro = {i: _dot(amat[i][c:, :c], uw[i]) for i in inst}
        mp = {i: _dot_tn(uw[i][:, B_N:], tails[i][:c]) for i in inst}
        bb = {i: _dot_tn(jnp.concatenate([uw[i][:, :B_N], vs[i]], axis=0), tails[i]) for i in inst}
        state = {}
        for j in members:
            per_head = []
            for h in heads:
                i = (j, h)
                skey = (0 if ns == 1 else j, h)
                st = state[skey] if skey in state else s_scr[seq_of[j], h]
                per_head.append(_dot_nt(r_abs[i] + ro[i][:, B_N:], st) + from_v[i][c:] + ro[i][:, :B_N])
                state[skey] = st * e_last[j][:, sls[h]] + _dot(st, mp[i]) + bb[i]
            o = jnp.concatenate(per_head, axis=1)
            d = o - head_sums(o) * (1.0 / B_N)
            var = head_sums(d * d) * (1.0 / B_N)
            o = d * lax.rsqrt(var + B_LN_EPS) * lnw_ref[...] + lnb_ref[...]
            y_ref[rows_of[j], :] = (o + at_rows('bonus', rows_of[j])) * at_rows('g', rows_of[j])
        for (sk, h), st in state.items():
            s_scr[seq_of[sk], h] = st
        return carry

    assert (ns * nc) % group == 0 and (ns == 1 or nc == 1)
    lax.fori_loop(0, ns * nc // group, advance, 0)

    @pl.when(last)
    def _():
        sout_ref[...] = s_scr[...]


def _rwkv(zb, shift0, s0, prm, *, bsz, t_pad, t_valid, c, nc, ns, group, layer, state_layer):
    grid, rows, valid_rows = _mixer_grid(bsz, t_pad, t_valid, nc * c, ns)
    nt = grid[1]
    kern = functools.partial(_rwkv_kernel, c=c, nc=nc, ns=ns, nt=nt, valid_rows=valid_rows,
                             group=group)
    spec = functools.partial(_param_spec, layer=layer)
    row = lambda wd: spec((1, wd))
    return pl.pallas_call(
        kern,
        grid=grid,
        in_specs=[pl.BlockSpec((rows, ZB_W), lambda b, t: (b * nt + t, 0)),
                  _state_spec((ns, 1, B_IN), state_layer),
                  _state_spec((ns, B_HEADS, B_N, B_N), state_layer),
                  row(B_IN), row(B_W), spec((B_LORA_WA, B_W)), row(B_W),
                  spec((B_LORA_WA, B_W)), spec((B_GATE_LORA, B_W)),
                  row(B_W), row(B_W), row(B_W), row(B_W), row(B_W), _param_spec((B_W, B_W), 0)],
        out_specs=[pl.BlockSpec((rows, B_W), lambda b, t: (b * nt + t, 0)),
                   pl.BlockSpec((ns, B_HEADS, B_N, B_N), lambda b, t: (b, 0, 0, 0)),
                   pl.BlockSpec((ns, 1, B_IN), lambda b, t: (b, 0, 0))],
        out_shape=[jax.ShapeDtypeStruct((bsz * t_pad, B_W), F32),
                   jax.ShapeDtypeStruct((bsz, B_HEADS, B_N, B_N), F32),
                   jax.ShapeDtypeStruct((bsz, 1, B_IN), F32)],
        scratch_shapes=[pltpu.VMEM((ns, CONV_PAD + nc * c, B_IN), F32),
                        pltpu.VMEM((ns, B_HEADS, B_N, B_N), F32),
                        pltpu.VMEM((9, rows, B_W), F32)],
        compiler_params=pltpu.CompilerParams(dimension_semantics=("arbitrary", "arbitrary"),
                                             vmem_limit_bytes=VMEM_LIMIT),
        name="rwkv",
    )(zb, shift0, s0, *prm)


def _rglru_kernel(zc_ref, conv0_ref, h0_ref, convw_ref, convb_ref, wg_ref, bg_ref, l_ref,
                  y_ref, hout_ref, convout_ref, xpad, h_scr, *, l, ns, nt, valid_rows):
    t = pl.program_id(1)
    first = t == 0
    last = t == nt - 1
    rows = ns * l

    @pl.when(first)
    def _():
        h_scr[...] = h0_ref[...]

    xc = (_causal_conv(zc_ref[:, 0:C_WIDTH], xpad, convw_ref, first, conv0_ref, ns, l)
          + convb_ref[...])
    _carry_history(xpad, l, valid_rows, last, convout_ref, CONV_TAPS - 1)

    gates = _sigmoid(_dot(xc, wg_ref[...]) + bg_ref[...])
    r = gates[:, :C_WIDTH]
    i = gates[:, C_WIDTH:]
    log_a = -C_POW * r * _softplus(-l_ref[...])
    a = jnp.exp(log_a)
    b = jnp.sqrt(-jnp.tanh(log_a) * (a * a + 1.0)) * (i * xc)
    if valid_rows < l:
        valid = _rows_in((rows, C_WIDTH), l) < valid_rows
        a = jnp.where(valid, a, 1.0)
        b = jnp.where(valid, b, 0.0)
    s = 1
    while s < l:
        b = a * _shift_rows(b, s, 0.0, l) + b
        a = a * _shift_rows(a, s, 1.0, l)
        s *= 2
    gate = _gelu_tanh(zc_ref[:, C_WIDTH:])
    for q in range(ns):
        rs = slice(q * l, (q + 1) * l)
        hseq = a[rs] * h_scr[q] + b[rs]
        h_scr[q] = hseq[l - 1:l, :]
        y_ref[rs, :] = hseq * gate[rs]

    @pl.when(last)
    def _():
        hout_ref[...] = h_scr[...]


def _rglru(zc, conv0, h0, prm, *, bsz, t_pad, t_valid, l, ns, layer, state_layer):
    grid, rows, valid_rows = _mixer_grid(bsz, t_pad, t_valid, l, ns)
    nt = grid[1]
    kern = functools.partial(_rglru_kernel, l=l, ns=ns, nt=nt, valid_rows=valid_rows)
    spec = functools.partial(_param_spec, layer=layer)
    return pl.pallas_call(
        kern,
        grid=grid,
        in_specs=[pl.BlockSpec((rows, ZC_W), lambda b, t: (b * nt + t, 0)),
                  _state_spec((ns, 3, C_WIDTH), state_layer),
                  _state_spec((ns, 1, C_WIDTH), state_layer),
                  spec((CONV_TAPS, C_WIDTH)), spec((1, C_WIDTH)),
                  spec((C_WIDTH, 2 * C_WIDTH)), spec((1, 2 * C_WIDTH)),
                  spec((1, C_WIDTH))],
        out_specs=[pl.BlockSpec((rows, C_WIDTH), lambda b, t: (b * nt + t, 0)),
                   pl.BlockSpec((ns, 1, C_WIDTH), lambda b, t: (b, 0, 0)),
                   pl.BlockSpec((ns, 3, C_WIDTH), lambda b, t: (b, 0, 0))],
        out_shape=[jax.ShapeDtypeStruct((bsz * t_pad, C_WIDTH), F32),
                   jax.ShapeDtypeStruct((bsz, 1, C_WIDTH), F32),
                   jax.ShapeDtypeStruct((bsz, 3, C_WIDTH), F32)],
        scratch_shapes=[pltpu.VMEM((ns, CONV_PAD + l, C_WIDTH), F32),
                        pltpu.VMEM((ns, 1, C_WIDTH), F32)],
        compiler_params=pltpu.CompilerParams(dimension_semantics=("arbitrary", "arbitrary"),
                                             vmem_limit_bytes=VMEM_LIMIT),
        name="rglru",
    )(zc, conv0, h0, *prm)


FF_CHUNK = 1024


def _merge_ffn_kernel(x_ref, ya_ref, yb_ref, yc_ref, zg_ref, wbr_ref, wout_ref, g2_ref,
                      wup_ref, wdown_ref, gf_ref, o_ref, *, final):
    merged = None
    for n, y_ref in enumerate((ya_ref, yb_ref, yc_ref)):
        proj = _dot(y_ref[...], wbr_ref[n])
        term = _sigmoid(zg_ref[:, n * D_MODEL:(n + 1) * D_MODEL]) * proj
        merged = term if merged is None else merged + term
    x = x_ref[...] + _dot(merged, wout_ref[...])
    h = x * lax.rsqrt(jnp.mean(x * x, axis=-1, keepdims=True) + EPS) * g2_ref[...]
    h = h.astype(BF16)
    def ff_chunk(j, acc):
        cols = pl.ds(pl.multiple_of(j * FF_CHUNK, FF_CHUNK), FF_CHUNK)
        up = jnp.dot(h, wup_ref[:, cols], preferred_element_type=F32)
        return acc + _dot(jnp.square(jnp.maximum(up, 0.0)), wdown_ref[cols, :])

    acc = lax.fori_loop(0, D_FF // FF_CHUNK, ff_chunk, x)
    if final:
        acc = acc * lax.rsqrt(jnp.mean(acc * acc, axis=-1, keepdims=True) + EPS) * gf_ref[...]
    o_ref[...] = acc


def _merge_ffn(x, ya, yb, yc, zg, wbr, wout, g2, wup, wdown, gf, tm, final, layer):
    m = x.shape[0]
    spec = functools.partial(_param_spec, layer=layer)
    tile = lambda wd: pl.BlockSpec((tm, wd), lambda i: (i, 0))
    return pl.pallas_call(
        functools.partial(_merge_ffn_kernel, final=final),
        grid=(m // tm,),
        in_specs=[tile(D_MODEL), tile(A_V_W), tile(B_W), tile(C_WIDTH), tile(ZG_W),
                  spec((N_BRANCH, A_V_W, D_MODEL)), spec((D_MODEL, D_MODEL)),
                  spec((1, D_MODEL)), spec((D_MODEL, D_FF)),
                  spec((D_FF, D_MODEL)), _param_spec((1, D_MODEL), 0)],
        out_specs=tile(D_MODEL),
        out_shape=jax.ShapeDtypeStruct((m, D_MODEL), F32),
        compiler_params=pltpu.CompilerParams(dimension_semantics=("arbitrary",),
                                             vmem_limit_bytes=VMEM_LIMIT),
        name="merge_ffn",
    )(x, ya, yb, yc, zg, wbr, wout, g2, wup, wdown, gf)


def _block_diag(w):
    g, n, _ = w.shape
    eye = jnp.eye(g, dtype=w.dtype)
    return (eye[:, None, :, None] * w[:, :, None, :]).reshape(g * n, g * n)


def _prepare_params(p):
    depth = p['w_in'].shape[0]
    row = lambda a: a.reshape(depth, 1, -1).astype(F32)
    a_in = A_CONV_CH + A_V_W + 2 * A_HEADS
    pad = jnp.zeros((depth, D_MODEL, LANE - 2 * A_HEADS), p['w_in'].dtype)
    w_in = jnp.concatenate([p['w_in'][:, :, :a_in], pad, p['w_in'][:, :, a_in:]], axis=2).astype(BF16)
    lane_row = lambda a: jnp.zeros((depth, 1, LANE), F32).at[:, 0, A_HEADS:2 * A_HEADS].set(a)
    zeros_lora = jnp.zeros((depth, B_LORA_WA // 2, B_W), F32)
    block_diag = jax.vmap(_block_diag)
    gdn = (p['a_conv_w'], lane_row(p['a_A_log']), lane_row(p['a_dt_bias']), row(p['a_norm_g']))
    rwkv = (row(p['b_mu']), row(p['b_w0']),
            jnp.concatenate([p['b_w_up'], zeros_lora], axis=1).astype(BF16),
            row(p['b_a0']),
            jnp.concatenate([zeros_lora, p['b_a_up']], axis=1).astype(BF16),
            p['b_g_up'].astype(BF16),
            row(p['b_k_k']), row(p['b_k_a']), row(p['b_r_k']), row(p['b_ln_w']), row(p['b_ln_b']),
            jnp.kron(jnp.eye(B_HEADS, dtype=F32), jnp.ones((B_N, B_N), F32)).astype(BF16)[None])
    rglru = (p['c_conv_w'], row(p['c_conv_b']),
             jnp.concatenate([block_diag(p['c_wa']), block_diag(p['c_wx'])], axis=2).astype(BF16),
             jnp.concatenate([row(p['c_ba']), row(p['c_bx'])], axis=2), row(p['c_L']))
    ffn = (p['w_branch'].astype(BF16), p['w_out'].astype(BF16), row(p['norm2_g']),
           p['w_up'].astype(BF16), p['w_down'].astype(BF16))
    return dict(norm1_g=row(p['norm1_g']), w_in=w_in, gdn=gdn, rwkv=rwkv, rglru=rglru, ffn=ffn)


PROMPT_BLOCKING = dict(c=64, nc=4, ns=1, gdn_group=4, rwkv_group=4, lru_rows=512, lru_ns=1)
SAMPLE_BLOCKING = dict(c=SUBLANE, nc=1, ns=8, gdn_group=8, rwkv_group=4, lru_rows=SUBLANE, lru_ns=16)
TOKEN_TILE = 256


def _layer(x, states, lp, gf, *, layer, state_layer, bsz, t_pad, t_valid, blocking, final):
    a_s, a_conv, b_s, b_shift, c_h, c_conv = states
    kw = dict(bsz=bsz, t_pad=t_pad, t_valid=t_valid, layer=layer, state_layer=state_layer)
    chunked = dict(c=blocking['c'], nc=blocking['nc'], ns=blocking['ns'], **kw)
    za, zb, zc, zg = _in_proj(x, lp['norm1_g'], lp['w_in'], TOKEN_TILE, layer)
    ya, a_s, a_conv = _gdn(za, a_conv, a_s, *lp['gdn'], group=blocking['gdn_group'], **chunked)
    yb, b_s, b_shift = _rwkv(zb, b_shift, b_s, lp['rwkv'], group=blocking['rwkv_group'], **chunked)
    yc, c_h, c_conv = _rglru(zc, c_conv, c_h.reshape(-1, bsz, 1, C_WIDTH), lp['rglru'],
                             l=blocking['lru_rows'], ns=blocking['lru_ns'], **kw)
    x = _merge_ffn(x, ya, yb, yc, zg, *lp['ffn'], gf, TOKEN_TILE, final, layer)
    return x, (a_s, a_conv, b_s, b_shift, c_h.reshape(bsz, C_WIDTH), c_conv)


def kernel(x_prompt, x_sample, state_a_S, state_a_conv, state_b_S, state_b_shift, state_c_h, state_c_conv, norm1_g, w_in, a_conv_w, a_A_log, a_dt_bias, a_norm_g, b_mu, b_w0, b_w_up, b_a0, b_a_up, b_g_up, b_k_k, b_k_a, b_r_k, b_ln_w, b_ln_b, c_conv_w, c_conv_b, c_wa, c_ba, c_wx, c_bx, c_L, w_branch, w_out, norm2_g, w_up, w_down, final_norm_g):
    p = dict(norm1_g=norm1_g, w_in=w_in, a_conv_w=a_conv_w, a_A_log=a_A_log, a_dt_bias=a_dt_bias,
             a_norm_g=a_norm_g, b_mu=b_mu, b_w0=b_w0, b_w_up=b_w_up, b_a0=b_a0, b_a_up=b_a_up,
             b_g_up=b_g_up, b_k_k=b_k_k, b_k_a=b_k_a, b_r_k=b_r_k, b_ln_w=b_ln_w, b_ln_b=b_ln_b,
             c_conv_w=c_conv_w, c_conv_b=c_conv_b, c_wa=c_wa, c_ba=c_ba, c_wx=c_wx, c_bx=c_bx,
             c_L=c_L, w_branch=w_branch, w_out=w_out, norm2_g=norm2_g, w_up=w_up, w_down=w_down)
    bp, tp, _ = x_prompt.shape
    bs, ts, _ = x_sample.shape
    ts_pad = -(-ts // SUBLANE) * SUBLANE
    assert ts_pad == SAMPLE_BLOCKING['c'] and tp % (PROMPT_BLOCKING['c'] * PROMPT_BLOCKING['nc']) == 0
    gf = final_norm_g.reshape(1, 1, D_MODEL)

    xp = x_prompt.reshape(bp * tp, D_MODEL)
    xs = jnp.pad(x_sample, ((0, 0), (0, ts_pad - ts), (0, 0))).reshape(bs * ts_pad, D_MODEL)
    zero = lambda *shape: jnp.zeros((1,) + shape, F32)
    p_states = (zero(bp, A_HEADS, A_DK, A_DV), zero(bp, 3, A_CONV_CH),
                zero(bp, B_HEADS, B_N, B_N), zero(bp, 1, B_IN),
                zero(bp, C_WIDTH), zero(bp, 3, C_WIDTH))
    s_states = (state_a_S, state_a_conv, state_b_S, state_b_shift, state_c_h, state_c_conv)
    lp = _prepare_params(p)
    p_new = [[] for _ in range(6)]
    s_new = [[] for _ in range(6)]
    for l in range(DEPTH):
        final = l == DEPTH - 1
        xp, pst = _layer(xp, p_states, lp, gf, layer=l, state_layer=0, bsz=bp, t_pad=tp, t_valid=tp,
                         blocking=PROMPT_BLOCKING, final=final)
        xs, sst = _layer(xs, s_states, lp, gf, layer=l, state_layer=l, bsz=bs, t_pad=ts_pad,
                         t_valid=ts, blocking=SAMPLE_BLOCKING, final=final)
        for j in range(6):
            p_new[j].append(pst[j])
            s_new[j].append(sst[j])
    y_prompt = xp.reshape(bp, tp, D_MODEL)
    y_sample = xs.reshape(bs, ts_pad, D_MODEL)[:, :ts]
    return (y_prompt, y_sample,
            jnp.stack(p_new[0]), jnp.stack(p_new[1]), jnp.stack(p_new[2]),
            jnp.stack(p_new[3]), jnp.stack(p_new[4]), jnp.stack(p_new[5]),
            jnp.stack(s_new[0]), jnp.stack(s_new[1]), jnp.stack(s_new[2]),
            jnp.stack(s_new[3]), jnp.stack(s_new[4]), jnp.stack(s_new[5]))
```

```python
import functools

import jax
import jax.numpy as jnp
from jax import lax
from jax.experimental import pallas as pl
from jax.experimental.pallas import tpu as pltpu

F32 = jnp.float32
BF16 = jnp.bfloat16
HIGHEST = lax.Precision.HIGHEST

EPS = 1e-6
D_MODEL = 1024
DEPTH = 2
A_HEADS = 4
A_DK = 128
A_DV = 128
A_QK_W = A_HEADS * A_DK
A_V_W = A_HEADS * A_DV
A_CONV_CH = 2 * A_QK_W + A_V_W
B_N = 64
B_HEADS = 8
B_W = B_HEADS * B_N
B_LORA_WA = 128
B_GATE_LORA = 128
B_IN = 3 * B_W + B_LORA_WA + B_GATE_LORA
B_LN_EPS = 64e-5
C_WIDTH = 512
C_BLOCKS = 8
C_POW = 8.0
N_BRANCH = 3
D_FF = 4 * D_MODEL

LANE = 128
SUBLANE = 8
CONV_TAPS = 4
CONV_PAD = SUBLANE

ZA_W = A_CONV_CH + A_V_W + LANE
ZA_BA = A_CONV_CH + A_V_W
ZB_W = B_IN
ZC_W = 2 * C_WIDTH
ZG_W = N_BRANCH * D_MODEL
Z_WIDTHS = (ZA_W, ZB_W, ZC_W, ZG_W)
N_IN_PAD = sum(Z_WIDTHS)

VMEM_LIMIT = 56 * 1024 * 1024


def _dot(a, b):
    return jnp.dot(a.astype(BF16), b.astype(BF16), preferred_element_type=F32)


def _dot_nt(a, b):
    return lax.dot_general(a.astype(BF16), b.astype(BF16), (((1,), (1,)), ((), ())),
                           preferred_element_type=F32)


def _dot_tn(a, b):
    return lax.dot_general(a.astype(BF16), b.astype(BF16), (((0,), (0,)), ((), ())),
                           preferred_element_type=F32)


def _sigmoid(x):
    return 1.0 / (1.0 + jnp.exp(-x))


def _silu(x):
    return x * _sigmoid(x)


def _softplus(x):
    return jnp.maximum(x, 0.0) + jnp.log1p(jnp.exp(-jnp.abs(x)))


def _gelu_tanh(x):
    return 0.5 * x * (1.0 + jnp.tanh(0.7978845608028654 * (x + 0.044715 * (x * x * x))))


def _rows(shape):
    return lax.broadcasted_iota(jnp.int32, shape, 0)


def _cols(shape):
    return lax.broadcasted_iota(jnp.int32, shape, 1)


def _rows_in(shape, period):
    r = _rows(shape)
    return r if period >= shape[0] else r & (period - 1)


def _shift_rows(x, s, fill, period):
    return jnp.where(_rows_in(x.shape, period) >= s, pltpu.roll(x, s, 0), fill)


def _cumsum_rows(x, period):
    s = 1
    while s < period:
        x = x + _shift_rows(x, s, 0.0, period)
        s *= 2
    return x


def _unit_lower_inverses(ns):
    c = ns[0].shape[0]
    eye = (_rows(ns[0].shape) == _cols(ns[0].shape)).astype(F32)
    ts = [eye + n for n in ns]
    ps = list(ns)
    s = 2
    while s < c:
        ps = [_dot(p, p) for p in ps]
        ts = [t + _dot(t, p) for t, p in zip(ts, ps)]
        s *= 2
    return ts


def _unit_lower_solves(ns, xs):
    c = ns[0].shape[0]
    xs = [x + _dot(n, x) for n, x in zip(ns, xs)]
    ps = list(ns)
    s = 2
    while s < c:
        ps = [_dot(p, p) for p in ps]
        xs = [x + _dot(p, x) for p, x in zip(ps, xs)]
        s *= 2
    return xs


def _param_spec(shape, layer):
    nd = len(shape)
    return pl.BlockSpec((None,) + shape, lambda *_: (layer,) + (0,) * nd, pipeline_mode=pl.Buffered(1))


def _stacked_spec(layers, shape):
    nd = len(shape) - 1
    return pl.BlockSpec((layers,) + shape, lambda b, t: (0, b) + (0,) * nd)


def _emit_state(last, out_ref, prev_ref, new_ref):
    @pl.when(last)
    def _():
        n_prev = out_ref.shape[0] - 1
        if n_prev:
            out_ref[0:n_prev] = prev_ref[...]
        out_ref[n_prev] = new_ref[...]


def _state_spec(shape, layer):
    nd = len(shape) - 1
    return pl.BlockSpec((None,) + shape, lambda b, t: (layer, b) + (0,) * nd)


def _in_proj_kernel(x_ref, g_ref, w_ref, za_ref, zb_ref, zc_ref, zg_ref):
    x = x_ref[...]
    h = x * lax.rsqrt(jnp.mean(x * x, axis=-1, keepdims=True) + EPS) * g_ref[...]
    h = h.astype(BF16)
    off = 0
    for ref, width in zip((za_ref, zb_ref, zc_ref, zg_ref), Z_WIDTHS):
        ref[...] = jnp.dot(h, w_ref[:, off:off + width], preferred_element_type=F32)
        off += width


def _in_proj(x, g, w, tm, layer):
    m = x.shape[0]
    spec = functools.partial(_param_spec, layer=layer)
    return pl.pallas_call(
        _in_proj_kernel,
        grid=(m // tm,),
        in_specs=[pl.BlockSpec((tm, D_MODEL), lambda i: (i, 0)),
                  spec((1, D_MODEL)),
                  spec((D_MODEL, N_IN_PAD))],
        out_specs=[pl.BlockSpec((tm, wd), lambda i: (i, 0)) for wd in Z_WIDTHS],
        out_shape=[jax.ShapeDtypeStruct((m, wd), F32) for wd in Z_WIDTHS],
        compiler_params=pltpu.CompilerParams(dimension_semantics=("arbitrary",),
                                             vmem_limit_bytes=VMEM_LIMIT),
        name="in_proj",
    )(x, g, w)


def _with_history(x, xpad, first, hist_ref, ns, l, depth):
    @pl.when(first)
    def _():
        xpad[:, CONV_PAD - depth:CONV_PAD, :] = hist_ref[...]

    for s in range(ns):
        xpad[s, CONV_PAD:CONV_PAD + l, :] = x[s * l:(s + 1) * l]
    views = []
    for j in range(depth, 0, -1):
        parts = [xpad[s, CONV_PAD - j:CONV_PAD - j + l, :] for s in range(ns)]
        views.append(parts[0] if ns == 1 else jnp.concatenate(parts, axis=0))
    return views


def _carry_history(xpad, l, valid_rows, last, out_ref, prev_ref, depth):
    _emit_state(last, out_ref, prev_ref,
                xpad.at[:, CONV_PAD - depth + valid_rows:CONV_PAD + valid_rows, :])

    nxt = xpad[:, CONV_PAD - depth + l:CONV_PAD + l, :]
    xpad[:, CONV_PAD - depth:CONV_PAD, :] = nxt


def _causal_conv(x, xpad, w_ref, first, hist_ref, ns, l):
    y = x * w_ref[CONV_TAPS - 1:CONV_TAPS, :]
    for j, view in enumerate(_with_history(x, xpad, first, hist_ref, ns, l, CONV_TAPS - 1)):
        y = y + view * w_ref[j:j + 1, :]
    return y


def _gdn_kernel(prev_s_ref, prev_conv_ref, za_ref, conv0_ref, s0_ref, convw_ref, alog_ref, dt_ref, ng_ref,
                y_ref, sout_ref, convout_ref, xpad, s_scr, *, c, nc, ns, nt, valid_rows, group):
    t = pl.program_id(1)
    first = t == 0
    last = t == nt - 1
    l = nc * c
    rows = ns * l

    @pl.when(first)
    def _():
        s_scr[...] = s0_ref[...]

    qkv = _silu(_causal_conv(za_ref[:, 0:A_CONV_CH], xpad, convw_ref, first, conv0_ref, ns, l))
    _carry_history(xpad, l, valid_rows, last, convout_ref, prev_conv_ref, CONV_TAPS - 1)

    ba = za_ref[:, ZA_BA:ZA_BA + LANE]
    beta_all = _sigmoid(ba)
    g_all = -jnp.exp(alog_ref[...]) * _softplus(ba + dt_ref[...])
    masked = valid_rows < l
    if masked:
        valid = _rows_in((rows, LANE), l) < valid_rows
        beta_all = jnp.where(valid, beta_all, 0.0)
        g_all = jnp.where(valid, g_all, 0.0)
    gc_all = _cumsum_rows(g_all, c)

    rr = _rows((c, c))
    cc = _cols((c, c))
    causal = rr >= cc
    strict = rr > cc
    ng = ng_ref[...]
    pick = (_cols((SUBLANE * A_HEADS, LANE))
            == A_HEADS + _rows((SUBLANE * A_HEADS, LANE)) // SUBLANE).astype(F32)

    all_chunks = [(s, n) for n in range(nc) for s in range(ns)]
    heads = range(A_HEADS)
    state = {(s, h): s_scr[s, h] for s in range(ns) for h in heads}

    def advance(chunks):
        inst = [(s, n, h) for (s, n) in chunks for h in heads]
        lower, attn, rhs2, kdec, qg, glast = {}, {}, {}, {}, {}, {}
        for (s, n) in chunks:
            r0 = s * l + n * c
            rs = slice(r0, r0 + c)
            gc_rows = lax.dot_general(pick, gc_all[rs], (((1,), (1,)), ((), ())), precision=HIGHEST,
                                      preferred_element_type=F32)
            for h in heads:
                q = qkv[rs, h * A_DK:(h + 1) * A_DK]
                k = qkv[rs, A_QK_W + h * A_DK:A_QK_W + (h + 1) * A_DK]
                v = qkv[rs, 2 * A_QK_W + h * A_DV:2 * A_QK_W + (h + 1) * A_DV]
                q = q * lax.rsqrt(jnp.sum(q * q, axis=-1, keepdims=True) + EPS) * (A_DK ** -0.5)
                k = k * lax.rsqrt(jnp.sum(k * k, axis=-1, keepdims=True) + EPS)
                if masked:
                    valid_c = _rows((c, A_DK)) < valid_rows - n * c
                    k = jnp.where(valid_c, k, 0.0)
                    v = jnp.where(valid_c, v, 0.0)
                beta = beta_all[rs, h:h + 1]
                gc = gc_all[rs, A_HEADS + h:A_HEADS + h + 1]
                gc_row = gc_rows[SUBLANE * h:SUBLANE * h + 1, :]
                dec = jnp.where(causal, jnp.exp(jnp.where(causal, gc - gc_row, 0.0)), 0.0)
                kb = k * beta
                egc = jnp.exp(gc)
                sc = _dot_nt(jnp.concatenate([kb, q], axis=0), k)
                key = (s, n, h)
                lower[key] = jnp.where(strict, -sc[:c] * dec, 0.0)
                attn[key] = sc[c:] * dec
                rhs2[key] = jnp.concatenate([v * beta, kb * egc], axis=1)
                gc_last = gc[c - 1:c, :]
                kdec[key] = k * jnp.exp(gc_last - gc)
                glast[key] = jnp.exp(gc_last)
                qg[key] = q * egc
        tinv = dict(zip(inst, _unit_lower_inverses([lower[i] for i in inst])))
        uw = {i: _dot(tinv[i], rhs2[i]) for i in inst}
        ao = {i: _dot(attn[i], uw[i]) for i in inst}
        bm = {i: _dot_tn(kdec[i], uw[i]) for i in inst}
        for (s, n) in chunks:
            r0 = s * l + n * c
            for h in heads:
                i = (s, n, h)
                st = state[(s, h)]
                o = _dot(qg[i] - ao[i][:, A_DV:], st) + ao[i][:, :A_DV]
                state[(s, h)] = st * glast[i] - _dot(bm[i][:, A_DV:], st) + bm[i][:, :A_DV]
                o = o * lax.rsqrt(jnp.mean(o * o, axis=-1, keepdims=True) + EPS)
                z = za_ref[r0:r0 + c, A_CONV_CH + h * A_DV:A_CONV_CH + (h + 1) * A_DV]
                y_ref[r0:r0 + c, h * A_DV:(h + 1) * A_DV] = o * ng * _silu(z)

    for g in range(0, len(all_chunks), group):
        advance(all_chunks[g:g + group])
    for s in range(ns):
        for h in heads:
            s_scr[s, h] = state[(s, h)]

    _emit_state(last, sout_ref, prev_s_ref, s_scr)


def _with_prev(kern, prev, shapes, ns):
    if prev is None:
        return functools.partial(kern, None, None), [], [], 0
    n_prev = prev[0].shape[0]
    return kern, list(prev), [_stacked_spec(n_prev, (ns,) + shp) for shp in shapes], n_prev


def _mixer_grid(bsz, t_pad, t_valid, l, ns):
    assert t_pad % l == 0 and bsz % ns == 0 and (ns == 1 or t_pad == l)
    nt = t_pad // l
    valid_rows = t_valid - (nt - 1) * l
    assert 0 < valid_rows <= l and (nt == 1 or valid_rows == l)
    return (bsz // ns, nt), ns * l, valid_rows


def _gdn(za, conv0, s0, convw, alog_row, dt_row, ng, *, bsz, t_pad, t_valid, c, nc, ns, group,
         layer, state_layer, prev):
    grid, rows, valid_rows = _mixer_grid(bsz, t_pad, t_valid, nc * c, ns)
    spec = functools.partial(_param_spec, layer=layer)
    nt = grid[1]
    kern = functools.partial(_gdn_kernel, c=c, nc=nc, ns=ns, nt=nt, valid_rows=valid_rows,
                             group=group)
    state_shapes = ((A_HEADS, A_DK, A_DV), (3, A_CONV_CH))
    kern, prev_args, prev_specs, n_prev = _with_prev(kern, prev, state_shapes, ns)
    return pl.pallas_call(
        kern,
        grid=grid,
        in_specs=prev_specs + [
                  pl.BlockSpec((rows, ZA_W), lambda b, t: (b * nt + t, 0)),
                  _state_spec((ns, 3, A_CONV_CH), state_layer),
                  _state_spec((ns, A_HEADS, A_DK, A_DV), state_layer),
                  spec((CONV_TAPS, A_CONV_CH)),
                  spec((1, LANE)),
                  spec((1, LANE)),
                  spec((1, A_DV))],
        out_specs=[pl.BlockSpec((rows, A_V_W), lambda b, t: (b * nt + t, 0)),
                   ] + [_stacked_spec(n_prev + 1, (ns,) + shp) for shp in state_shapes],
        out_shape=[jax.ShapeDtypeStruct((bsz * t_pad, A_V_W), F32)]
        + [jax.ShapeDtypeStruct((n_prev + 1, bsz) + shp, F32) for shp in state_shapes],
        scratch_shapes=[pltpu.VMEM((ns, CONV_PAD + nc * c, A_CONV_CH), F32),
                        pltpu.VMEM((ns, A_HEADS, A_DK, A_DV), F32)],
        compiler_params=pltpu.CompilerParams(dimension_semantics=("arbitrary", "arbitrary"),
                                             vmem_limit_bytes=VMEM_LIMIT),
        name="gdn",
    )(*prev_args, za, conv0, s0, convw, alog_row, dt_row, ng)


def _rwkv_kernel(prev_s_ref, prev_shift_ref, zb_ref, shift0_ref, s0_ref, mu_ref, w0_ref, wup_ref, a0_ref,
                 aup_ref, gup_ref, kk_ref, ka_ref, rk_ref, lnw_ref, lnb_ref, hsum_ref,
                 y_ref, sout_ref, shiftout_ref, xpad, s_scr, prep, *, c, nc, ns, nt, valid_rows, group):
    t = pl.program_id(1)
    first = t == 0
    last = t == nt - 1
    l = nc * c
    rows = ns * l

    @pl.when(first)
    def _():
        s_scr[...] = s0_ref[...]

    zb = zb_ref[...]
    (prev,) = _with_history(zb, xpad, first, shift0_ref, ns, l, 1)
    _carry_history(xpad, l, valid_rows, last, shiftout_ref, prev_shift_ref, 1)
    zs = zb + (prev - zb) * mu_ref[...]

    r_all = zs[:, 0:B_W]
    k_all = zs[:, B_W:2 * B_W]
    v_all = zs[:, 2 * B_W:3 * B_W]
    xwa = zs[:, 3 * B_W:3 * B_W + B_LORA_WA]
    xg = zs[:, 3 * B_W + B_LORA_WA:B_IN]
    w_log = -_softplus(-(w0_ref[...] + _dot(jnp.tanh(xwa), wup_ref[...]))) - 0.5
    lw = -jnp.exp(w_log)
    a_all = _sigmoid(a0_ref[...] + _dot(xwa, aup_ref[...]))
    g_all = _dot(_sigmoid(xg), gup_ref[...])
    kk_all = k_all * kk_ref[...]
    k_all = k_all * (1.0 + (a_all - 1.0) * ka_ref[...])
    masked = valid_rows < l
    if masked:
        valid = _rows_in((rows, B_W), l) < valid_rows
        lw = jnp.where(valid, lw, 0.0)
        kk_all = jnp.where(valid, kk_all, 0.0)
        k_all = jnp.where(valid, k_all, 0.0)
        v_all = jnp.where(valid, v_all, 0.0)
    gc_all = _cumsum_rows(lw, c)

    def head_sums(x):
        hi = x.astype(BF16)
        lo = (x - hi.astype(F32)).astype(BF16)
        ones = hsum_ref[...]
        return (jnp.dot(hi, ones, preferred_element_type=F32)
                + jnp.dot(lo, ones, preferred_element_type=F32))

    kkn_all = kk_all * lax.rsqrt(head_sums(kk_all * kk_all) + EPS)
    bonus_all = head_sums(r_all * k_all * rk_ref[...]) * v_all

    rr = _rows((2 * c, 2 * c))
    cc = _cols((2 * c, 2 * c))
    tt = jnp.where(rr >= c, rr - c, rr)
    ss = jnp.where(cc >= c, cc - c, cc)
    mask2 = jnp.logical_or(tt > ss, jnp.logical_and(rr >= c, tt == ss))

    prep_names = ('r', 'k', 'v', 'kkn', 'a', 'lw', 'gc', 'bonus', 'g')
    for idx, val in enumerate((r_all, k_all, v_all, kkn_all, a_all, lw, gc_all, bonus_all, g_all)):
        prep[idx] = val
    heads = range(B_HEADS)
    sls = [slice(h * B_N, (h + 1) * B_N) for h in heads]

    def at_rows(name, rs):
        return prep[prep_names.index(name), rs, :]

    def advance(g, carry):
        members = range(group)
        inst = [(j, h) for j in members for h in heads]
        amat, rhs_uw, r_abs, tails, e_last, vs, rows_of, seq_of = {}, {}, {}, {}, {}, {}, {}, {}
        for j in members:
            q = g * group + j
            rs = pl.ds(pl.multiple_of(q * c, c), c)
            rows_of[j] = rs
            seq_of[j] = 0 if ns == 1 else q
            gc = at_rows('gc', rs)
            lwc = at_rows('lw', rs)
            gmid = gc[c // 2:c // 2 + 1, :]
            glast = gc[c - 1:c, :]
            e_last[j] = jnp.exp(glast)
            r = at_rows('r', rs)
            k = at_rows('k', rs)
            v = at_rows('v', rs)
            alpha = -at_rows('kkn', rs)
            bet = at_rows('kkn', rs) * at_rows('a', rs)
            e_inv = jnp.exp(gmid - gc)
            e_tail = jnp.exp(glast - gc)
            lhs_a = alpha * jnp.exp(gc - lwc - gmid)
            lhs_r = r * jnp.exp(gc - gmid)
            rhs_b = bet * e_inv
            rhs_k = k * e_inv
            abs_a = alpha * jnp.exp(gc - lwc)
            abs_r = r * jnp.exp(gc)
            tail_b = bet * e_tail
            tail_k = k * e_tail
            for h in heads:
                sl = sls[h]
                lhs = jnp.concatenate([lhs_a[:, sl], lhs_r[:, sl]], axis=0)
                rhs = jnp.concatenate([rhs_b[:, sl], rhs_k[:, sl]], axis=0)
                key = (j, h)
                amat[key] = jnp.where(mask2, _dot_nt(lhs, rhs), 0.0)
                rhs_uw[key] = abs_a[:, sl]
                r_abs[key] = abs_r[:, sl]
                tails[key] = jnp.concatenate([tail_b[:, sl], tail_k[:, sl]], axis=0)
                vs[key] = v[:, sl]

        from_v = {i: _dot(amat[i][:, c:], vs[i]) for i in inst}
        uw = dict(zip(inst, _unit_lower_solves(
            [amat[i][:c, :c] for i in inst],
            [jnp.concatenate([from_v[i][:c], rhs_uw[i]], axis=1) for i in inst])))
        ro = {i: _dot(amat[i][c:, :c], uw[i]) for i in inst}
        mp = {i: _dot_tn(uw[i][:, B_N:], tails[i][:c]) for i in inst}
        bb = {i: _dot_tn(jnp.concatenate([uw[i][:, :B_N], vs[i]], axis=0), tails[i]) for i in inst}
        state = {}
        outs = []
        for j in members:
            per_head = []
            for h in heads:
                i = (j, h)
                skey = (0 if ns == 1 else j, h)
                st = state[skey] if skey in state else s_scr[seq_of[j], h]
                per_head.append(_dot_nt(r_abs[i] + ro[i][:, B_N:], st) + from_v[i][c:] + ro[i][:, :B_N])
                state[skey] = st * e_last[j][:, sls[h]] + _dot(st, mp[i]) + bb[i]
            outs.append(jnp.concatenate(per_head, axis=1))
        for j, o in zip(members, outs):
            d = o - head_sums(o) * (1.0 / B_N)
            var = head_sums(d * d) * (1.0 / B_N)
            o = d * lax.rsqrt(var + B_LN_EPS) * lnw_ref[...] + lnb_ref[...]
            y_ref[rows_of[j], :] = (o + at_rows('bonus', rows_of[j])) * at_rows('g', rows_of[j])
        for (sk, h), st in state.items():
            s_scr[seq_of[sk], h] = st
        return carry

    assert (ns * nc) % group == 0 and (ns == 1 or nc == 1)
    lax.fori_loop(0, ns * nc // group, advance, 0)

    _emit_state(last, sout_ref, prev_s_ref, s_scr)


def _rwkv(zb, shift0, s0, prm, *, bsz, t_pad, t_valid, c, nc, ns, group, layer, state_layer, prev):
    grid, rows, valid_rows = _mixer_grid(bsz, t_pad, t_valid, nc * c, ns)
    nt = grid[1]
    kern = functools.partial(_rwkv_kernel, c=c, nc=nc, ns=ns, nt=nt, valid_rows=valid_rows,
                             group=group)
    spec = functools.partial(_param_spec, layer=layer)
    row = lambda wd: spec((1, wd))
    state_shapes = ((B_HEADS, B_N, B_N), (1, B_IN))
    kern, prev_args, prev_specs, n_prev = _with_prev(kern, prev, state_shapes, ns)
    return pl.pallas_call(
        kern,
        grid=grid,
        in_specs=prev_specs + [
                  pl.BlockSpec((rows, ZB_W), lambda b, t: (b * nt + t, 0)),
                  _state_spec((ns, 1, B_IN), state_layer),
                  _state_spec((ns, B_HEADS, B_N, B_N), state_layer),
                  row(B_IN), row(B_W), spec((B_LORA_WA, B_W)), row(B_W),
                  spec((B_LORA_WA, B_W)), spec((B_GATE_LORA, B_W)),
                  row(B_W), row(B_W), row(B_W), row(B_W), row(B_W), _param_spec((B_W, B_W), 0)],
        out_specs=[pl.BlockSpec((rows, B_W), lambda b, t: (b * nt + t, 0)),
                   ] + [_stacked_spec(n_prev + 1, (ns,) + shp) for shp in state_shapes],
        out_shape=[jax.ShapeDtypeStruct((bsz * t_pad, B_W), F32)]
        + [jax.ShapeDtypeStruct((n_prev + 1, bsz) + shp, F32) for shp in state_shapes],
        scratch_shapes=[pltpu.VMEM((ns, CONV_PAD + nc * c, B_IN), F32),
                        pltpu.VMEM((ns, B_HEADS, B_N, B_N), F32),
                        pltpu.VMEM((9, rows, B_W), F32)],
        compiler_params=pltpu.CompilerParams(dimension_semantics=("arbitrary", "arbitrary"),
                                             vmem_limit_bytes=VMEM_LIMIT),
        name="rwkv",
    )(*prev_args, zb, shift0, s0, *prm)


def _rglru_kernel(prev_h_ref, prev_conv_ref, zc_ref, conv0_ref, h0_ref, convw_ref, convb_ref, wg_ref, bg_ref,
                  l_ref, y_ref, hout_ref, convout_ref, xpad, h_scr, *, l, ns, nt, valid_rows):
    t = pl.program_id(1)
    first = t == 0
    last = t == nt - 1
    rows = ns * l

    @pl.when(first)
    def _():
        h_scr[...] = h0_ref[...]

    xc = (_causal_conv(zc_ref[:, 0:C_WIDTH], xpad, convw_ref, first, conv0_ref, ns, l)
          + convb_ref[...])
    _carry_history(xpad, l, valid_rows, last, convout_ref, prev_conv_ref, CONV_TAPS - 1)

    gates = _sigmoid(_dot(xc, wg_ref[...]) + bg_ref[...])
    r = gates[:, :C_WIDTH]
    i = gates[:, C_WIDTH:]
    log_a = -C_POW * r * _softplus(-l_ref[...])
    a = jnp.exp(log_a)
    b = jnp.sqrt(-jnp.tanh(log_a) * (a * a + 1.0)) * (i * xc)
    if valid_rows < l:
        valid = _rows_in((rows, C_WIDTH), l) < valid_rows
        a = jnp.where(valid, a, 1.0)
        b = jnp.where(valid, b, 0.0)
    s = 1
    while s < SUBLANE:
        b = a * _shift_rows(b, s, 0.0, SUBLANE) + b
        a = a * _shift_rows(a, s, 1.0, SUBLANE)
        s *= 2
    gate = _gelu_tanh(zc_ref[:, C_WIDTH:])
    for q in range(ns):
        carry = h_scr[q]
        for r0 in range(q * l, (q + 1) * l, SUBLANE):
            rs = slice(r0, r0 + SUBLANE)
            hseq = a[rs] * carry + b[rs]
            carry = hseq[SUBLANE - 1:SUBLANE, :]
            y_ref[rs, :] = hseq * gate[rs]
        h_scr[q] = carry

    _emit_state(last, hout_ref, prev_h_ref, h_scr)


def _rglru(zc, conv0, h0, prm, *, bsz, t_pad, t_valid, l, ns, layer, state_layer, prev):
    grid, rows, valid_rows = _mixer_grid(bsz, t_pad, t_valid, l, ns)
    nt = grid[1]
    kern = functools.partial(_rglru_kernel, l=l, ns=ns, nt=nt, valid_rows=valid_rows)
    spec = functools.partial(_param_spec, layer=layer)
    state_shapes = ((1, C_WIDTH), (3, C_WIDTH))
    kern, prev_args, prev_specs, n_prev = _with_prev(kern, prev, state_shapes, ns)
    return pl.pallas_call(
        kern,
        grid=grid,
        in_specs=prev_specs + [
                  pl.BlockSpec((rows, ZC_W), lambda b, t: (b * nt + t, 0)),
                  _state_spec((ns, 3, C_WIDTH), state_layer),
                  _state_spec((ns, 1, C_WIDTH), state_layer),
                  spec((CONV_TAPS, C_WIDTH)), spec((1, C_WIDTH)),
                  spec((C_WIDTH, 2 * C_WIDTH)), spec((1, 2 * C_WIDTH)),
                  spec((1, C_WIDTH))],
        out_specs=[pl.BlockSpec((rows, C_WIDTH), lambda b, t: (b * nt + t, 0)),
                   ] + [_stacked_spec(n_prev + 1, (ns,) + shp) for shp in state_shapes],
        out_shape=[jax.ShapeDtypeStruct((bsz * t_pad, C_WIDTH), F32)]
        + [jax.ShapeDtypeStruct((n_prev + 1, bsz) + shp, F32) for shp in state_shapes],
        scratch_shapes=[pltpu.VMEM((ns, CONV_PAD + l, C_WIDTH), F32),
                        pltpu.VMEM((ns, 1, C_WIDTH), F32)],
        compiler_params=pltpu.CompilerParams(dimension_semantics=("arbitrary", "arbitrary"),
                                             vmem_limit_bytes=VMEM_LIMIT),
        name="rglru",
    )(*prev_args, zc, conv0, h0, *prm)


FF_CHUNK = 1024


def _merge_ffn_kernel(x_ref, ya_ref, yb_ref, yc_ref, zg_ref, wbr_ref, wout_ref, g2_ref,
                      wup_ref, wdown_ref, gf_ref, o_ref, *, final):
    merged = None
    for n, y_ref in enumerate((ya_ref, yb_ref, yc_ref)):
        proj = _dot(y_ref[...], wbr_ref[n])
        term = _sigmoid(zg_ref[:, n * D_MODEL:(n + 1) * D_MODEL]) * proj
        merged = term if merged is None else merged + term
    x = x_ref[...] + _dot(merged, wout_ref[...])
    h = x * lax.rsqrt(jnp.mean(x * x, axis=-1, keepdims=True) + EPS) * g2_ref[...]
    h = h.astype(BF16)
    acc = x
    for j in range(0, D_FF, FF_CHUNK):
        up = jnp.dot(h, wup_ref[:, j:j + FF_CHUNK], preferred_element_type=F32)
        acc = acc + _dot(jnp.square(jnp.maximum(up, 0.0)), wdown_ref[j:j + FF_CHUNK, :])
    if final:
        acc = acc * lax.rsqrt(jnp.mean(acc * acc, axis=-1, keepdims=True) + EPS) * gf_ref[...]
    o_ref[...] = acc


def _merge_ffn(x, ya, yb, yc, zg, wbr, wout, g2, wup, wdown, gf, tm, final, layer):
    m = x.shape[0]
    spec = functools.partial(_param_spec, layer=layer)
    tile = lambda wd: pl.BlockSpec((tm, wd), lambda i: (i, 0))
    return pl.pallas_call(
        functools.partial(_merge_ffn_kernel, final=final),
        grid=(m // tm,),
        in_specs=[tile(D_MODEL), tile(A_V_W), tile(B_W), tile(C_WIDTH), tile(ZG_W),
                  spec((N_BRANCH, A_V_W, D_MODEL)), spec((D_MODEL, D_MODEL)),
                  spec((1, D_MODEL)), spec((D_MODEL, D_FF)),
                  spec((D_FF, D_MODEL)), _param_spec((1, D_MODEL), 0)],
        out_specs=tile(D_MODEL),
        out_shape=jax.ShapeDtypeStruct((m, D_MODEL), F32),
        compiler_params=pltpu.CompilerParams(dimension_semantics=("arbitrary",),
                                             vmem_limit_bytes=VMEM_LIMIT),
        name="merge_ffn",
    )(x, ya, yb, yc, zg, wbr, wout, g2, wup, wdown, gf)


def _block_diag(w):
    g, n, _ = w.shape
    eye = jnp.eye(g, dtype=w.dtype)
    return (eye[:, None, :, None] * w[:, :, None, :]).reshape(g * n, g * n)


def _prepare_params(p):
    depth = p['w_in'].shape[0]
    row = lambda a: a.reshape(depth, 1, -1).astype(F32)
    a_in = A_CONV_CH + A_V_W + 2 * A_HEADS
    pad = jnp.zeros((depth, D_MODEL, LANE - 2 * A_HEADS), p['w_in'].dtype)
    w_in = jnp.concatenate([p['w_in'][:, :, :a_in], pad, p['w_in'][:, :, a_in:]], axis=2).astype(BF16)
    lane_row = lambda a: jnp.zeros((depth, 1, LANE), F32).at[:, 0, A_HEADS:2 * A_HEADS].set(a)
    zeros_lora = jnp.zeros((depth, B_LORA_WA // 2, B_W), F32)
    block_diag = jax.vmap(_block_diag)
    gdn = (p['a_conv_w'], lane_row(p['a_A_log']), lane_row(p['a_dt_bias']), row(p['a_norm_g']))
    rwkv = (row(p['b_mu']), row(p['b_w0']),
            jnp.concatenate([p['b_w_up'], zeros_lora], axis=1).astype(BF16),
            row(p['b_a0']),
            jnp.concatenate([zeros_lora, p['b_a_up']], axis=1).astype(BF16),
            p['b_g_up'].astype(BF16),
            row(p['b_k_k']), row(p['b_k_a']), row(p['b_r_k']), row(p['b_ln_w']), row(p['b_ln_b']),
            jnp.kron(jnp.eye(B_HEADS, dtype=F32), jnp.ones((B_N, B_N), F32)).astype(BF16)[None])
    rglru = (p['c_conv_w'], row(p['c_conv_b']),
             jnp.concatenate([block_diag(p['c_wa']), block_diag(p['c_wx'])], axis=2).astype(BF16),
             jnp.concatenate([row(p['c_ba']), row(p['c_bx'])], axis=2), row(p['c_L']))
    ffn = (p['w_branch'].astype(BF16), p['w_out'].astype(BF16), row(p['norm2_g']),
           p['w_up'].astype(BF16), p['w_down'].astype(BF16))
    return dict(norm1_g=row(p['norm1_g']), w_in=w_in, gdn=gdn, rwkv=rwkv, rglru=rglru, ffn=ffn)


PROMPT_BLOCKING = dict(c=64, nc=4, ns=1, gdn_group=4, rwkv_group=4, lru_rows=512, lru_ns=1)
SAMPLE_BLOCKING = dict(c=SUBLANE, nc=1, ns=8, gdn_group=8, rwkv_group=4, lru_rows=SUBLANE, lru_ns=16)
TOKEN_TILE = 256


def _layer(x, states, new_states, lp, gf, *, layer, state_layer, bsz, t_pad, t_valid, blocking, final):
    a_s, a_conv, b_s, b_shift, c_h, c_conv = states
    prev = (lambda i, j: None) if new_states is None else (lambda i, j: (new_states[i], new_states[j]))
    kw = dict(bsz=bsz, t_pad=t_pad, t_valid=t_valid, layer=layer, state_layer=state_layer)
    chunked = dict(c=blocking['c'], nc=blocking['nc'], ns=blocking['ns'], **kw)
    za, zb, zc, zg = _in_proj(x, lp['norm1_g'], lp['w_in'], TOKEN_TILE, layer)
    ya, a_s, a_conv = _gdn(za, a_conv, a_s, *lp['gdn'], group=blocking['gdn_group'], prev=prev(0, 1),
                           **chunked)
    yb, b_s, b_shift = _rwkv(zb, b_shift, b_s, lp['rwkv'], group=blocking['rwkv_group'], prev=prev(2, 3),
                             **chunked)
    yc, c_h, c_conv = _rglru(zc, c_conv, c_h.reshape(-1, bsz, 1, C_WIDTH), lp['rglru'],
                             l=blocking['lru_rows'], ns=blocking['lru_ns'], prev=prev(4, 5), **kw)
    x = _merge_ffn(x, ya, yb, yc, zg, *lp['ffn'], gf, TOKEN_TILE, final, layer)
    return x, (a_s, a_conv, b_s, b_shift, c_h, c_conv)


def kernel(x_prompt, x_sample, state_a_S, state_a_conv, state_b_S, state_b_shift, state_c_h, state_c_conv, norm1_g, w_in, a_conv_w, a_A_log, a_dt_bias, a_norm_g, b_mu, b_w0, b_w_up, b_a0, b_a_up, b_g_up, b_k_k, b_k_a, b_r_k, b_ln_w, b_ln_b, c_conv_w, c_conv_b, c_wa, c_ba, c_wx, c_bx, c_L, w_branch, w_out, norm2_g, w_up, w_down, final_norm_g):
    p = dict(norm1_g=norm1_g, w_in=w_in, a_conv_w=a_conv_w, a_A_log=a_A_log, a_dt_bias=a_dt_bias,
             a_norm_g=a_norm_g, b_mu=b_mu, b_w0=b_w0, b_w_up=b_w_up, b_a0=b_a0, b_a_up=b_a_up,
             b_g_up=b_g_up, b_k_k=b_k_k, b_k_a=b_k_a, b_r_k=b_r_k, b_ln_w=b_ln_w, b_ln_b=b_ln_b,
             c_conv_w=c_conv_w, c_conv_b=c_conv_b, c_wa=c_wa, c_ba=c_ba, c_wx=c_wx, c_bx=c_bx,
             c_L=c_L, w_branch=w_branch, w_out=w_out, norm2_g=norm2_g, w_up=w_up, w_down=w_down)
    bp, tp, _ = x_prompt.shape
    bs, ts, _ = x_sample.shape
    ts_pad = -(-ts // SUBLANE) * SUBLANE
    assert ts_pad == SAMPLE_BLOCKING['c'] and tp % (PROMPT_BLOCKING['c'] * PROMPT_BLOCKING['nc']) == 0
    gf = final_norm_g.reshape(1, 1, D_MODEL)

    xp = x_prompt.reshape(bp * tp, D_MODEL)
    xs = jnp.pad(x_sample, ((0, 0), (0, ts_pad - ts), (0, 0))).reshape(bs * ts_pad, D_MODEL)
    zero = lambda *shape: jnp.zeros((1,) + shape, F32)
    p_states = (zero(bp, A_HEADS, A_DK, A_DV), zero(bp, 3, A_CONV_CH),
                zero(bp, B_HEADS, B_N, B_N), zero(bp, 1, B_IN),
                zero(bp, C_WIDTH), zero(bp, 3, C_WIDTH))
    s_states = (state_a_S, state_a_conv, state_b_S, state_b_shift, state_c_h, state_c_conv)
    lp = _prepare_params(p)
    p_new = s_new = None
    for l in range(DEPTH):
        final = l == DEPTH - 1
        xp, p_new = _layer(xp, p_states, p_new, lp, gf, layer=l, state_layer=0, bsz=bp, t_pad=tp,
                           t_valid=tp, blocking=PROMPT_BLOCKING, final=final)
        xs, s_new = _layer(xs, s_states, s_new, lp, gf, layer=l, state_layer=l, bsz=bs, t_pad=ts_pad,
                           t_valid=ts, blocking=SAMPLE_BLOCKING, final=final)
    squeeze_h = lambda st: st[:4] + (st[4].reshape(DEPTH, -1, C_WIDTH), st[5])
    y_prompt = xp.reshape(bp, tp, D_MODEL)
    y_sample = xs.reshape(bs, ts_pad, D_MODEL)[:, :ts]
    return (y_prompt, y_sample) + squeeze_h(p_new) + squeeze_h(s_new)
```

```python
import functools

import jax
import jax.numpy as jnp
from jax import lax
from jax.experimental import pallas as pl
from jax.experimental.pallas import tpu as pltpu

F32 = jnp.float32
BF16 = jnp.bfloat16
HIGHEST = lax.Precision.HIGHEST

EPS = 1e-6
D_MODEL = 1024
DEPTH = 2
A_HEADS = 4
A_DK = 128
A_DV = 128
A_QK_W = A_HEADS * A_DK
A_V_W = A_HEADS * A_DV
A_CONV_CH = 2 * A_QK_W + A_V_W
B_N = 64
B_HEADS = 8
B_W = B_HEADS * B_N
B_LORA_WA = 128
B_GATE_LORA = 128
B_IN = 3 * B_W + B_LORA_WA + B_GATE_LORA
B_LN_EPS = 64e-5
C_WIDTH = 512
C_BLOCKS = 8
C_POW = 8.0
N_BRANCH = 3
D_FF = 4 * D_MODEL

LANE = 128
SUBLANE = 8
CONV_TAPS = 4
CONV_PAD = SUBLANE

ZA_W = A_CONV_CH + A_V_W + LANE
ZA_BA = A_CONV_CH + A_V_W
ZB_W = B_IN
ZC_W = 2 * C_WIDTH
ZG_W = N_BRANCH * D_MODEL
Z_WIDTHS = (ZA_W, ZB_W, ZC_W, ZG_W)
A_IN = A_CONV_CH + A_V_W + 2 * A_HEADS
N_IN = A_IN + ZB_W + ZC_W + ZG_W

VMEM_LIMIT = 56 * 1024 * 1024


def _dot(a, b):
    return jnp.dot(a.astype(BF16), b.astype(BF16), preferred_element_type=F32)


def _dot_nt(a, b):
    return lax.dot_general(a.astype(BF16), b.astype(BF16), (((1,), (1,)), ((), ())),
                           preferred_element_type=F32)


def _dot_tn(a, b):
    return lax.dot_general(a.astype(BF16), b.astype(BF16), (((0,), (0,)), ((), ())),
                           preferred_element_type=F32)


def _sigmoid(x):
    return 1.0 / (1.0 + jnp.exp(-x))


def _silu(x):
    return x * _sigmoid(x)


def _softplus(x):
    return jnp.maximum(x, 0.0) + jnp.log1p(jnp.exp(-jnp.abs(x)))


def _gelu_tanh(x):
    return 0.5 * x * (1.0 + jnp.tanh(0.7978845608028654 * (x + 0.044715 * (x * x * x))))


def _rows(shape):
    return lax.broadcasted_iota(jnp.int32, shape, 0)


def _cols(shape):
    return lax.broadcasted_iota(jnp.int32, shape, 1)


def _rows_in(shape, period):
    r = _rows(shape)
    return r if period >= shape[0] else r & (period - 1)


def _shift_rows(x, s, fill, period):
    return jnp.where(_rows_in(x.shape, period) >= s, pltpu.roll(x, s, 0), fill)


def _cumsum_rows(x, period):
    s = 1
    while s < period:
        x = x + _shift_rows(x, s, 0.0, period)
        s *= 2
    return x


def _unit_lower_inverses(ns):
    c = ns[0].shape[0]
    eye = (_rows(ns[0].shape) == _cols(ns[0].shape)).astype(F32)
    ts = [eye + n for n in ns]
    ps = list(ns)
    s = 2
    while s < c:
        ps = [_dot(p, p) for p in ps]
        ts = [t + _dot(t, p) for t, p in zip(ts, ps)]
        s *= 2
    return ts


def _unit_lower_solves(ns, xs):
    c = ns[0].shape[0]
    xs = [x + _dot(n, x) for n, x in zip(ns, xs)]
    ps = list(ns)
    s = 2
    while s < c:
        ps = [_dot(p, p) for p in ps]
        xs = [x + _dot(p, x) for p, x in zip(ps, xs)]
        s *= 2
    return xs


def _param_spec(shape, layer):
    nd = len(shape)
    return pl.BlockSpec((None,) + shape, lambda *_: (layer,) + (0,) * nd, pipeline_mode=pl.Buffered(1))


def _group_of_2d(b, t):
    return b


def _stacked_spec(layers, shape, group_of=_group_of_2d):
    nd = len(shape) - 1
    return pl.BlockSpec((layers,) + shape, lambda *g: (0, group_of(*g)) + (0,) * nd)


def _emit_state(last, out_ref, prev_ref, new_ref):
    @pl.when(last)
    def _():
        n_prev = out_ref.shape[0] - 1
        if n_prev:
            out_ref[0:n_prev] = prev_ref[...]
        out_ref[n_prev] = new_ref[...]


def _state_spec(shape, layer, group_of=_group_of_2d):
    nd = len(shape) - 1
    return pl.BlockSpec((None,) + shape, lambda *g: (layer, group_of(*g)) + (0,) * nd)


def _in_proj_kernel(prev_h_ref, prev_conv_ref, x_ref, g_ref, w_ref, conv0_ref, h0_ref, convw_ref, convb_ref,
                    wg_ref, bg_ref, l_ref, za_ref, zb_ref, yc_ref, zg_ref, hout_ref, convout_ref,
                    w_rest, xpad, h_scr, *, l, ns, nt, valid_rows):
    step = pl.program_id(0)
    t = step % nt
    @pl.when(step == 0)
    def _():
        w_rest[...] = w_ref[:, A_IN:]

    x = x_ref[...]
    h = x * lax.rsqrt(jnp.mean(x * x, axis=-1, keepdims=True) + EPS) * g_ref[...]
    h = h.astype(BF16)
    proj = lambda ref, off, width: jnp.dot(h, ref[:, off:off + width], preferred_element_type=F32)
    zc = proj(w_rest, ZB_W, ZC_W)
    za_ref[...] = proj(w_ref, 0, ZA_W)
    hand_over = _rglru_block(
        zc, yc_ref,
        (prev_h_ref, prev_conv_ref, conv0_ref, h0_ref, convw_ref, convb_ref, wg_ref, bg_ref, l_ref,
         hout_ref, convout_ref),
        (xpad, h_scr), t == 0, t == nt - 1, l=l, ns=ns, valid_rows=valid_rows)
    zb_ref[...] = proj(w_rest, 0, ZB_W)
    zg_ref[...] = proj(w_rest, ZB_W + ZC_W, ZG_W)
    hand_over()


def _in_proj(x, g, w, conv0, h0, prm, *, tm, bsz, t_pad, t_valid, layer, state_layer, prev):
    m = x.shape[0]
    l = min(tm, t_pad)
    ns = tm // l
    _, _, valid_rows = _mixer_grid(bsz, t_pad, t_valid, l, ns)
    nt = t_pad // l
    group_of = lambda i: i // nt
    spec = functools.partial(_param_spec, layer=layer)
    kern = functools.partial(_in_proj_kernel, l=l, ns=ns, nt=nt, valid_rows=valid_rows)
    state_shapes = ((1, C_WIDTH), (3, C_WIDTH))
    kern, prev_args, prev_specs, n_prev = _with_prev(kern, prev, state_shapes, ns, group_of)
    tile = lambda wd: pl.BlockSpec((tm, wd), lambda i: (i, 0))
    widths = (ZA_W, ZB_W, C_WIDTH, ZG_W)
    return pl.pallas_call(
        kern,
        grid=(m // tm,),
        in_specs=prev_specs + [
                  tile(D_MODEL), spec((1, D_MODEL)), spec((D_MODEL, N_IN)),
                  _state_spec((ns, 3, C_WIDTH), state_layer, group_of),
                  _state_spec((ns, 1, C_WIDTH), state_layer, group_of),
                  spec((CONV_TAPS, C_WIDTH)), spec((1, C_WIDTH)),
                  spec((C_WIDTH, 2 * C_WIDTH)), spec((1, 2 * C_WIDTH)),
                  spec((1, C_WIDTH))],
        out_specs=[tile(wd) for wd in widths]
        + [_stacked_spec(n_prev + 1, (ns,) + shp, group_of) for shp in state_shapes],
        out_shape=[jax.ShapeDtypeStruct((m, wd), F32) for wd in widths]
        + [jax.ShapeDtypeStruct((n_prev + 1, bsz) + shp, F32) for shp in state_shapes],
        scratch_shapes=[pltpu.VMEM((D_MODEL, N_IN - A_IN), BF16),
                        pltpu.VMEM((ns, CONV_PAD + l, C_WIDTH), F32),
                        pltpu.VMEM((ns, 1, C_WIDTH), F32)],
        compiler_params=pltpu.CompilerParams(dimension_semantics=("arbitrary",),
                                             vmem_limit_bytes=VMEM_LIMIT),
        name="in_proj",
    )(*prev_args, x, g, w, conv0, h0, *prm)


def _with_history(x, xpad, first, hist_ref, ns, l, depth):
    @pl.when(first)
    def _():
        xpad[:, CONV_PAD - depth:CONV_PAD, :] = hist_ref[...]

    for s in range(ns):
        xpad[s, CONV_PAD:CONV_PAD + l, :] = x[s * l:(s + 1) * l]
    views = []
    for j in range(depth, 0, -1):
        parts = [xpad[s, CONV_PAD - j:CONV_PAD - j + l, :] for s in range(ns)]
        views.append(parts[0] if ns == 1 else jnp.concatenate(parts, axis=0))
    return views


def _carry_history(xpad, l, valid_rows, last, out_ref, prev_ref, depth):
    _emit_state(last, out_ref, prev_ref,
                xpad.at[:, CONV_PAD - depth + valid_rows:CONV_PAD + valid_rows, :])

    nxt = xpad[:, CONV_PAD - depth + l:CONV_PAD + l, :]
    xpad[:, CONV_PAD - depth:CONV_PAD, :] = nxt


def _causal_conv(x, xpad, w_ref, first, hist_ref, ns, l):
    y = x * w_ref[CONV_TAPS - 1:CONV_TAPS, :]
    for j, view in enumerate(_with_history(x, xpad, first, hist_ref, ns, l, CONV_TAPS - 1)):
        y = y + view * w_ref[j:j + 1, :]
    return y


def _gdn_kernel(prev_s_ref, prev_conv_ref, za_ref, conv0_ref, s0_ref, convw_ref, alog_ref, dt_ref, ng_ref,
                y_ref, sout_ref, convout_ref, xpad, s_scr, *, c, nc, ns, nt, valid_rows, group):
    t = pl.program_id(1)
    first = t == 0
    last = t == nt - 1
    l = nc * c
    rows = ns * l

    @pl.when(first)
    def _():
        s_scr[...] = s0_ref[...]

    qkv = _silu(_causal_conv(za_ref[:, 0:A_CONV_CH], xpad, convw_ref, first, conv0_ref, ns, l))
    _carry_history(xpad, l, valid_rows, last, convout_ref, prev_conv_ref, CONV_TAPS - 1)

    ba = za_ref[:, ZA_BA:ZA_BA + LANE]
    ba = jnp.where(_cols(ba.shape) < 2 * A_HEADS, ba, 0.0)
    beta_all = _sigmoid(ba)
    g_all = -jnp.exp(alog_ref[...]) * _softplus(ba + dt_ref[...])
    masked = valid_rows < l
    if masked:
        valid = _rows_in((rows, LANE), l) < valid_rows
        beta_all = jnp.where(valid, beta_all, 0.0)
        g_all = jnp.where(valid, g_all, 0.0)
    gc_all = _cumsum_rows(g_all, c)

    rr = _rows((c, c))
    cc = _cols((c, c))
    causal = rr >= cc
    strict = rr > cc
    ng = ng_ref[...]
    pick = (_cols((SUBLANE * A_HEADS, LANE))
            == A_HEADS + _rows((SUBLANE * A_HEADS, LANE)) // SUBLANE).astype(F32)

    all_chunks = [(s, n) for n in range(nc) for s in range(ns)]
    heads = range(A_HEADS)
    state = {(s, h): s_scr[s, h] for s in range(ns) for h in heads}

    def advance(chunks):
        inst = [(s, n, h) for (s, n) in chunks for h in heads]
        lower, attn, rhs2, kdec, qg, glast = {}, {}, {}, {}, {}, {}
        for (s, n) in chunks:
            r0 = s * l + n * c
            rs = slice(r0, r0 + c)
            gc_rows = lax.dot_general(pick, gc_all[rs], (((1,), (1,)), ((), ())), precision=HIGHEST,
                                      preferred_element_type=F32)
            for h in heads:
                q = qkv[rs, h * A_DK:(h + 1) * A_DK]
                k = qkv[rs, A_QK_W + h * A_DK:A_QK_W + (h + 1) * A_DK]
                v = qkv[rs, 2 * A_QK_W + h * A_DV:2 * A_QK_W + (h + 1) * A_DV]
                q = q * lax.rsqrt(jnp.sum(q * q, axis=-1, keepdims=True) + EPS) * (A_DK ** -0.5)
                k = k * lax.rsqrt(jnp.sum(k * k, axis=-1, keepdims=True) + EPS)
                if masked:
                    valid_c = _rows((c, A_DK)) < valid_rows - n * c
                    k = jnp.where(valid_c, k, 0.0)
                    v = jnp.where(valid_c, v, 0.0)
                beta = beta_all[rs, h:h + 1]
                gc = gc_all[rs, A_HEADS + h:A_HEADS + h + 1]
                gc_row = gc_rows[SUBLANE * h:SUBLANE * h + 1, :]
                dec = jnp.where(causal, jnp.exp(jnp.where(causal, gc - gc_row, 0.0)), 0.0)
                kb = k * beta
                egc = jnp.exp(gc)
                sc = _dot_nt(jnp.concatenate([kb, q], axis=0), k)
                key = (s, n, h)
                lower[key] = jnp.where(strict, -sc[:c] * dec, 0.0)
                attn[key] = sc[c:] * dec
                rhs2[key] = jnp.concatenate([v * beta, kb * egc], axis=1)
                gc_last = gc[c - 1:c, :]
                kdec[key] = k * jnp.exp(gc_last - gc)
                glast[key] = jnp.exp(gc_last)
                qg[key] = q * egc
        tinv = dict(zip(inst, _unit_lower_inverses([lower[i] for i in inst])))
        uw = {i: _dot(tinv[i], rhs2[i]) for i in inst}
        ao = {i: _dot(attn[i], uw[i]) for i in inst}
        bm = {i: _dot_tn(kdec[i], uw[i]) for i in inst}
        for (s, n) in chunks:
            r0 = s * l + n * c
            for h in heads:
                i = (s, n, h)
                st = state[(s, h)]
                o = _dot(qg[i] - ao[i][:, A_DV:], st) + ao[i][:, :A_DV]
                state[(s, h)] = st * glast[i] - _dot(bm[i][:, A_DV:], st) + bm[i][:, :A_DV]
                o = o * lax.rsqrt(jnp.mean(o * o, axis=-1, keepdims=True) + EPS)
                z = za_ref[r0:r0 + c, A_CONV_CH + h * A_DV:A_CONV_CH + (h + 1) * A_DV]
                y_ref[r0:r0 + c, h * A_DV:(h + 1) * A_DV] = o * ng * _silu(z)

    for g in range(0, len(all_chunks), group):
        advance(all_chunks[g:g + group])
    for s in range(ns):
        for h in heads:
            s_scr[s, h] = state[(s, h)]

    _emit_state(last, sout_ref, prev_s_ref, s_scr)


def _with_prev(kern, prev, shapes, ns, group_of=_group_of_2d):
    if prev is None:
        return functools.partial(kern, None, None), [], [], 0
    n_prev = prev[0].shape[0]
    return kern, list(prev), [_stacked_spec(n_prev, (ns,) + shp, group_of) for shp in shapes], n_prev


def _mixer_grid(bsz, t_pad, t_valid, l, ns):
    assert t_pad % l == 0 and bsz % ns == 0 and (ns == 1 or t_pad == l)
    nt = t_pad // l
    valid_rows = t_valid - (nt - 1) * l
    assert 0 < valid_rows <= l and (nt == 1 or valid_rows == l)
    return (bsz // ns, nt), ns * l, valid_rows


def _gdn(za, conv0, s0, convw, alog_row, dt_row, ng, *, bsz, t_pad, t_valid, c, nc, ns, group,
         layer, state_layer, prev):
    grid, rows, valid_rows = _mixer_grid(bsz, t_pad, t_valid, nc * c, ns)
    spec = functools.partial(_param_spec, layer=layer)
    nt = grid[1]
    kern = functools.partial(_gdn_kernel, c=c, nc=nc, ns=ns, nt=nt, valid_rows=valid_rows,
                             group=group)
    state_shapes = ((A_HEADS, A_DK, A_DV), (3, A_CONV_CH))
    kern, prev_args, prev_specs, n_prev = _with_prev(kern, prev, state_shapes, ns)
    return pl.pallas_call(
        kern,
        grid=grid,
        in_specs=prev_specs + [
                  pl.BlockSpec((rows, ZA_W), lambda b, t: (b * nt + t, 0)),
                  _state_spec((ns, 3, A_CONV_CH), state_layer),
                  _state_spec((ns, A_HEADS, A_DK, A_DV), state_layer),
                  spec((CONV_TAPS, A_CONV_CH)),
                  spec((1, LANE)),
                  spec((1, LANE)),
                  spec((1, A_DV))],
        out_specs=[pl.BlockSpec((rows, A_V_W), lambda b, t: (b * nt + t, 0)),
                   ] + [_stacked_spec(n_prev + 1, (ns,) + shp) for shp in state_shapes],
        out_shape=[jax.ShapeDtypeStruct((bsz * t_pad, A_V_W), F32)]
        + [jax.ShapeDtypeStruct((n_prev + 1, bsz) + shp, F32) for shp in state_shapes],
        scratch_shapes=[pltpu.VMEM((ns, CONV_PAD + nc * c, A_CONV_CH), F32),
                        pltpu.VMEM((ns, A_HEADS, A_DK, A_DV), F32)],
        compiler_params=pltpu.CompilerParams(dimension_semantics=("arbitrary", "arbitrary"),
                                             vmem_limit_bytes=VMEM_LIMIT),
        name="gdn",
    )(*prev_args, za, conv0, s0, convw, alog_row, dt_row, ng)


def _rwkv_kernel(prev_s_ref, prev_shift_ref, zb_ref, shift0_ref, s0_ref, mu_ref, w0_ref, wup_ref, a0_ref,
                 aup_ref, gup_ref, kk_ref, ka_ref, rk_ref, lnw_ref, lnb_ref, hsum_ref,
                 y_ref, sout_ref, shiftout_ref, xpad, s_scr, prep, *, c, nc, ns, nt, valid_rows, group):
    t = pl.program_id(1)
    first = t == 0
    last = t == nt - 1
    l = nc * c
    rows = ns * l

    @pl.when(first)
    def _():
        s_scr[...] = s0_ref[...]

    zb = zb_ref[...]
    (prev,) = _with_history(zb, xpad, first, shift0_ref, ns, l, 1)
    _carry_history(xpad, l, valid_rows, last, shiftout_ref, prev_shift_ref, 1)
    zs = zb + (prev - zb) * mu_ref[...]

    r_all = zs[:, 0:B_W]
    k_all = zs[:, B_W:2 * B_W]
    v_all = zs[:, 2 * B_W:3 * B_W]
    xwa = zs[:, 3 * B_W:3 * B_W + B_LORA_WA]
    xg = zs[:, 3 * B_W + B_LORA_WA:B_IN]
    w_log = -_softplus(-(w0_ref[...] + _dot(jnp.tanh(xwa), wup_ref[...]))) - 0.5
    lw = -jnp.exp(w_log)
    a_all = _sigmoid(a0_ref[...] + _dot(xwa, aup_ref[...]))
    g_all = _dot(_sigmoid(xg), gup_ref[...])
    kk_all = k_all * kk_ref[...]
    k_all = k_all * (1.0 + (a_all - 1.0) * ka_ref[...])
    masked = valid_rows < l
    if masked:
        valid = _rows_in((rows, B_W), l) < valid_rows
        lw = jnp.where(valid, lw, 0.0)
        kk_all = jnp.where(valid, kk_all, 0.0)
        k_all = jnp.where(valid, k_all, 0.0)
        v_all = jnp.where(valid, v_all, 0.0)
    gc_all = _cumsum_rows(lw, c)

    def head_sums(x):
        hi = x.astype(BF16)
        lo = (x - hi.astype(F32)).astype(BF16)
        ones = hsum_ref[...]
        return (jnp.dot(hi, ones, preferred_element_type=F32)
                + jnp.dot(lo, ones, preferred_element_type=F32))

    kkn_all = kk_all * lax.rsqrt(head_sums(kk_all * kk_all) + EPS)
    bonus_all = head_sums(r_all * k_all * rk_ref[...]) * v_all

    rr = _rows((2 * c, 2 * c))
    cc = _cols((2 * c, 2 * c))
    tt = jnp.where(rr >= c, rr - c, rr)
    ss = jnp.where(cc >= c, cc - c, cc)
    mask2 = jnp.logical_or(tt > ss, jnp.logical_and(rr >= c, tt == ss))

    prep_names = ('r', 'k', 'v', 'kkn', 'a', 'lw', 'gc', 'bonus', 'g')
    for idx, val in enumerate((r_all, k_all, v_all, kkn_all, a_all, lw, gc_all, bonus_all, g_all)):
        prep[idx] = val
    heads = range(B_HEADS)
    sls = [slice(h * B_N, (h + 1) * B_N) for h in heads]

    def at_rows(name, rs):
        return prep[prep_names.index(name), rs, :]

    def advance(g, carry):
        members = range(group)
        inst = [(j, h) for j in members for h in heads]
        amat, rhs_uw, r_abs, tails, e_last, vs, rows_of, seq_of = {}, {}, {}, {}, {}, {}, {}, {}
        for j in members:
            q = g * group + j
            rs = pl.ds(pl.multiple_of(q * c, c), c)
            rows_of[j] = rs
            seq_of[j] = 0 if ns == 1 else q
            gc = at_rows('gc', rs)
            lwc = at_rows('lw', rs)
            gmid = gc[c // 2:c // 2 + 1, :]
            glast = gc[c - 1:c, :]
            e_last[j] = jnp.exp(glast)
            r = at_rows('r', rs)
            k = at_rows('k', rs)
            v = at_rows('v', rs)
            alpha = -at_rows('kkn', rs)
            bet = at_rows('kkn', rs) * at_rows('a', rs)
            e_inv = jnp.exp(gmid - gc)
            e_tail = jnp.exp(glast - gc)
            lhs_a = alpha * jnp.exp(gc - lwc - gmid)
            lhs_r = r * jnp.exp(gc - gmid)
            rhs_b = bet * e_inv
            rhs_k = k * e_inv
            abs_a = alpha * jnp.exp(gc - lwc)
            abs_r = r * jnp.exp(gc)
            tail_b = bet * e_tail
            tail_k = k * e_tail
            for h in heads:
                sl = sls[h]
                lhs = jnp.concatenate([lhs_a[:, sl], lhs_r[:, sl]], axis=0)
                rhs = jnp.concatenate([rhs_b[:, sl], rhs_k[:, sl]], axis=0)
                key = (j, h)
                amat[key] = jnp.where(mask2, _dot_nt(lhs, rhs), 0.0)
                rhs_uw[key] = abs_a[:, sl]
                r_abs[key] = abs_r[:, sl]
                tails[key] = jnp.concatenate([tail_b[:, sl], tail_k[:, sl]], axis=0)
                vs[key] = v[:, sl]

        from_v = {i: _dot(amat[i][:, c:], vs[i]) for i in inst}
        uw = dict(zip(inst, _unit_lower_solves(
            [amat[i][:c, :c] for i in inst],
            [jnp.concatenate([from_v[i][:c], rhs_uw[i]], axis=1) for i in inst])))
        ro = {i: _dot(amat[i][c:, :c], uw[i]) for i in inst}
        mp = {i: _dot_tn(uw[i][:, B_N:], tails[i][:c]) for i in inst}
        bb = {i: _dot_tn(jnp.concatenate([uw[i][:, :B_N], vs[i]], axis=0), tails[i]) for i in inst}
        state = {}
        outs = []
        for j in members:
            per_head = []
            for h in heads:
                i = (j, h)
                skey = (0 if ns == 1 else j, h)
                st = state[skey] if skey in state else s_scr[seq_of[j], h]
                per_head.append(_dot_nt(r_abs[i] + ro[i][:, B_N:], st) + from_v[i][c:] + ro[i][:, :B_N])
                state[skey] = st * e_last[j][:, sls[h]] + _dot(st, mp[i]) + bb[i]
            outs.append(jnp.concatenate(per_head, axis=1))
        for j, o in zip(members, outs):
            d = o - head_sums(o) * (1.0 / B_N)
            var = head_sums(d * d) * (1.0 / B_N)
            o = d * lax.rsqrt(var + B_LN_EPS) * lnw_ref[...] + lnb_ref[...]
            y_ref[rows_of[j], :] = (o + at_rows('bonus', rows_of[j])) * at_rows('g', rows_of[j])
        for (sk, h), st in state.items():
            s_scr[seq_of[sk], h] = st
        return carry

    assert (ns * nc) % group == 0 and (ns == 1 or nc == 1)
    lax.fori_loop(0, ns * nc // group, advance, 0)

    _emit_state(last, sout_ref, prev_s_ref, s_scr)


def _rwkv(zb, shift0, s0, prm, *, bsz, t_pad, t_valid, c, nc, ns, group, layer, state_layer, prev):
    grid, rows, valid_rows = _mixer_grid(bsz, t_pad, t_valid, nc * c, ns)
    nt = grid[1]
    kern = functools.partial(_rwkv_kernel, c=c, nc=nc, ns=ns, nt=nt, valid_rows=valid_rows,
                             group=group)
    spec = functools.partial(_param_spec, layer=layer)
    row = lambda wd: spec((1, wd))
    state_shapes = ((B_HEADS, B_N, B_N), (1, B_IN))
    kern, prev_args, prev_specs, n_prev = _with_prev(kern, prev, state_shapes, ns)
    return pl.pallas_call(
        kern,
        grid=grid,
        in_specs=prev_specs + [
                  pl.BlockSpec((rows, ZB_W), lambda b, t: (b * nt + t, 0)),
                  _state_spec((ns, 1, B_IN), state_layer),
                  _state_spec((ns, B_HEADS, B_N, B_N), state_layer),
                  row(B_IN), row(B_W), spec((B_LORA_WA, B_W)), row(B_W),
                  spec((B_LORA_WA, B_W)), spec((B_GATE_LORA, B_W)),
                  row(B_W), row(B_W), row(B_W), row(B_W), row(B_W), _param_spec((B_W, B_W), 0)],
        out_specs=[pl.BlockSpec((rows, B_W), lambda b, t: (b * nt + t, 0)),
                   ] + [_stacked_spec(n_prev + 1, (ns,) + shp) for shp in state_shapes],
        out_shape=[jax.ShapeDtypeStruct((bsz * t_pad, B_W), F32)]
        + [jax.ShapeDtypeStruct((n_prev + 1, bsz) + shp, F32) for shp in state_shapes],
        scratch_shapes=[pltpu.VMEM((ns, CONV_PAD + nc * c, B_IN), F32),
                        pltpu.VMEM((ns, B_HEADS, B_N, B_N), F32),
                        pltpu.VMEM((9, rows, B_W), F32)],
        compiler_params=pltpu.CompilerParams(dimension_semantics=("arbitrary", "arbitrary"),
                                             vmem_limit_bytes=VMEM_LIMIT),
        name="rwkv",
    )(*prev_args, zb, shift0, s0, *prm)


def _rglru_block(zc, y_ref, refs, scratch, first, last, *, l, ns, valid_rows):
    prev_h_ref, prev_conv_ref, conv0_ref, h0_ref, convw_ref, convb_ref, wg_ref, bg_ref, l_ref, \
        hout_ref, convout_ref = refs
    xpad, h_scr = scratch
    rows = ns * l

    @pl.when(first)
    def _():
        h_scr[...] = h0_ref[...]

    xc = _causal_conv(zc[:, 0:C_WIDTH], xpad, convw_ref, first, conv0_ref, ns, l) + convb_ref[...]

    def hand_over():
        _carry_history(xpad, l, valid_rows, last, convout_ref, prev_conv_ref, CONV_TAPS - 1)
        _emit_state(last, hout_ref, prev_h_ref, h_scr)

    gates = _sigmoid(_dot(xc, wg_ref[...]) + bg_ref[...])
    r = gates[:, :C_WIDTH]
    i = gates[:, C_WIDTH:]
    log_a = -C_POW * r * _softplus(-l_ref[...])
    a = jnp.exp(log_a)
    b = jnp.sqrt(-jnp.tanh(log_a) * (a * a + 1.0)) * (i * xc)
    if valid_rows < l:
        valid = _rows_in((rows, C_WIDTH), l) < valid_rows
        a = jnp.where(valid, a, 1.0)
        b = jnp.where(valid, b, 0.0)
    s = 1
    while s < SUBLANE:
        b = a * _shift_rows(b, s, 0.0, SUBLANE) + b
        a = a * _shift_rows(a, s, 1.0, SUBLANE)
        s *= 2
    gate = _gelu_tanh(zc[:, C_WIDTH:])
    for q in range(ns):
        carry = h_scr[q]
        for r0 in range(q * l, (q + 1) * l, SUBLANE):
            rs = slice(r0, r0 + SUBLANE)
            hseq = a[rs] * carry + b[rs]
            carry = hseq[SUBLANE - 1:SUBLANE, :]
            y_ref[rs, :] = hseq * gate[rs]
        h_scr[q] = carry
    return hand_over


FF_CHUNK = 1024


def _merge_ffn_kernel(x_ref, ya_ref, yb_ref, yc_ref, zg_ref, wbr_ref, wout_ref, g2_ref,
                      wup_ref, wdown_ref, gf_ref, o_ref, *, final):
    merged = None
    for n, y_ref in enumerate((ya_ref, yb_ref, yc_ref)):
        proj = _dot(y_ref[...], wbr_ref[n])
        term = _sigmoid(zg_ref[:, n * D_MODEL:(n + 1) * D_MODEL]) * proj
        merged = term if merged is None else merged + term
    x = x_ref[...] + _dot(merged, wout_ref[...])
    h = x * lax.rsqrt(jnp.mean(x * x, axis=-1, keepdims=True) + EPS) * g2_ref[...]
    h = h.astype(BF16)
    acc = x
    for j in range(0, D_FF, FF_CHUNK):
        up = jnp.dot(h, wup_ref[:, j:j + FF_CHUNK], preferred_element_type=F32)
        acc = acc + _dot(jnp.square(jnp.maximum(up, 0.0)), wdown_ref[j:j + FF_CHUNK, :])
    if final:
        acc = acc * lax.rsqrt(jnp.mean(acc * acc, axis=-1, keepdims=True) + EPS) * gf_ref[...]
    o_ref[...] = acc


def _merge_ffn(x, ya, yb, yc, zg, wbr, wout, g2, wup, wdown, gf, tm, final, layer):
    m = x.shape[0]
    spec = functools.partial(_param_spec, layer=layer)
    tile = lambda wd: pl.BlockSpec((tm, wd), lambda i: (i, 0))
    return pl.pallas_call(
        functools.partial(_merge_ffn_kernel, final=final),
        grid=(m // tm,),
        in_specs=[tile(D_MODEL), tile(A_V_W), tile(B_W), tile(C_WIDTH), tile(ZG_W),
                  spec((N_BRANCH, A_V_W, D_MODEL)), spec((D_MODEL, D_MODEL)),
                  spec((1, D_MODEL)), spec((D_MODEL, D_FF)),
                  spec((D_FF, D_MODEL)), _param_spec((1, D_MODEL), 0)],
        out_specs=tile(D_MODEL),
        out_shape=jax.ShapeDtypeStruct((m, D_MODEL), F32),
        compiler_params=pltpu.CompilerParams(dimension_semantics=("arbitrary",),
                                             vmem_limit_bytes=VMEM_LIMIT),
        name="merge_ffn",
    )(x, ya, yb, yc, zg, wbr, wout, g2, wup, wdown, gf)


def _block_diag(w):
    g, n, _ = w.shape
    eye = jnp.eye(g, dtype=w.dtype)
    return (eye[:, None, :, None] * w[:, :, None, :]).reshape(g * n, g * n)


def _prepare_params(p):
    depth = p['w_in'].shape[0]
    row = lambda a: a.reshape(depth, 1, -1).astype(F32)
    w_in = p['w_in'].astype(BF16)
    lane_row = lambda a: jnp.zeros((depth, 1, LANE), F32).at[:, 0, A_HEADS:2 * A_HEADS].set(a)
    zeros_lora = jnp.zeros((depth, B_LORA_WA // 2, B_W), F32)
    block_diag = jax.vmap(_block_diag)
    gdn = (p['a_conv_w'], lane_row(p['a_A_log']), lane_row(p['a_dt_bias']), row(p['a_norm_g']))
    rwkv = (row(p['b_mu']), row(p['b_w0']),
            jnp.concatenate([p['b_w_up'], zeros_lora], axis=1).astype(BF16),
            row(p['b_a0']),
            jnp.concatenate([zeros_lora, p['b_a_up']], axis=1).astype(BF16),
            p['b_g_up'].astype(BF16),
            row(p['b_k_k']), row(p['b_k_a']), row(p['b_r_k']), row(p['b_ln_w']), row(p['b_ln_b']),
            jnp.kron(jnp.eye(B_HEADS, dtype=F32), jnp.ones((B_N, B_N), F32)).astype(BF16)[None])
    rglru = (p['c_conv_w'], row(p['c_conv_b']),
             jnp.concatenate([block_diag(p['c_wa']), block_diag(p['c_wx'])], axis=2).astype(BF16),
             jnp.concatenate([row(p['c_ba']), row(p['c_bx'])], axis=2), row(p['c_L']))
    ffn = (p['w_branch'].astype(BF16), p['w_out'].astype(BF16), row(p['norm2_g']),
           p['w_up'].astype(BF16), p['w_down'].astype(BF16))
    return dict(norm1_g=row(p['norm1_g']), w_in=w_in, gdn=gdn, rwkv=rwkv, rglru=rglru, ffn=ffn)


PROMPT_BLOCKING = dict(c=64, nc=4, ns=1, gdn_group=4, rwkv_group=4)
SAMPLE_BLOCKING = dict(c=SUBLANE, nc=1, ns=8, gdn_group=8, rwkv_group=4)
TOKEN_TILE = 256


def _layer(x, states, new_states, lp, gf, *, layer, state_layer, bsz, t_pad, t_valid, blocking, final):
    a_s, a_conv, b_s, b_shift, c_h, c_conv = states
    prev = (lambda i, j: None) if new_states is None else (lambda i, j: (new_states[i], new_states[j]))
    kw = dict(bsz=bsz, t_pad=t_pad, t_valid=t_valid, layer=layer, state_layer=state_layer)
    chunked = dict(c=blocking['c'], nc=blocking['nc'], ns=blocking['ns'], **kw)
    za, zb, yc, zg, c_h, c_conv = _in_proj(x, lp['norm1_g'], lp['w_in'], c_conv,
                                           c_h.reshape(-1, bsz, 1, C_WIDTH), lp['rglru'], tm=TOKEN_TILE,
                                           prev=prev(4, 5), **kw)
    ya, a_s, a_conv = _gdn(za, a_conv, a_s, *lp['gdn'], group=blocking['gdn_group'], prev=prev(0, 1),
                           **chunked)
    yb, b_s, b_shift = _rwkv(zb, b_shift, b_s, lp['rwkv'], group=blocking['rwkv_group'], prev=prev(2, 3),
                             **chunked)
    x = _merge_ffn(x, ya, yb, yc, zg, *lp['ffn'], gf, TOKEN_TILE, final, layer)
    return x, (a_s, a_conv, b_s, b_shift, c_h, c_conv)


def kernel(x_prompt, x_sample, state_a_S, state_a_conv, state_b_S, state_b_shift, state_c_h, state_c_conv, norm1_g, w_in, a_conv_w, a_A_log, a_dt_bias, a_norm_g, b_mu, b_w0, b_w_up, b_a0, b_a_up, b_g_up, b_k_k, b_k_a, b_r_k, b_ln_w, b_ln_b, c_conv_w, c_conv_b, c_wa, c_ba, c_wx, c_bx, c_L, w_branch, w_out, norm2_g, w_up, w_down, final_norm_g):
    p = dict(norm1_g=norm1_g, w_in=w_in, a_conv_w=a_conv_w, a_A_log=a_A_log, a_dt_bias=a_dt_bias,
             a_norm_g=a_norm_g, b_mu=b_mu, b_w0=b_w0, b_w_up=b_w_up, b_a0=b_a0, b_a_up=b_a_up,
             b_g_up=b_g_up, b_k_k=b_k_k, b_k_a=b_k_a, b_r_k=b_r_k, b_ln_w=b_ln_w, b_ln_b=b_ln_b,
             c_conv_w=c_conv_w, c_conv_b=c_conv_b, c_wa=c_wa, c_ba=c_ba, c_wx=c_wx, c_bx=c_bx,
             c_L=c_L, w_branch=w_branch, w_out=w_out, norm2_g=norm2_g, w_up=w_up, w_down=w_down)
    bp, tp, _ = x_prompt.shape
    bs, ts, _ = x_sample.shape
    ts_pad = -(-ts // SUBLANE) * SUBLANE
    assert ts_pad == SAMPLE_BLOCKING['c'] and tp % (PROMPT_BLOCKING['c'] * PROMPT_BLOCKING['nc']) == 0
    gf = final_norm_g.reshape(1, 1, D_MODEL)

    xp = x_prompt.reshape(bp * tp, D_MODEL)
    xs = jnp.pad(x_sample, ((0, 0), (0, ts_pad - ts), (0, 0))).reshape(bs * ts_pad, D_MODEL)
    zero = lambda *shape: jnp.zeros((1,) + shape, F32)
    p_states = (zero(bp, A_HEADS, A_DK, A_DV), zero(bp, 3, A_CONV_CH),
                zero(bp, B_HEADS, B_N, B_N), zero(bp, 1, B_IN),
                zero(bp, C_WIDTH), zero(bp, 3, C_WIDTH))
    s_states = (state_a_S, state_a_conv, state_b_S, state_b_shift, state_c_h, state_c_conv)
    lp = _prepare_params(p)
    p_new = s_new = None
    for l in range(DEPTH):
        final = l == DEPTH - 1
        xp, p_new = _layer(xp, p_states, p_new, lp, gf, layer=l, state_layer=0, bsz=bp, t_pad=tp,
                           t_valid=tp, blocking=PROMPT_BLOCKING, final=final)
        xs, s_new = _layer(xs, s_states, s_new, lp, gf, layer=l, state_layer=l, bsz=bs, t_pad=ts_pad,
                           t_valid=ts, blocking=SAMPLE_BLOCKING, final=final)
    squeeze_h = lambda st: st[:4] + (st[4].reshape(DEPTH, -1, C_WIDTH), st[5])
    y_prompt = xp.reshape(bp, tp, D_MODEL)
    y_sample = xs.reshape(bs, ts_pad, D_MODEL)[:, :ts]
    return (y_prompt, y_sample) + squeeze_h(p_new) + squeeze_h(s_new)
```

```python
import functools

import jax
import jax.numpy as jnp
from jax import lax
from jax.experimental import pallas as pl
from jax.experimental.pallas import tpu as pltpu

F32 = jnp.float32
BF16 = jnp.bfloat16
HIGHEST = lax.Precision.HIGHEST

EPS = 1e-6
D_MODEL = 1024
DEPTH = 2
A_HEADS = 4
A_DK = 128
A_DV = 128
A_QK_W = A_HEADS * A_DK
A_V_W = A_HEADS * A_DV
A_CONV_CH = 2 * A_QK_W + A_V_W
B_N = 64
B_HEADS = 8
B_W = B_HEADS * B_N
B_LORA_WA = 128
B_GATE_LORA = 128
B_IN = 3 * B_W + B_LORA_WA + B_GATE_LORA
B_LN_EPS = 64e-5
C_WIDTH = 512
C_BLOCKS = 8
C_POW = 8.0
N_BRANCH = 3
D_FF = 4 * D_MODEL

LANE = 128
SUBLANE = 8
CONV_TAPS = 4
CONV_PAD = SUBLANE

ZA_W = A_CONV_CH + A_V_W + LANE
ZA_BA = A_CONV_CH + A_V_W
ZB_W = B_IN
ZC_W = 2 * C_WIDTH
ZG_W = N_BRANCH * D_MODEL
Z_WIDTHS = (ZA_W, ZB_W, ZC_W, ZG_W)
A_IN = A_CONV_CH + A_V_W + 2 * A_HEADS
N_IN = A_IN + ZB_W + ZC_W + ZG_W

VMEM_LIMIT = 56 * 1024 * 1024
PROJ_PIECE = 512


def _dot(a, b):
    return jnp.dot(a.astype(BF16), b.astype(BF16), preferred_element_type=F32)


def _dot_nt(a, b):
    return lax.dot_general(a.astype(BF16), b.astype(BF16), (((1,), (1,)), ((), ())),
                           preferred_element_type=F32)


def _dot_tn(a, b):
    return lax.dot_general(a.astype(BF16), b.astype(BF16), (((0,), (0,)), ((), ())),
                           preferred_element_type=F32)


def _sigmoid(x):
    return 1.0 / (1.0 + jnp.exp(-x))


def _silu(x):
    return x * _sigmoid(x)


def _softplus(x):
    return jnp.maximum(x, 0.0) + jnp.log1p(jnp.exp(-jnp.abs(x)))


def _gelu_tanh(x):
    return 0.5 * x * (1.0 + jnp.tanh(0.7978845608028654 * (x + 0.044715 * (x * x * x))))


def _rows(shape):
    return lax.broadcasted_iota(jnp.int32, shape, 0)


def _cols(shape):
    return lax.broadcasted_iota(jnp.int32, shape, 1)


def _rows_in(shape, period):
    r = _rows(shape)
    return r if period >= shape[0] else r & (period - 1)


def _shift_rows(x, s, fill, period):
    return jnp.where(_rows_in(x.shape, period) >= s, pltpu.roll(x, s, 0), fill)


def _cumsum_rows(x, period):
    s = 1
    while s < period:
        x = x + _shift_rows(x, s, 0.0, period)
        s *= 2
    return x


def _bf16_pieces(x, n):
    pieces = []
    for _ in range(n):
        p = x.astype(BF16)
        pieces.append(p)
        x = x - p.astype(F32)
    return pieces


def _cumsum_chunks(x, c):
    tri = (_rows((c, c)) >= _cols((c, c))).astype(BF16)
    pieces = _bf16_pieces(x, 3)
    chunks = [sum(jnp.dot(tri, p[r0:r0 + c], preferred_element_type=F32) for p in pieces)
              for r0 in range(0, x.shape[0], c)]
    return chunks[0] if len(chunks) == 1 else jnp.concatenate(chunks, axis=0)


def _unit_lower_inverses(ns):
    c = ns[0].shape[0]
    eye = (_rows(ns[0].shape) == _cols(ns[0].shape)).astype(F32)
    ts = [eye + n for n in ns]
    ps = list(ns)
    s = 2
    while s < c:
        ps = [_dot(p, p) for p in ps]
        ts = [t + _dot(t, p) for t, p in zip(ts, ps)]
        s *= 2
    return ts


def _unit_lower_solves(ns, xs):
    c = ns[0].shape[0]
    xs = [x + _dot(n, x) for n, x in zip(ns, xs)]
    ps = list(ns)
    s = 2
    while s < c:
        ps = [_dot(p, p) for p in ps]
        xs = [x + _dot(p, x) for p, x in zip(ps, xs)]
        s *= 2
    return xs


def _param_spec(shape, layer):
    nd = len(shape)
    return pl.BlockSpec((None,) + shape, lambda *_: (layer,) + (0,) * nd, pipeline_mode=pl.Buffered(1))


def _group_of_2d(b, t):
    return b


def _stacked_spec(layers, shape, group_of=_group_of_2d):
    nd = len(shape) - 1
    return pl.BlockSpec((layers,) + shape, lambda *g: (0, group_of(*g)) + (0,) * nd)


def _emit_state(last, out_ref, prev_ref, new_ref):
    @pl.when(last)
    def _():
        n_prev = out_ref.shape[0] - 1
        if n_prev:
            out_ref[0:n_prev] = prev_ref[...]
        out_ref[n_prev] = new_ref[...]


def _state_spec(shape, layer, group_of=_group_of_2d):
    nd = len(shape) - 1
    return pl.BlockSpec((None,) + shape, lambda *g: (layer, group_of(*g)) + (0,) * nd)


def _in_proj_kernel(prev_h_ref, prev_conv_ref, x_ref, g_ref, w_ref, conv0_ref, h0_ref, convw_ref, convb_ref,
                    wg_ref, bg_ref, l_ref, za_ref, zb_ref, yc_ref, zg_ref, hout_ref, convout_ref,
                    w_rest, xpad, h_scr, *, l, ns, nt, valid_rows):
    step = pl.program_id(0)
    t = step % nt
    @pl.when(step == 0)
    def _():
        w_rest[...] = w_ref[:, A_IN:]

    x = x_ref[...]
    h = x * lax.rsqrt(jnp.mean(x * x, axis=-1, keepdims=True) + EPS) * g_ref[...]
    h = h.astype(BF16)
    proj = lambda ref, off, width: jnp.dot(h, ref[:, off:off + width], preferred_element_type=F32)

    def project(out_ref, w, off, width):
        def piece(lo):
            hi = min(lo + PROJ_PIECE, width)
            out_ref[:, lo:hi] = proj(w, off + lo, hi - lo)
        return [functools.partial(piece, lo) for lo in range(0, width, PROJ_PIECE)]

    zc = proj(w_rest, ZB_W, ZC_W)
    pieces = (project(za_ref, w_ref, 0, ZA_W) + project(zb_ref, w_rest, 0, ZB_W)
              + project(zg_ref, w_rest, ZB_W + ZC_W, ZG_W))
    stages = _rglru_block(
        zc, yc_ref,
        (prev_h_ref, prev_conv_ref, conv0_ref, h0_ref, convw_ref, convb_ref, wg_ref, bg_ref, l_ref,
         hout_ref, convout_ref),
        (xpad, h_scr), t == 0, t == nt - 1, l=l, ns=ns, valid_rows=valid_rows)
    hand_over = None
    for hand_over in stages:
        if pieces:
            pieces.pop(0)()
    for piece in pieces:
        piece()
    hand_over()


def _in_proj(x, g, w, conv0, h0, prm, *, tm, bsz, t_pad, t_valid, layer, state_layer, prev):
    m = x.shape[0]
    l = min(tm, t_pad)
    ns = tm // l
    _, _, valid_rows = _mixer_grid(bsz, t_pad, t_valid, l, ns)
    nt = t_pad // l
    group_of = lambda i: i // nt
    spec = functools.partial(_param_spec, layer=layer)
    kern = functools.partial(_in_proj_kernel, l=l, ns=ns, nt=nt, valid_rows=valid_rows)
    state_shapes = ((1, C_WIDTH), (3, C_WIDTH))
    kern, prev_args, prev_specs, n_prev = _with_prev(kern, prev, state_shapes, ns, group_of)
    tile = lambda wd: pl.BlockSpec((tm, wd), lambda i: (i, 0))
    widths = (ZA_W, ZB_W, C_WIDTH, ZG_W)
    return pl.pallas_call(
        kern,
        grid=(m // tm,),
        in_specs=prev_specs + [
                  tile(D_MODEL), spec((1, D_MODEL)), spec((D_MODEL, N_IN)),
                  _state_spec((ns, 3, C_WIDTH), state_layer, group_of),
                  _state_spec((ns, 1, C_WIDTH), state_layer, group_of),
                  spec((CONV_TAPS, C_WIDTH)), spec((1, C_WIDTH)),
                  spec((C_WIDTH, 2 * C_WIDTH)), spec((1, 2 * C_WIDTH)),
                  spec((1, C_WIDTH))],
        out_specs=[tile(wd) for wd in widths]
        + [_stacked_spec(n_prev + 1, (ns,) + shp, group_of) for shp in state_shapes],
        out_shape=[jax.ShapeDtypeStruct((m, wd), F32) for wd in widths]
        + [jax.ShapeDtypeStruct((n_prev + 1, bsz) + shp, F32) for shp in state_shapes],
        scratch_shapes=[pltpu.VMEM((D_MODEL, N_IN - A_IN), BF16),
                        pltpu.VMEM((ns, CONV_PAD + l, C_WIDTH), F32),
                        pltpu.VMEM((ns, 1, C_WIDTH), F32)],
        compiler_params=pltpu.CompilerParams(dimension_semantics=("arbitrary",),
                                             vmem_limit_bytes=VMEM_LIMIT),
        name="in_proj",
    )(*prev_args, x, g, w, conv0, h0, *prm)


def _with_history(x, xpad, first, hist_ref, ns, l, depth):
    @pl.when(first)
    def _():
        xpad[:, CONV_PAD - depth:CONV_PAD, :] = hist_ref[...]

    for s in range(ns):
        xpad[s, CONV_PAD:CONV_PAD + l, :] = x[s * l:(s + 1) * l]
    views = []
    for j in range(depth, 0, -1):
        parts = [xpad[s, CONV_PAD - j:CONV_PAD - j + l, :] for s in range(ns)]
        views.append(parts[0] if ns == 1 else jnp.concatenate(parts, axis=0))
    return views


def _carry_history(xpad, l, valid_rows, last, out_ref, prev_ref, depth):
    _emit_state(last, out_ref, prev_ref,
                xpad.at[:, CONV_PAD - depth + valid_rows:CONV_PAD + valid_rows, :])

    nxt = xpad[:, CONV_PAD - depth + l:CONV_PAD + l, :]
    xpad[:, CONV_PAD - depth:CONV_PAD, :] = nxt


def _causal_conv(x, xpad, w_ref, first, hist_ref, ns, l):
    y = x * w_ref[CONV_TAPS - 1:CONV_TAPS, :]
    for j, view in enumerate(_with_history(x, xpad, first, hist_ref, ns, l, CONV_TAPS - 1)):
        y = y + view * w_ref[j:j + 1, :]
    return y


def _gdn_kernel(prev_s_ref, prev_conv_ref, za_ref, conv0_ref, s0_ref, convw_ref, alog_ref, dt_ref, ng_ref,
                y_ref, sout_ref, convout_ref, xpad, s_scr, *, c, nc, ns, nt, valid_rows, group):
    t = pl.program_id(1)
    first = t == 0
    last = t == nt - 1
    l = nc * c
    rows = ns * l

    @pl.when(first)
    def _():
        s_scr[...] = s0_ref[...]

    qkv = _silu(_causal_conv(za_ref[:, 0:A_CONV_CH], xpad, convw_ref, first, conv0_ref, ns, l))
    _carry_history(xpad, l, valid_rows, last, convout_ref, prev_conv_ref, CONV_TAPS - 1)

    ba = za_ref[:, ZA_BA:ZA_BA + LANE]
    ba = jnp.where(_cols(ba.shape) < 2 * A_HEADS, ba, 0.0)
    beta_all = _sigmoid(ba)
    g_all = -jnp.exp(alog_ref[...]) * _softplus(ba + dt_ref[...])
    masked = valid_rows < l
    if masked:
        valid = _rows_in((rows, LANE), l) < valid_rows
        beta_all = jnp.where(valid, beta_all, 0.0)
        g_all = jnp.where(valid, g_all, 0.0)
    gc_all = _cumsum_rows(g_all, c)

    rr = _rows((c, c))
    cc = _cols((c, c))
    causal = rr >= cc
    strict = rr > cc
    ng = ng_ref[...]
    pick = (_cols((SUBLANE * A_HEADS, LANE))
            == A_HEADS + _rows((SUBLANE * A_HEADS, LANE)) // SUBLANE).astype(F32)

    all_chunks = [(s, n) for n in range(nc) for s in range(ns)]
    heads = range(A_HEADS)
    state = {(s, h): s_scr[s, h] for s in range(ns) for h in heads}

    def advance(chunks):
        inst = [(s, n, h) for (s, n) in chunks for h in heads]
        lower, attn, rhs2, kdec, qg, glast = {}, {}, {}, {}, {}, {}
        for (s, n) in chunks:
            r0 = s * l + n * c
            rs = slice(r0, r0 + c)
            gc_rows = lax.dot_general(pick, gc_all[rs], (((1,), (1,)), ((), ())), precision=HIGHEST,
                                      preferred_element_type=F32)
            for h in heads:
                q = qkv[rs, h * A_DK:(h + 1) * A_DK]
                k = qkv[rs, A_QK_W + h * A_DK:A_QK_W + (h + 1) * A_DK]
                v = qkv[rs, 2 * A_QK_W + h * A_DV:2 * A_QK_W + (h + 1) * A_DV]
                q = q * lax.rsqrt(jnp.sum(q * q, axis=-1, keepdims=True) + EPS) * (A_DK ** -0.5)
                k = k * lax.rsqrt(jnp.sum(k * k, axis=-1, keepdims=True) + EPS)
                if masked:
                    valid_c = _rows((c, A_DK)) < valid_rows - n * c
                    k = jnp.where(valid_c, k, 0.0)
                    v = jnp.where(valid_c, v, 0.0)
                beta = beta_all[rs, h:h + 1]
                gc = gc_all[rs, A_HEADS + h:A_HEADS + h + 1]
                gc_row = gc_rows[SUBLANE * h:SUBLANE * h + 1, :]
                dec = jnp.where(causal, jnp.exp(jnp.where(causal, gc - gc_row, 0.0)), 0.0)
                kb = k * beta
                egc = jnp.exp(gc)
                sc = _dot_nt(jnp.concatenate([kb, q], axis=0), k)
                key = (s, n, h)
                lower[key] = jnp.where(strict, -sc[:c] * dec, 0.0)
                attn[key] = sc[c:] * dec
                rhs2[key] = jnp.concatenate([v * beta, kb * egc], axis=1)
                gc_last = gc[c - 1:c, :]
                kdec[key] = k * jnp.exp(gc_last - gc)
                glast[key] = jnp.exp(gc_last)
                qg[key] = q * egc
        tinv = dict(zip(inst, _unit_lower_inverses([lower[i] for i in inst])))
        uw = {i: _dot(tinv[i], rhs2[i]) for i in inst}
        ao = {i: _dot(attn[i], uw[i]) for i in inst}
        bm = {i: _dot_tn(kdec[i], uw[i]) for i in inst}
        for (s, n) in chunks:
            r0 = s * l + n * c
            for h in heads:
                i = (s, n, h)
                st = state[(s, h)]
                o = _dot(qg[i] - ao[i][:, A_DV:], st) + ao[i][:, :A_DV]
                state[(s, h)] = st * glast[i] - _dot(bm[i][:, A_DV:], st) + bm[i][:, :A_DV]
                o = o * lax.rsqrt(jnp.mean(o * o, axis=-1, keepdims=True) + EPS)
                z = za_ref[r0:r0 + c, A_CONV_CH + h * A_DV:A_CONV_CH + (h + 1) * A_DV]
                y_ref[r0:r0 + c, h * A_DV:(h + 1) * A_DV] = o * ng * _silu(z)

    for g in range(0, len(all_chunks), group):
        advance(all_chunks[g:g + group])
    for s in range(ns):
        for h in heads:
            s_scr[s, h] = state[(s, h)]

    _emit_state(last, sout_ref, prev_s_ref, s_scr)


def _with_prev(kern, prev, shapes, ns, group_of=_group_of_2d):
    if prev is None:
        return functools.partial(kern, None, None), [], [], 0
    n_prev = prev[0].shape[0]
    return kern, list(prev), [_stacked_spec(n_prev, (ns,) + shp, group_of) for shp in shapes], n_prev


def _mixer_grid(bsz, t_pad, t_valid, l, ns):
    assert t_pad % l == 0 and bsz % ns == 0 and (ns == 1 or t_pad == l)
    nt = t_pad // l
    valid_rows = t_valid - (nt - 1) * l
    assert 0 < valid_rows <= l and (nt == 1 or valid_rows == l)
    return (bsz // ns, nt), ns * l, valid_rows


def _gdn(za, conv0, s0, convw, alog_row, dt_row, ng, *, bsz, t_pad, t_valid, c, nc, ns, group,
         layer, state_layer, prev):
    grid, rows, valid_rows = _mixer_grid(bsz, t_pad, t_valid, nc * c, ns)
    spec = functools.partial(_param_spec, layer=layer)
    nt = grid[1]
    kern = functools.partial(_gdn_kernel, c=c, nc=nc, ns=ns, nt=nt, valid_rows=valid_rows,
                             group=group)
    state_shapes = ((A_HEADS, A_DK, A_DV), (3, A_CONV_CH))
    kern, prev_args, prev_specs, n_prev = _with_prev(kern, prev, state_shapes, ns)
    return pl.pallas_call(
        kern,
        grid=grid,
        in_specs=prev_specs + [
                  pl.BlockSpec((rows, ZA_W), lambda b, t: (b * nt + t, 0)),
                  _state_spec((ns, 3, A_CONV_CH), state_layer),
                  _state_spec((ns, A_HEADS, A_DK, A_DV), state_layer),
                  spec((CONV_TAPS, A_CONV_CH)),
                  spec((1, LANE)),
                  spec((1, LANE)),
                  spec((1, A_DV))],
        out_specs=[pl.BlockSpec((rows, A_V_W), lambda b, t: (b * nt + t, 0)),
                   ] + [_stacked_spec(n_prev + 1, (ns,) + shp) for shp in state_shapes],
        out_shape=[jax.ShapeDtypeStruct((bsz * t_pad, A_V_W), F32)]
        + [jax.ShapeDtypeStruct((n_prev + 1, bsz) + shp, F32) for shp in state_shapes],
        scratch_shapes=[pltpu.VMEM((ns, CONV_PAD + nc * c, A_CONV_CH), F32),
                        pltpu.VMEM((ns, A_HEADS, A_DK, A_DV), F32)],
        compiler_params=pltpu.CompilerParams(dimension_semantics=("arbitrary", "arbitrary"),
                                             vmem_limit_bytes=VMEM_LIMIT),
        name="gdn",
    )(*prev_args, za, conv0, s0, convw, alog_row, dt_row, ng)


def _rwkv_kernel(prev_s_ref, prev_shift_ref, zb_ref, shift0_ref, s0_ref, mu_ref, w0_ref, wup_ref, a0_ref,
                 aup_ref, gup_ref, kk_ref, ka_ref, rk_ref, lnw_ref, lnb_ref, hsum_ref,
                 y_ref, sout_ref, shiftout_ref, xpad, s_scr, prep, *, c, nc, ns, nt, valid_rows, group):
    t = pl.program_id(1)
    first = t == 0
    last = t == nt - 1
    l = nc * c
    rows = ns * l

    @pl.when(first)
    def _():
        s_scr[...] = s0_ref[...]

    zb = zb_ref[...]
    (prev,) = _with_history(zb, xpad, first, shift0_ref, ns, l, 1)
    _carry_history(xpad, l, valid_rows, last, shiftout_ref, prev_shift_ref, 1)
    zs = zb + (prev - zb) * mu_ref[...]

    r_all = zs[:, 0:B_W]
    k_all = zs[:, B_W:2 * B_W]
    v_all = zs[:, 2 * B_W:3 * B_W]
    xwa = zs[:, 3 * B_W:3 * B_W + B_LORA_WA]
    xg = zs[:, 3 * B_W + B_LORA_WA:B_IN]
    w_log = -_softplus(-(w0_ref[...] + _dot(jnp.tanh(xwa), wup_ref[...]))) - 0.5
    lw = -jnp.exp(w_log)
    a_all = _sigmoid(a0_ref[...] + _dot(xwa, aup_ref[...]))
    g_all = _dot(_sigmoid(xg), gup_ref[...])
    kk_all = k_all * kk_ref[...]
    k_all = k_all * (1.0 + (a_all - 1.0) * ka_ref[...])
    masked = valid_rows < l
    if masked:
        valid = _rows_in((rows, B_W), l) < valid_rows
        lw = jnp.where(valid, lw, 0.0)
        kk_all = jnp.where(valid, kk_all, 0.0)
        k_all = jnp.where(valid, k_all, 0.0)
        v_all = jnp.where(valid, v_all, 0.0)
    gc_all = _cumsum_chunks(lw, c)

    def head_sums(x, pieces=1):
        return sum(jnp.dot(p, hsum_ref[...], preferred_element_type=F32) for p in _bf16_pieces(x, pieces))

    kkn_all = kk_all * lax.rsqrt(head_sums(kk_all * kk_all, pieces=2) + EPS)
    bonus_all = head_sums(r_all * k_all * rk_ref[...]) * v_all

    rr = _rows((2 * c, 2 * c))
    cc = _cols((2 * c, 2 * c))
    tt = jnp.where(rr >= c, rr - c, rr)
    ss = jnp.where(cc >= c, cc - c, cc)
    mask2 = jnp.logical_or(tt > ss, jnp.logical_and(rr >= c, tt == ss))

    prep_names = ('r', 'k', 'v', 'kkn', 'a', 'lw', 'gc', 'bonus', 'g')
    for idx, val in enumerate((r_all, k_all, v_all, kkn_all, a_all, lw, gc_all, bonus_all, g_all)):
        prep[idx] = val
    heads = range(B_HEADS)
    sls = [slice(h * B_N, (h + 1) * B_N) for h in heads]

    def at_rows(name, rs):
        return prep[prep_names.index(name), rs, :]

    def advance(g, carry):
        members = range(group)
        inst = [(j, h) for j in members for h in heads]
        amat, rhs_uw, r_abs, tails, e_last, vs, rows_of, seq_of = {}, {}, {}, {}, {}, {}, {}, {}
        for j in members:
            q = g * group + j
            rs = pl.ds(pl.multiple_of(q * c, c), c)
            rows_of[j] = rs
            seq_of[j] = 0 if ns == 1 else q
            gc = at_rows('gc', rs)
            lwc = at_rows('lw', rs)
            gmid = gc[c // 2:c // 2 + 1, :]
            glast = gc[c - 1:c, :]
            e_last[j] = jnp.exp(glast)
            r = at_rows('r', rs)
            k = at_rows('k', rs)
            v = at_rows('v', rs)
            alpha = -at_rows('kkn', rs)
            bet = at_rows('kkn', rs) * at_rows('a', rs)
            e_inv = jnp.exp(gmid - gc)
            e_tail = jnp.exp(glast - gc)
            lhs_a = alpha * jnp.exp(gc - lwc - gmid)
            lhs_r = r * jnp.exp(gc - gmid)
            rhs_b = bet * e_inv
            rhs_k = k * e_inv
            abs_a = alpha * jnp.exp(gc - lwc)
            abs_r = r * jnp.exp(gc)
            tail_b = bet * e_tail
            tail_k = k * e_tail
            for h in heads:
                sl = sls[h]
                lhs = jnp.concatenate([lhs_a[:, sl], lhs_r[:, sl]], axis=0)
                rhs = jnp.concatenate([rhs_b[:, sl], rhs_k[:, sl]], axis=0)
                key = (j, h)
                amat[key] = jnp.where(mask2, _dot_nt(lhs, rhs), 0.0)
                rhs_uw[key] = abs_a[:, sl]
                r_abs[key] = abs_r[:, sl]
                tails[key] = jnp.concatenate([tail_b[:, sl], tail_k[:, sl]], axis=0)
                vs[key] = v[:, sl]

        from_v = {i: _dot(amat[i][:, c:], vs[i]) for i in inst}
        uw = dict(zip(inst, _unit_lower_solves(
            [amat[i][:c, :c] for i in inst],
            [jnp.concatenate([from_v[i][:c], rhs_uw[i]], axis=1) for i in inst])))
        ro = {i: _dot(amat[i][c:, :c], uw[i]) for i in inst}
        mp = {i: _dot_tn(uw[i][:, B_N:], tails[i][:c]) for i in inst}
        bb = {i: _dot_tn(jnp.concatenate([uw[i][:, :B_N], vs[i]], axis=0), tails[i]) for i in inst}
        state = {}
        outs = []
        for j in members:
            per_head = []
            for h in heads:
                i = (j, h)
                skey = (0 if ns == 1 else j, h)
                st = state[skey] if skey in state else s_scr[seq_of[j], h]
                per_head.append(_dot_nt(r_abs[i] + ro[i][:, B_N:], st) + from_v[i][c:] + ro[i][:, :B_N])
                state[skey] = st * e_last[j][:, sls[h]] + _dot(st, mp[i]) + bb[i]
            outs.append(jnp.concatenate(per_head, axis=1))
        for j, o in zip(members, outs):
            d = o - head_sums(o) * (1.0 / B_N)
            var = head_sums(d * d) * (1.0 / B_N)
            o = d * lax.rsqrt(var + B_LN_EPS) * lnw_ref[...] + lnb_ref[...]
            y_ref[rows_of[j], :] = (o + at_rows('bonus', rows_of[j])) * at_rows('g', rows_of[j])
        for (sk, h), st in state.items():
            s_scr[seq_of[sk], h] = st
        return carry

    assert (ns * nc) % group == 0 and (ns == 1 or nc == 1)
    lax.fori_loop(0, ns * nc // group, advance, 0)

    _emit_state(last, sout_ref, prev_s_ref, s_scr)


def _rwkv(zb, shift0, s0, prm, *, bsz, t_pad, t_valid, c, nc, ns, group, layer, state_layer, prev):
    grid, rows, valid_rows = _mixer_grid(bsz, t_pad, t_valid, nc * c, ns)
    nt = grid[1]
    kern = functools.partial(_rwkv_kernel, c=c, nc=nc, ns=ns, nt=nt, valid_rows=valid_rows,
                             group=group)
    spec = functools.partial(_param_spec, layer=layer)
    row = lambda wd: spec((1, wd))
    state_shapes = ((B_HEADS, B_N, B_N), (1, B_IN))
    kern, prev_args, prev_specs, n_prev = _with_prev(kern, prev, state_shapes, ns)
    return pl.pallas_call(
        kern,
        grid=grid,
        in_specs=prev_specs + [
                  pl.BlockSpec((rows, ZB_W), lambda b, t: (b * nt + t, 0)),
                  _state_spec((ns, 1, B_IN), state_layer),
                  _state_spec((ns, B_HEADS, B_N, B_N), state_layer),
                  row(B_IN), row(B_W), spec((B_LORA_WA, B_W)), row(B_W),
                  spec((B_LORA_WA, B_W)), spec((B_GATE_LORA, B_W)),
                  row(B_W), row(B_W), row(B_W), row(B_W), row(B_W), _param_spec((B_W, B_W), 0)],
        out_specs=[pl.BlockSpec((rows, B_W), lambda b, t: (b * nt + t, 0)),
                   ] + [_stacked_spec(n_prev + 1, (ns,) + shp) for shp in state_shapes],
        out_shape=[jax.ShapeDtypeStruct((bsz * t_pad, B_W), F32)]
        + [jax.ShapeDtypeStruct((n_prev + 1, bsz) + shp, F32) for shp in state_shapes],
        scratch_shapes=[pltpu.VMEM((ns, CONV_PAD + nc * c, B_IN), F32),
                        pltpu.VMEM((ns, B_HEADS, B_N, B_N), F32),
                        pltpu.VMEM((9, rows, B_W), F32)],
        compiler_params=pltpu.CompilerParams(dimension_semantics=("arbitrary", "arbitrary"),
                                             vmem_limit_bytes=VMEM_LIMIT),
        name="rwkv",
    )(*prev_args, zb, shift0, s0, *prm)


def _rglru_block(zc, y_ref, refs, scratch, first, last, *, l, ns, valid_rows):
    prev_h_ref, prev_conv_ref, conv0_ref, h0_ref, convw_ref, convb_ref, wg_ref, bg_ref, l_ref, \
        hout_ref, convout_ref = refs
    xpad, h_scr = scratch
    rows = ns * l

    @pl.when(first)
    def _():
        h_scr[...] = h0_ref[...]

    xc = _causal_conv(zc[:, 0:C_WIDTH], xpad, convw_ref, first, conv0_ref, ns, l) + convb_ref[...]

    def hand_over():
        _carry_history(xpad, l, valid_rows, last, convout_ref, prev_conv_ref, CONV_TAPS - 1)
        _emit_state(last, hout_ref, prev_h_ref, h_scr)

    yield None
    gates = _sigmoid(_dot(xc, wg_ref[...]) + bg_ref[...])
    yield None
    r = gates[:, :C_WIDTH]
    i = gates[:, C_WIDTH:]
    log_a = -C_POW * r * _softplus(-l_ref[...])
    a = jnp.exp(log_a)
    b = jnp.sqrt(-jnp.tanh(log_a) * (a * a + 1.0)) * (i * xc)
    yield None
    if valid_rows < l:
        valid = _rows_in((rows, C_WIDTH), l) < valid_rows
        a = jnp.where(valid, a, 1.0)
        b = jnp.where(valid, b, 0.0)
    s = 1
    while s < SUBLANE:
        b = a * _shift_rows(b, s, 0.0, SUBLANE) + b
        a = a * _shift_rows(a, s, 1.0, SUBLANE)
        s *= 2
        yield None
    gate = _gelu_tanh(zc[:, C_WIDTH:])
    yield None
    tiles_per_stage = max(1, rows // SUBLANE // 4)
    done = 0
    for q in range(ns):
        carry = h_scr[q]
        for r0 in range(q * l, (q + 1) * l, SUBLANE):
            rs = slice(r0, r0 + SUBLANE)
            hseq = a[rs] * carry + b[rs]
            carry = hseq[SUBLANE - 1:SUBLANE, :]
            y_ref[rs, :] = hseq * gate[rs]
            done += 1
            if done % tiles_per_stage == 0:
                yield None
        h_scr[q] = carry
    yield hand_over


FF_CHUNK = 1024


def _merge_ffn_kernel(x_ref, ya_ref, yb_ref, yc_ref, zg_ref, wbr_ref, wout_ref, g2_ref,
                      wup_ref, wdown_ref, gf_ref, o_ref, *, final):
    merged = None
    for n, y_ref in enumerate((ya_ref, yb_ref, yc_ref)):
        proj = _dot(y_ref[...], wbr_ref[n])
        term = _sigmoid(zg_ref[:, n * D_MODEL:(n + 1) * D_MODEL]) * proj
        merged = term if merged is None else merged + term
    x = x_ref[...] + _dot(merged, wout_ref[...])
    h = x * lax.rsqrt(jnp.mean(x * x, axis=-1, keepdims=True) + EPS) * g2_ref[...]
    h = h.astype(BF16)
    acc = x
    for j in range(0, D_FF, FF_CHUNK):
        up = jnp.dot(h, wup_ref[:, j:j + FF_CHUNK], preferred_element_type=F32)
        acc = acc + _dot(jnp.square(jnp.maximum(up, 0.0)), wdown_ref[j:j + FF_CHUNK, :])
    if final:
        acc = acc * lax.rsqrt(jnp.mean(acc * acc, axis=-1, keepdims=True) + EPS) * gf_ref[...]
    o_ref[...] = acc


def _merge_ffn(x, ya, yb, yc, zg, wbr, wout, g2, wup, wdown, gf, tm, final, layer):
    m = x.shape[0]
    spec = functools.partial(_param_spec, layer=layer)
    tile = lambda wd: pl.BlockSpec((tm, wd), lambda i: (i, 0))
    return pl.pallas_call(
        functools.partial(_merge_ffn_kernel, final=final),
        grid=(m // tm,),
        in_specs=[tile(D_MODEL), tile(A_V_W), tile(B_W), tile(C_WIDTH), tile(ZG_W),
                  spec((N_BRANCH, A_V_W, D_MODEL)), spec((D_MODEL, D_MODEL)),
                  spec((1, D_MODEL)), spec((D_MODEL, D_FF)),
                  spec((D_FF, D_MODEL)), _param_spec((1, D_MODEL), 0)],
        out_specs=tile(D_MODEL),
        out_shape=jax.ShapeDtypeStruct((m, D_MODEL), F32),
        compiler_params=pltpu.CompilerParams(dimension_semantics=("arbitrary",),
                                             vmem_limit_bytes=VMEM_LIMIT),
        name="merge_ffn",
    )(x, ya, yb, yc, zg, wbr, wout, g2, wup, wdown, gf)


def _block_diag(w):
    g, n, _ = w.shape
    eye = jnp.eye(g, dtype=w.dtype)
    return (eye[:, None, :, None] * w[:, :, None, :]).reshape(g * n, g * n)


def _prepare_params(p):
    depth = p['w_in'].shape[0]
    row = lambda a: a.reshape(depth, 1, -1).astype(F32)
    w_in = p['w_in'].astype(BF16)
    lane_row = lambda a: jnp.zeros((depth, 1, LANE), F32).at[:, 0, A_HEADS:2 * A_HEADS].set(a)
    zeros_lora = jnp.zeros((depth, B_LORA_WA // 2, B_W), F32)
    block_diag = jax.vmap(_block_diag)
    gdn = (p['a_conv_w'], lane_row(p['a_A_log']), lane_row(p['a_dt_bias']), row(p['a_norm_g']))
    rwkv = (row(p['b_mu']), row(p['b_w0']),
            jnp.concatenate([p['b_w_up'], zeros_lora], axis=1).astype(BF16),
            row(p['b_a0']),
            jnp.concatenate([zeros_lora, p['b_a_up']], axis=1).astype(BF16),
            p['b_g_up'].astype(BF16),
            row(p['b_k_k']), row(p['b_k_a']), row(p['b_r_k']), row(p['b_ln_w']), row(p['b_ln_b']),
            jnp.kron(jnp.eye(B_HEADS, dtype=F32), jnp.ones((B_N, B_N), F32)).astype(BF16)[None])
    rglru = (p['c_conv_w'], row(p['c_conv_b']),
             jnp.concatenate([block_diag(p['c_wa']), block_diag(p['c_wx'])], axis=2).astype(BF16),
             jnp.concatenate([row(p['c_ba']), row(p['c_bx'])], axis=2), row(p['c_L']))
    ffn = (p['w_branch'].astype(BF16), p['w_out'].astype(BF16), row(p['norm2_g']),
           p['w_up'].astype(BF16), p['w_down'].astype(BF16))
    return dict(norm1_g=row(p['norm1_g']), w_in=w_in, gdn=gdn, rwkv=rwkv, rglru=rglru, ffn=ffn)


PROMPT_BLOCKING = dict(c=64, nc=8, ns=1, gdn_group=4, rwkv_group=4)
SAMPLE_BLOCKING = dict(c=SUBLANE, nc=1, ns=8, gdn_group=8, rwkv_group=4)
TOKEN_TILE = 256


def _layer(x, states, new_states, lp, gf, *, layer, state_layer, bsz, t_pad, t_valid, blocking, final):
    a_s, a_conv, b_s, b_shift, c_h, c_conv = states
    prev = (lambda i, j: None) if new_states is None else (lambda i, j: (new_states[i], new_states[j]))
    kw = dict(bsz=bsz, t_pad=t_pad, t_valid=t_valid, layer=layer, state_layer=state_layer)
    chunked = dict(c=blocking['c'], nc=blocking['nc'], ns=blocking['ns'], **kw)
    za, zb, yc, zg, c_h, c_conv = _in_proj(x, lp['norm1_g'], lp['w_in'], c_conv,
                                           c_h.reshape(-1, bsz, 1, C_WIDTH), lp['rglru'], tm=TOKEN_TILE,
                                           prev=prev(4, 5), **kw)
    ya, a_s, a_conv = _gdn(za, a_conv, a_s, *lp['gdn'], group=blocking['gdn_group'], prev=prev(0, 1),
                           **chunked)
    yb, b_s, b_shift = _rwkv(zb, b_shift, b_s, lp['rwkv'], group=blocking['rwkv_group'], prev=prev(2, 3),
                             **chunked)
    x = _merge_ffn(x, ya, yb, yc, zg, *lp['ffn'], gf, TOKEN_TILE, final, layer)
    return x, (a_s, a_conv, b_s, b_shift, c_h, c_conv)


def kernel(x_prompt, x_sample, state_a_S, state_a_conv, state_b_S, state_b_shift, state_c_h, state_c_conv, norm1_g, w_in, a_conv_w, a_A_log, a_dt_bias, a_norm_g, b_mu, b_w0, b_w_up, b_a0, b_a_up, b_g_up, b_k_k, b_k_a, b_r_k, b_ln_w, b_ln_b, c_conv_w, c_conv_b, c_wa, c_ba, c_wx, c_bx, c_L, w_branch, w_out, norm2_g, w_up, w_down, final_norm_g):
    p = dict(norm1_g=norm1_g, w_in=w_in, a_conv_w=a_conv_w, a_A_log=a_A_log, a_dt_bias=a_dt_bias,
             a_norm_g=a_norm_g, b_mu=b_mu, b_w0=b_w0, b_w_up=b_w_up, b_a0=b_a0, b_a_up=b_a_up,
             b_g_up=b_g_up, b_k_k=b_k_k, b_k_a=b_k_a, b_r_k=b_r_k, b_ln_w=b_ln_w, b_ln_b=b_ln_b,
             c_conv_w=c_conv_w, c_conv_b=c_conv_b, c_wa=c_wa, c_ba=c_ba, c_wx=c_wx, c_bx=c_bx,
             c_L=c_L, w_branch=w_branch, w_out=w_out, norm2_g=norm2_g, w_up=w_up, w_down=w_down)
    bp, tp, _ = x_prompt.shape
    bs, ts, _ = x_sample.shape
    ts_pad = -(-ts // SUBLANE) * SUBLANE
    assert ts_pad == SAMPLE_BLOCKING['c'] and tp % (PROMPT_BLOCKING['c'] * PROMPT_BLOCKING['nc']) == 0
    gf = final_norm_g.reshape(1, 1, D_MODEL)

    xp = x_prompt.reshape(bp * tp, D_MODEL)
    xs = jnp.pad(x_sample, ((0, 0), (0, ts_pad - ts), (0, 0))).reshape(bs * ts_pad, D_MODEL)
    zero = lambda *shape: jnp.zeros((1,) + shape, F32)
    p_states = (zero(bp, A_HEADS, A_DK, A_DV), zero(bp, 3, A_CONV_CH),
                zero(bp, B_HEADS, B_N, B_N), zero(bp, 1, B_IN),
                zero(bp, C_WIDTH), zero(bp, 3, C_WIDTH))
    s_states = (state_a_S, state_a_conv, state_b_S, state_b_shift, state_c_h, state_c_conv)
    lp = _prepare_params(p)
    p_new = s_new = None
    for l in range(DEPTH):
        final = l == DEPTH - 1
        xp, p_new = _layer(xp, p_states, p_new, lp, gf, layer=l, state_layer=0, bsz=bp, t_pad=tp,
                           t_valid=tp, blocking=PROMPT_BLOCKING, final=final)
        xs, s_new = _layer(xs, s_states, s_new, lp, gf, layer=l, state_layer=l, bsz=bs, t_pad=ts_pad,
                           t_valid=ts, blocking=SAMPLE_BLOCKING, final=final)
    squeeze_h = lambda st: st[:4] + (st[4].reshape(DEPTH, -1, C_WIDTH), st[5])
    y_prompt = xp.reshape(bp, tp, D_MODEL)
    y_sample = xs.reshape(bs, ts_pad, D_MODEL)[:, :ts]
    return (y_prompt, y_sample) + squeeze_h(p_new) + squeeze_h(s_new)
```

```python
import functools

import jax
import jax.numpy as jnp
from jax import lax
from jax.experimental import pallas as pl
from jax.experimental.pallas import tpu as pltpu

F32 = jnp.float32
BF16 = jnp.bfloat16
HIGHEST = lax.Precision.HIGHEST

EPS = 1e-6
D_MODEL = 1024
DEPTH = 2
A_HEADS = 4
A_DK = 128
A_DV = 128
A_QK_W = A_HEADS * A_DK
A_V_W = A_HEADS * A_DV
A_CONV_CH = 2 * A_QK_W + A_V_W
B_N = 64
B_HEADS = 8
B_W = B_HEADS * B_N
B_LORA_WA = 128
B_GATE_LORA = 128
B_IN = 3 * B_W + B_LORA_WA + B_GATE_LORA
B_LN_EPS = 64e-5
DECAY_SCALE = 0.6065306597126334
C_WIDTH = 512
C_BLOCKS = 8
C_POW = 8.0
N_BRANCH = 3
D_FF = 4 * D_MODEL

LANE = 128
SUBLANE = 8
CONV_TAPS = 4
CONV_PAD = SUBLANE

ZA_W = A_CONV_CH + A_V_W + LANE
ZA_BA = A_CONV_CH + A_V_W
ZB_W = B_IN
ZC_W = 2 * C_WIDTH
ZG_W = N_BRANCH * D_MODEL
Z_WIDTHS = (ZA_W, ZB_W, ZC_W, ZG_W)
A_IN = A_CONV_CH + A_V_W + 2 * A_HEADS
N_IN = A_IN + ZB_W + ZC_W + ZG_W

VMEM_LIMIT = 56 * 1024 * 1024
PROJ_PIECE = 512


def _dot(a, b):
    return jnp.dot(a.astype(BF16), b.astype(BF16), preferred_element_type=F32)


def _dot_nt(a, b):
    return lax.dot_general(a.astype(BF16), b.astype(BF16), (((1,), (1,)), ((), ())),
                           preferred_element_type=F32)


def _dot_tn(a, b):
    return lax.dot_general(a.astype(BF16), b.astype(BF16), (((0,), (0,)), ((), ())),
                           preferred_element_type=F32)


def _sigmoid(x):
    return 1.0 / (1.0 + jnp.exp(-x))


def _silu(x):
    return x * _sigmoid(x)


def _softplus(x):
    return jnp.maximum(x, 0.0) + jnp.log1p(jnp.exp(-jnp.abs(x)))


def _gelu_tanh(x):
    return 0.5 * x * (1.0 + jnp.tanh(0.7978845608028654 * (x + 0.044715 * (x * x * x))))


def _rows(shape):
    return lax.broadcasted_iota(jnp.int32, shape, 0)


def _cols(shape):
    return lax.broadcasted_iota(jnp.int32, shape, 1)


def _rows_in(shape, period):
    r = _rows(shape)
    return r if period >= shape[0] else r & (period - 1)


def _shift_rows(x, s, fill, period):
    return jnp.where(_rows_in(x.shape, period) >= s, pltpu.roll(x, s, 0), fill)


def _bf16_pieces(x, n):
    pieces = []
    for _ in range(n):
        p = x.astype(BF16)
        pieces.append(p)
        x = x - p.astype(F32)
    return pieces


def _cumsum_chunks(x, c):
    tri = (_rows((c, c)) >= _cols((c, c))).astype(BF16)
    pieces = _bf16_pieces(x, 3)
    chunks = [sum(jnp.dot(tri, p[r0:r0 + c], preferred_element_type=F32) for p in pieces)
              for r0 in range(0, x.shape[0], c)]
    return chunks[0] if len(chunks) == 1 else jnp.concatenate(chunks, axis=0)


def _unit_lower_inverses(ns):
    c = ns[0].shape[0]
    eye = (_rows(ns[0].shape) == _cols(ns[0].shape)).astype(F32)
    ts = [eye + n for n in ns]
    ps = list(ns)
    s = 2
    while s < c:
        ps = [_dot(p, p) for p in ps]
        ts = [t + _dot(t, p) for t, p in zip(ts, ps)]
        s *= 2
    return ts


def _unit_lower_solves(ns, xs):
    c = ns[0].shape[0]
    xs = [x + _dot(n, x) for n, x in zip(ns, xs)]
    ps = list(ns)
    s = 2
    while s < c:
        ps = [_dot(p, p) for p in ps]
        xs = [x + _dot(p, x) for p, x in zip(ps, xs)]
        s *= 2
    return xs


def _param_spec(shape, layer):
    nd = len(shape)
    return pl.BlockSpec((None,) + shape, lambda *_: (layer,) + (0,) * nd, pipeline_mode=pl.Buffered(1))


def _group_of_2d(b, t):
    return b


def _stacked_spec(layers, shape, group_of=_group_of_2d):
    nd = len(shape) - 1
    return pl.BlockSpec((layers,) + shape, lambda *g: (0, group_of(*g)) + (0,) * nd)


def _emit_state(last, out_ref, prev_ref, new_ref):
    @pl.when(last)
    def _():
        n_prev = out_ref.shape[0] - 1
        if n_prev:
            out_ref[0:n_prev] = prev_ref[...]
        out_ref[n_prev] = new_ref[...]


def _state_spec(shape, layer, group_of=_group_of_2d):
    nd = len(shape) - 1
    return pl.BlockSpec((None,) + shape, lambda *g: (layer, group_of(*g)) + (0,) * nd)


def _in_proj_kernel(prev_h_ref, prev_conv_ref, x_ref, g_ref, w_ref, conv0_ref, h0_ref, convw_ref, convb_ref,
                    wg_ref, bg_ref, l_ref, za_ref, zb_ref, yc_ref, zg_ref, hout_ref, convout_ref,
                    w_rest, xpad, h_scr, *, l, ns, nt, valid_rows):
    step = pl.program_id(0)
    t = step % nt
    @pl.when(step == 0)
    def _():
        w_rest[...] = w_ref[:, A_IN:]

    x = x_ref[...]
    h = x * lax.rsqrt(jnp.mean(x * x, axis=-1, keepdims=True) + EPS) * g_ref[...]
    h = h.astype(BF16)
    proj = lambda ref, off, width: jnp.dot(h, ref[:, off:off + width], preferred_element_type=F32)

    def project(out_ref, w, off, width):
        def piece(lo):
            hi = min(lo + PROJ_PIECE, width)
            out_ref[:, lo:hi] = proj(w, off + lo, hi - lo)
        return [functools.partial(piece, lo) for lo in range(0, width, PROJ_PIECE)]

    zc = proj(w_rest, ZB_W, ZC_W)
    pieces = (project(za_ref, w_ref, 0, ZA_W) + project(zb_ref, w_rest, 0, ZB_W)
              + project(zg_ref, w_rest, ZB_W + ZC_W, ZG_W))
    stages = _rglru_block(
        zc, yc_ref,
        (prev_h_ref, prev_conv_ref, conv0_ref, h0_ref, convw_ref, convb_ref, wg_ref, bg_ref, l_ref,
         hout_ref, convout_ref),
        (xpad, h_scr), t == 0, t == nt - 1, l=l, ns=ns, valid_rows=valid_rows)
    hand_over = None
    for hand_over in stages:
        if pieces:
            pieces.pop(0)()
    for piece in pieces:
        piece()
    hand_over()


def _in_proj(x, g, w, conv0, h0, prm, *, tm, bsz, t_pad, t_valid, layer, state_layer, prev):
    m = x.shape[0]
    l = min(tm, t_pad)
    ns = tm // l
    _, _, valid_rows = _mixer_grid(bsz, t_pad, t_valid, l, ns)
    nt = t_pad // l
    group_of = lambda i: i // nt
    spec = functools.partial(_param_spec, layer=layer)
    kern = functools.partial(_in_proj_kernel, l=l, ns=ns, nt=nt, valid_rows=valid_rows)
    state_shapes = ((1, C_WIDTH), (3, C_WIDTH))
    kern, prev_args, prev_specs, n_prev = _with_prev(kern, prev, state_shapes, ns, group_of)
    tile = lambda wd: pl.BlockSpec((tm, wd), lambda i: (i, 0))
    widths = (ZA_W, ZB_W, C_WIDTH, ZG_W)
    return pl.pallas_call(
        kern,
        grid=(m // tm,),
        in_specs=prev_specs + [
                  tile(D_MODEL), spec((1, D_MODEL)), spec((D_MODEL, N_IN)),
                  _state_spec((ns, 3, C_WIDTH), state_layer, group_of),
                  _state_spec((ns, 1, C_WIDTH), state_layer, group_of),
                  spec((CONV_TAPS, C_WIDTH)), spec((1, C_WIDTH)),
                  spec((C_WIDTH, 2 * C_WIDTH)), spec((1, 2 * C_WIDTH)),
                  spec((1, C_WIDTH))],
        out_specs=[tile(wd) for wd in widths]
        + [_stacked_spec(n_prev + 1, (ns,) + shp, group_of) for shp in state_shapes],
        out_shape=[jax.ShapeDtypeStruct((m, wd), F32) for wd in widths]
        + [jax.ShapeDtypeStruct((n_prev + 1, bsz) + shp, F32) for shp in state_shapes],
        scratch_shapes=[pltpu.VMEM((D_MODEL, N_IN - A_IN), BF16),
                        pltpu.VMEM((ns, CONV_PAD + l, C_WIDTH), F32),
                        pltpu.VMEM((ns, 1, C_WIDTH), F32)],
        compiler_params=pltpu.CompilerParams(dimension_semantics=("arbitrary",),
                                             vmem_limit_bytes=VMEM_LIMIT),
        name="in_proj",
    )(*prev_args, x, g, w, conv0, h0, *prm)


def _with_history(x, xpad, first, hist_ref, ns, l, depth):
    @pl.when(first)
    def _():
        xpad[:, CONV_PAD - depth:CONV_PAD, :] = hist_ref[...]

    for s in range(ns):
        xpad[s, CONV_PAD:CONV_PAD + l, :] = x[s * l:(s + 1) * l]
    views = []
    for j in range(depth, 0, -1):
        parts = [xpad[s, CONV_PAD - j:CONV_PAD - j + l, :] for s in range(ns)]
        views.append(parts[0] if ns == 1 else jnp.concatenate(parts, axis=0))
    return views


def _carry_history(xpad, l, valid_rows, last, out_ref, prev_ref, depth):
    _emit_state(last, out_ref, prev_ref,
                xpad.at[:, CONV_PAD - depth + valid_rows:CONV_PAD + valid_rows, :])

    nxt = xpad[:, CONV_PAD - depth + l:CONV_PAD + l, :]
    xpad[:, CONV_PAD - depth:CONV_PAD, :] = nxt


def _causal_conv(x, xpad, w_ref, first, hist_ref, ns, l):
    y = x * w_ref[CONV_TAPS - 1:CONV_TAPS, :]
    for j, view in enumerate(_with_history(x, xpad, first, hist_ref, ns, l, CONV_TAPS - 1)):
        y = y + view * w_ref[j:j + 1, :]
    return y


def _gdn_kernel(prev_s_ref, prev_conv_ref, za_ref, conv0_ref, s0_ref, convw_ref, alog_ref, dt_ref, ng_ref,
                y_ref, sout_ref, convout_ref, xpad, s_scr, *, c, nc, ns, nt, valid_rows, group):
    t = pl.program_id(1)
    first = t == 0
    last = t == nt - 1
    l = nc * c
    rows = ns * l

    @pl.when(first)
    def _():
        s_scr[...] = s0_ref[...]

    qkv = _silu(_causal_conv(za_ref[:, 0:A_CONV_CH], xpad, convw_ref, first, conv0_ref, ns, l))
    _carry_history(xpad, l, valid_rows, last, convout_ref, prev_conv_ref, CONV_TAPS - 1)

    ba = za_ref[:, ZA_BA:ZA_BA + LANE]
    ba = jnp.where(_cols(ba.shape) < 2 * A_HEADS, ba, 0.0)
    beta_all = _sigmoid(ba)
    g_all = -jnp.exp(alog_ref[...]) * _softplus(ba + dt_ref[...])
    masked = valid_rows < l
    if masked:
        valid = _rows_in((rows, LANE), l) < valid_rows
        beta_all = jnp.where(valid, beta_all, 0.0)
        g_all = jnp.where(valid, g_all, 0.0)
    gc_all = _cumsum_chunks(g_all, c)

    rr = _rows((c, c))
    cc = _cols((c, c))
    causal = rr >= cc
    strict = rr > cc
    ng = ng_ref[...]
    pick = (_cols((SUBLANE * A_HEADS, LANE))
            == A_HEADS + _rows((SUBLANE * A_HEADS, LANE)) // SUBLANE).astype(F32)

    all_chunks = [(s, n) for n in range(nc) for s in range(ns)]
    heads = range(A_HEADS)
    state = {(s, h): s_scr[s, h] for s in range(ns) for h in heads}

    def advance(chunks):
        inst = [(s, n, h) for (s, n) in chunks for h in heads]
        lower, attn, rhs2, kdec, qg, glast = {}, {}, {}, {}, {}, {}
        for (s, n) in chunks:
            r0 = s * l + n * c
            rs = slice(r0, r0 + c)
            gc_rows = lax.dot_general(pick, gc_all[rs], (((1,), (1,)), ((), ())), precision=HIGHEST,
                                      preferred_element_type=F32)
            for h in heads:
                q = qkv[rs, h * A_DK:(h + 1) * A_DK]
                k = qkv[rs, A_QK_W + h * A_DK:A_QK_W + (h + 1) * A_DK]
                v = qkv[rs, 2 * A_QK_W + h * A_DV:2 * A_QK_W + (h + 1) * A_DV]
                q = q * lax.rsqrt(jnp.sum(q * q, axis=-1, keepdims=True) + EPS) * (A_DK ** -0.5)
                k = k * lax.rsqrt(jnp.sum(k * k, axis=-1, keepdims=True) + EPS)
                if masked:
                    valid_c = _rows((c, A_DK)) < valid_rows - n * c
                    k = jnp.where(valid_c, k, 0.0)
                    v = jnp.where(valid_c, v, 0.0)
                beta = beta_all[rs, h:h + 1]
                gc = gc_all[rs, A_HEADS + h:A_HEADS + h + 1]
                gc_row = gc_rows[SUBLANE * h:SUBLANE * h + 1, :]
                dec = jnp.where(causal, jnp.exp(jnp.where(causal, gc - gc_row, 0.0)), 0.0)
                kb = k * beta
                egc = jnp.exp(gc)
                sc = _dot_nt(jnp.concatenate([kb, q], axis=0), k)
                key = (s, n, h)
                lower[key] = jnp.where(strict, -sc[:c] * dec, 0.0)
                attn[key] = sc[c:] * dec
                rhs2[key] = jnp.concatenate([v * beta, kb * egc], axis=1)
                gc_last = gc[c - 1:c, :]
                kdec[key] = k * jnp.exp(gc_last - gc)
                glast[key] = jnp.exp(gc_last)
                qg[key] = q * egc
        tinv = dict(zip(inst, _unit_lower_inverses([lower[i] for i in inst])))
        uw = {i: _dot(tinv[i], rhs2[i]) for i in inst}
        ao = {i: _dot(attn[i], uw[i]) for i in inst}
        bm = {i: _dot_tn(kdec[i], uw[i]) for i in inst}
        for (s, n) in chunks:
            r0 = s * l + n * c
            for h in heads:
                i = (s, n, h)
                st = state[(s, h)]
                o = _dot(qg[i] - ao[i][:, A_DV:], st) + ao[i][:, :A_DV]
                state[(s, h)] = st * glast[i] - _dot(bm[i][:, A_DV:], st) + bm[i][:, :A_DV]
                o = o * lax.rsqrt(jnp.mean(o * o, axis=-1, keepdims=True) + EPS)
                z = za_ref[r0:r0 + c, A_CONV_CH + h * A_DV:A_CONV_CH + (h + 1) * A_DV]
                y_ref[r0:r0 + c, h * A_DV:(h + 1) * A_DV] = o * ng * _silu(z)

    for g in range(0, len(all_chunks), group):
        advance(all_chunks[g:g + group])
    for s in range(ns):
        for h in heads:
            s_scr[s, h] = state[(s, h)]

    _emit_state(last, sout_ref, prev_s_ref, s_scr)


def _with_prev(kern, prev, shapes, ns, group_of=_group_of_2d):
    if prev is None:
        return functools.partial(kern, None, None), [], [], 0
    n_prev = prev[0].shape[0]
    return kern, list(prev), [_stacked_spec(n_prev, (ns,) + shp, group_of) for shp in shapes], n_prev


def _mixer_grid(bsz, t_pad, t_valid, l, ns):
    assert t_pad % l == 0 and bsz % ns == 0 and (ns == 1 or t_pad == l)
    nt = t_pad // l
    valid_rows = t_valid - (nt - 1) * l
    assert 0 < valid_rows <= l and (nt == 1 or valid_rows == l)
    return (bsz // ns, nt), ns * l, valid_rows


def _gdn(za, conv0, s0, convw, alog_row, dt_row, ng, *, bsz, t_pad, t_valid, c, nc, ns, group,
         layer, state_layer, prev):
    grid, rows, valid_rows = _mixer_grid(bsz, t_pad, t_valid, nc * c, ns)
    spec = functools.partial(_param_spec, layer=layer)
    nt = grid[1]
    kern = functools.partial(_gdn_kernel, c=c, nc=nc, ns=ns, nt=nt, valid_rows=valid_rows,
                             group=group)
    state_shapes = ((A_HEADS, A_DK, A_DV), (3, A_CONV_CH))
    kern, prev_args, prev_specs, n_prev = _with_prev(kern, prev, state_shapes, ns)
    return pl.pallas_call(
        kern,
        grid=grid,
        in_specs=prev_specs + [
                  pl.BlockSpec((rows, ZA_W), lambda b, t: (b * nt + t, 0)),
                  _state_spec((ns, 3, A_CONV_CH), state_layer),
                  _state_spec((ns, A_HEADS, A_DK, A_DV), state_layer),
                  spec((CONV_TAPS, A_CONV_CH)),
                  spec((1, LANE)),
                  spec((1, LANE)),
                  spec((1, A_DV))],
        out_specs=[pl.BlockSpec((rows, A_V_W), lambda b, t: (b * nt + t, 0)),
                   ] + [_stacked_spec(n_prev + 1, (ns,) + shp) for shp in state_shapes],
        out_shape=[jax.ShapeDtypeStruct((bsz * t_pad, A_V_W), F32)]
        + [jax.ShapeDtypeStruct((n_prev + 1, bsz) + shp, F32) for shp in state_shapes],
        scratch_shapes=[pltpu.VMEM((ns, CONV_PAD + nc * c, A_CONV_CH), F32),
                        pltpu.VMEM((ns, A_HEADS, A_DK, A_DV), F32)],
        compiler_params=pltpu.CompilerParams(dimension_semantics=("arbitrary", "arbitrary"),
                                             vmem_limit_bytes=VMEM_LIMIT),
        name="gdn",
    )(*prev_args, za, conv0, s0, convw, alog_row, dt_row, ng)


def _rwkv_kernel(prev_s_ref, prev_shift_ref, zb_ref, shift0_ref, s0_ref, mu_ref, w0_ref, wup_ref, a0_ref,
                 aup_ref, gup_ref, kk_ref, ka_ref, rk_ref, lnw_ref, lnb_ref, hsum_ref,
                 y_ref, sout_ref, shiftout_ref, xpad, s_scr, prep, *, c, nc, ns, nt, valid_rows, group):
    t = pl.program_id(1)
    first = t == 0
    last = t == nt - 1
    l = nc * c
    rows = ns * l

    @pl.when(first)
    def _():
        s_scr[...] = s0_ref[...]

    zb = zb_ref[...]
    (prev,) = _with_history(zb, xpad, first, shift0_ref, ns, l, 1)
    _carry_history(xpad, l, valid_rows, last, shiftout_ref, prev_shift_ref, 1)
    zs = zb + (prev - zb) * mu_ref[...]

    r_all = zs[:, 0:B_W]
    k_all = zs[:, B_W:2 * B_W]
    v_all = zs[:, 2 * B_W:3 * B_W]
    xwa = zs[:, 3 * B_W:3 * B_W + B_LORA_WA]
    xg = zs[:, 3 * B_W + B_LORA_WA:B_IN]
    lw = -DECAY_SCALE * _sigmoid(w0_ref[...] + _dot(jnp.tanh(xwa), wup_ref[...]))
    a_all = _sigmoid(a0_ref[...] + _dot(xwa, aup_ref[...]))
    g_all = _dot(_sigmoid(xg), gup_ref[...])
    kk_all = k_all * kk_ref[...]
    k_all = k_all * (1.0 + (a_all - 1.0) * ka_ref[...])
    masked = valid_rows < l
    if masked:
        valid = _rows_in((rows, B_W), l) < valid_rows
        lw = jnp.where(valid, lw, 0.0)
        kk_all = jnp.where(valid, kk_all, 0.0)
        k_all = jnp.where(valid, k_all, 0.0)
        v_all = jnp.where(valid, v_all, 0.0)
    gc_all = _cumsum_chunks(lw, c)

    def head_sums(x, pieces=1):
        return sum(jnp.dot(p, hsum_ref[...], preferred_element_type=F32) for p in _bf16_pieces(x, pieces))

    kkn_all = kk_all * lax.rsqrt(head_sums(kk_all * kk_all, pieces=2) + EPS)
    bonus_all = head_sums(r_all * k_all * rk_ref[...]) * v_all

    rr = _rows((2 * c, 2 * c))
    cc = _cols((2 * c, 2 * c))
    tt = jnp.where(rr >= c, rr - c, rr)
    ss = jnp.where(cc >= c, cc - c, cc)
    mask2 = jnp.logical_or(tt > ss, jnp.logical_and(rr >= c, tt == ss))

    prep_names = ('r', 'k', 'v', 'kkn', 'a', 'lw', 'gc', 'bonus', 'g')
    for idx, val in enumerate((r_all, k_all, v_all, kkn_all, a_all, lw, gc_all, bonus_all, g_all)):
        prep[idx] = val
    heads = range(B_HEADS)
    sls = [slice(h * B_N, (h + 1) * B_N) for h in heads]

    def at_rows(name, rs):
        return prep[prep_names.index(name), rs, :]

    def advance(g, carry):
        members = range(group)
        inst = [(j, h) for j in members for h in heads]
        amat, rhs_uw, r_abs, tails, e_last, vs, rows_of, seq_of = {}, {}, {}, {}, {}, {}, {}, {}
        for j in members:
            q = g * group + j
            rs = pl.ds(pl.multiple_of(q * c, c), c)
            rows_of[j] = rs
            seq_of[j] = 0 if ns == 1 else q
            gc = at_rows('gc', rs)
            lwc = at_rows('lw', rs)
            gmid = gc[c // 2:c // 2 + 1, :]
            glast = gc[c - 1:c, :]
            e_last[j] = jnp.exp(glast)
            r = at_rows('r', rs)
            k = at_rows('k', rs)
            v = at_rows('v', rs)
            alpha = -at_rows('kkn', rs)
            bet = at_rows('kkn', rs) * at_rows('a', rs)
            e_inv = jnp.exp(gmid - gc)
            e_tail = jnp.exp(glast - gc)
            lhs_a = alpha * jnp.exp(gc - lwc - gmid)
            lhs_r = r * jnp.exp(gc - gmid)
            rhs_b = bet * e_inv
            rhs_k = k * e_inv
            abs_a = alpha * jnp.exp(gc - lwc)
            abs_r = r * jnp.exp(gc)
            tail_b = bet * e_tail
            tail_k = k * e_tail
            for h in heads:
                sl = sls[h]
                lhs = jnp.concatenate([lhs_a[:, sl], lhs_r[:, sl]], axis=0)
                rhs = jnp.concatenate([rhs_b[:, sl], rhs_k[:, sl]], axis=0)
                key = (j, h)
                amat[key] = jnp.where(mask2, _dot_nt(lhs, rhs), 0.0)
                rhs_uw[key] = abs_a[:, sl]
                r_abs[key] = abs_r[:, sl]
                tails[key] = jnp.concatenate([tail_b[:, sl], tail_k[:, sl]], axis=0)
                vs[key] = v[:, sl]

        from_v = {i: _dot(amat[i][:, c:], vs[i]) for i in inst}
        uw = dict(zip(inst, _unit_lower_solves(
            [amat[i][:c, :c] for i in inst],
            [jnp.concatenate([from_v[i][:c], rhs_uw[i]], axis=1) for i in inst])))
        ro = {i: _dot(amat[i][c:, :c], uw[i]) for i in inst}
        mp = {i: _dot_tn(uw[i][:, B_N:], tails[i][:c]) for i in inst}
        bb = {i: _dot_tn(jnp.concatenate([uw[i][:, :B_N], vs[i]], axis=0), tails[i]) for i in inst}
        state = {}
        outs = []
        for j in members:
            per_head = []
            for h in heads:
                i = (j, h)
                skey = (0 if ns == 1 else j, h)
                st = state[skey] if skey in state else s_scr[seq_of[j], h]
                per_head.append(_dot_nt(r_abs[i] + ro[i][:, B_N:], st) + from_v[i][c:] + ro[i][:, :B_N])
                state[skey] = st * e_last[j][:, sls[h]] + _dot(st, mp[i]) + bb[i]
            outs.append(jnp.concatenate(per_head, axis=1))
        for j, o in zip(members, outs):
            d = o - head_sums(o) * (1.0 / B_N)
            var = head_sums(d * d) * (1.0 / B_N)
            o = d * lax.rsqrt(var + B_LN_EPS) * lnw_ref[...] + lnb_ref[...]
            y_ref[rows_of[j], :] = (o + at_rows('bonus', rows_of[j])) * at_rows('g', rows_of[j])
        for (sk, h), st in state.items():
            s_scr[seq_of[sk], h] = st
        return carry

    assert (ns * nc) % group == 0 and (ns == 1 or nc == 1)
    lax.fori_loop(0, ns * nc // group, advance, 0)

    _emit_state(last, sout_ref, prev_s_ref, s_scr)


def _rwkv(zb, shift0, s0, prm, *, bsz, t_pad, t_valid, c, nc, ns, group, layer, state_layer, prev):
    grid, rows, valid_rows = _mixer_grid(bsz, t_pad, t_valid, nc * c, ns)
    nt = grid[1]
    kern = functools.partial(_rwkv_kernel, c=c, nc=nc, ns=ns, nt=nt, valid_rows=valid_rows,
                             group=group)
    spec = functools.partial(_param_spec, layer=layer)
    row = lambda wd: spec((1, wd))
    state_shapes = ((B_HEADS, B_N, B_N), (1, B_IN))
    kern, prev_args, prev_specs, n_prev = _with_prev(kern, prev, state_shapes, ns)
    return pl.pallas_call(
        kern,
        grid=grid,
        in_specs=prev_specs + [
                  pl.BlockSpec((rows, ZB_W), lambda b, t: (b * nt + t, 0)),
                  _state_spec((ns, 1, B_IN), state_layer),
                  _state_spec((ns, B_HEADS, B_N, B_N), state_layer),
                  row(B_IN), row(B_W), spec((B_LORA_WA, B_W)), row(B_W),
                  spec((B_LORA_WA, B_W)), spec((B_GATE_LORA, B_W)),
                  row(B_W), row(B_W), row(B_W), row(B_W), row(B_W), _param_spec((B_W, B_W), 0)],
        out_specs=[pl.BlockSpec((rows, B_W), lambda b, t: (b * nt + t, 0)),
                   ] + [_stacked_spec(n_prev + 1, (ns,) + shp) for shp in state_shapes],
        out_shape=[jax.ShapeDtypeStruct((bsz * t_pad, B_W), F32)]
        + [jax.ShapeDtypeStruct((n_prev + 1, bsz) + shp, F32) for shp in state_shapes],
        scratch_shapes=[pltpu.VMEM((ns, CONV_PAD + nc * c, B_IN), F32),
                        pltpu.VMEM((ns, B_HEADS, B_N, B_N), F32),
                        pltpu.VMEM((9, rows, B_W), F32)],
        compiler_params=pltpu.CompilerParams(dimension_semantics=("arbitrary", "arbitrary"),
                                             vmem_limit_bytes=VMEM_LIMIT),
        name="rwkv",
    )(*prev_args, zb, shift0, s0, *prm)


def _rglru_block(zc, y_ref, refs, scratch, first, last, *, l, ns, valid_rows):
    prev_h_ref, prev_conv_ref, conv0_ref, h0_ref, convw_ref, convb_ref, wg_ref, bg_ref, l_ref, \
        hout_ref, convout_ref = refs
    xpad, h_scr = scratch
    rows = ns * l

    @pl.when(first)
    def _():
        h_scr[...] = h0_ref[...]

    xc = _causal_conv(zc[:, 0:C_WIDTH], xpad, convw_ref, first, conv0_ref, ns, l) + convb_ref[...]

    def hand_over():
        _carry_history(xpad, l, valid_rows, last, convout_ref, prev_conv_ref, CONV_TAPS - 1)
        _emit_state(last, hout_ref, prev_h_ref, h_scr)

    yield None
    gates = _sigmoid(_dot(xc, wg_ref[...]) + bg_ref[...])
    yield None
    r = gates[:, :C_WIDTH]
    i = gates[:, C_WIDTH:]
    log_a = -C_POW * r * _softplus(-l_ref[...])
    a = jnp.exp(log_a)
    b = jnp.sqrt(-jnp.tanh(log_a) * (a * a + 1.0)) * (i * xc)
    yield None
    if valid_rows < l:
        valid = _rows_in((rows, C_WIDTH), l) < valid_rows
        a = jnp.where(valid, a, 1.0)
        b = jnp.where(valid, b, 0.0)
    s = 1
    while s < SUBLANE:
        b = a * _shift_rows(b, s, 0.0, SUBLANE) + b
        a = a * _shift_rows(a, s, 1.0, SUBLANE)
        s *= 2
        yield None
    gate = _gelu_tanh(zc[:, C_WIDTH:])
    yield None
    tiles_per_stage = max(1, rows // SUBLANE // 4)
    done = 0
    for q in range(ns):
        carry = h_scr[q]
        for r0 in range(q * l, (q + 1) * l, SUBLANE):
            rs = slice(r0, r0 + SUBLANE)
            hseq = a[rs] * carry + b[rs]
            carry = hseq[SUBLANE - 1:SUBLANE, :]
            y_ref[rs, :] = hseq * gate[rs]
            done += 1
            if done % tiles_per_stage == 0:
                yield None
        h_scr[q] = carry
    yield hand_over


FF_CHUNK = 1024


def _merge_ffn_kernel(x_ref, ya_ref, yb_ref, yc_ref, zg_ref, wbr_ref, wout_ref, g2_ref,
                      wup_ref, wdown_ref, gf_ref, o_ref, *, final):
    merged = None
    for n, y_ref in enumerate((ya_ref, yb_ref, yc_ref)):
        proj = _dot(y_ref[...], wbr_ref[n])
        term = _sigmoid(zg_ref[:, n * D_MODEL:(n + 1) * D_MODEL]) * proj
        merged = term if merged is None else merged + term
    x = x_ref[...] + _dot(merged, wout_ref[...])
    h = x * lax.rsqrt(jnp.mean(x * x, axis=-1, keepdims=True) + EPS) * g2_ref[...]
    h = h.astype(BF16)
    acc = x
    for j in range(0, D_FF, FF_CHUNK):
        up = jnp.dot(h, wup_ref[:, j:j + FF_CHUNK], preferred_element_type=F32)
        acc = acc + _dot(jnp.square(jnp.maximum(up, 0.0)), wdown_ref[j:j + FF_CHUNK, :])
    if final:
        acc = acc * lax.rsqrt(jnp.mean(acc * acc, axis=-1, keepdims=True) + EPS) * gf_ref[...]
    o_ref[...] = acc


def _merge_ffn(x, ya, yb, yc, zg, wbr, wout, g2, wup, wdown, gf, tm, final, layer):
    m = x.shape[0]
    spec = functools.partial(_param_spec, layer=layer)
    tile = lambda wd: pl.BlockSpec((tm, wd), lambda i: (i, 0))
    return pl.pallas_call(
        functools.partial(_merge_ffn_kernel, final=final),
        grid=(m // tm,),
        in_specs=[tile(D_MODEL), tile(A_V_W), tile(B_W), tile(C_WIDTH), tile(ZG_W),
                  spec((N_BRANCH, A_V_W, D_MODEL)), spec((D_MODEL, D_MODEL)),
                  spec((1, D_MODEL)), spec((D_MODEL, D_FF)),
                  spec((D_FF, D_MODEL)), _param_spec((1, D_MODEL), 0)],
        out_specs=tile(D_MODEL),
        out_shape=jax.ShapeDtypeStruct((m, D_MODEL), F32),
        compiler_params=pltpu.CompilerParams(dimension_semantics=("arbitrary",),
                                             vmem_limit_bytes=VMEM_LIMIT),
        name="merge_ffn",
    )(x, ya, yb, yc, zg, wbr, wout, g2, wup, wdown, gf)


def _block_diag(w):
    g, n, _ = w.shape
    eye = jnp.eye(g, dtype=w.dtype)
    return (eye[:, None, :, None] * w[:, :, None, :]).reshape(g * n, g * n)


def _prepare_params(p):
    depth = p['w_in'].shape[0]
    row = lambda a: a.reshape(depth, 1, -1).astype(F32)
    w_in = p['w_in'].astype(BF16)
    lane_row = lambda a: jnp.zeros((depth, 1, LANE), F32).at[:, 0, A_HEADS:2 * A_HEADS].set(a)
    zeros_lora = jnp.zeros((depth, B_LORA_WA // 2, B_W), F32)
    block_diag = jax.vmap(_block_diag)
    gdn = (p['a_conv_w'], lane_row(p['a_A_log']), lane_row(p['a_dt_bias']), row(p['a_norm_g']))
    rwkv = (row(p['b_mu']), row(p['b_w0']),
            jnp.concatenate([p['b_w_up'], zeros_lora], axis=1).astype(BF16),
            row(p['b_a0']),
            jnp.concatenate([zeros_lora, p['b_a_up']], axis=1).astype(BF16),
            p['b_g_up'].astype(BF16),
            row(p['b_k_k']), row(p['b_k_a']), row(p['b_r_k']), row(p['b_ln_w']), row(p['b_ln_b']),
            jnp.kron(jnp.eye(B_HEADS, dtype=F32), jnp.ones((B_N, B_N), F32)).astype(BF16)[None])
    rglru = (p['c_conv_w'], row(p['c_conv_b']),
             jnp.concatenate([block_diag(p['c_wa']), block_diag(p['c_wx'])], axis=2).astype(BF16),
             jnp.concatenate([row(p['c_ba']), row(p['c_bx'])], axis=2), row(p['c_L']))
    ffn = (p['w_branch'].astype(BF16), p['w_out'].astype(BF16), row(p['norm2_g']),
           p['w_up'].astype(BF16), p['w_down'].astype(BF16))
    return dict(norm1_g=row(p['norm1_g']), w_in=w_in, gdn=gdn, rwkv=rwkv, rglru=rglru, ffn=ffn)


PROMPT_BLOCKING = dict(c=64, nc=8, ns=1, gdn_group=4, rwkv_group=4)
SAMPLE_BLOCKING = dict(c=SUBLANE, nc=1, ns=8, gdn_group=8, rwkv_group=4)
TOKEN_TILE = 256
FFN_TILE = 512


def _layer(x, states, new_states, lp, gf, *, layer, state_layer, bsz, t_pad, t_valid, blocking, final):
    a_s, a_conv, b_s, b_shift, c_h, c_conv = states
    prev = (lambda i, j: None) if new_states is None else (lambda i, j: (new_states[i], new_states[j]))
    kw = dict(bsz=bsz, t_pad=t_pad, t_valid=t_valid, layer=layer, state_layer=state_layer)
    chunked = dict(c=blocking['c'], nc=blocking['nc'], ns=blocking['ns'], **kw)
    za, zb, yc, zg, c_h, c_conv = _in_proj(x, lp['norm1_g'], lp['w_in'], c_conv,
                                           c_h.reshape(-1, bsz, 1, C_WIDTH), lp['rglru'], tm=TOKEN_TILE,
                                           prev=prev(4, 5), **kw)
    ya, a_s, a_conv = _gdn(za, a_conv, a_s, *lp['gdn'], group=blocking['gdn_group'], prev=prev(0, 1),
                           **chunked)
    yb, b_s, b_shift = _rwkv(zb, b_shift, b_s, lp['rwkv'], group=blocking['rwkv_group'], prev=prev(2, 3),
                             **chunked)
    x = _merge_ffn(x, ya, yb, yc, zg, *lp['ffn'], gf, min(FFN_TILE, x.shape[0]), final, layer)
    return x, (a_s, a_conv, b_s, b_shift, c_h, c_conv)


def kernel(x_prompt, x_sample, state_a_S, state_a_conv, state_b_S, state_b_shift, state_c_h, state_c_conv, norm1_g, w_in, a_conv_w, a_A_log, a_dt_bias, a_norm_g, b_mu, b_w0, b_w_up, b_a0, b_a_up, b_g_up, b_k_k, b_k_a, b_r_k, b_ln_w, b_ln_b, c_conv_w, c_conv_b, c_wa, c_ba, c_wx, c_bx, c_L, w_branch, w_out, norm2_g, w_up, w_down, final_norm_g):
    p = dict(norm1_g=norm1_g, w_in=w_in, a_conv_w=a_conv_w, a_A_log=a_A_log, a_dt_bias=a_dt_bias,
             a_norm_g=a_norm_g, b_mu=b_mu, b_w0=b_w0, b_w_up=b_w_up, b_a0=b_a0, b_a_up=b_a_up,
             b_g_up=b_g_up, b_k_k=b_k_k, b_k_a=b_k_a, b_r_k=b_r_k, b_ln_w=b_ln_w, b_ln_b=b_ln_b,
             c_conv_w=c_conv_w, c_conv_b=c_conv_b, c_wa=c_wa, c_ba=c_ba, c_wx=c_wx, c_bx=c_bx,
             c_L=c_L, w_branch=w_branch, w_out=w_out, norm2_g=norm2_g, w_up=w_up, w_down=w_down)
    bp, tp, _ = x_prompt.shape
    bs, ts, _ = x_sample.shape
    ts_pad = -(-ts // SUBLANE) * SUBLANE
    assert ts_pad == SAMPLE_BLOCKING['c'] and tp % (PROMPT_BLOCKING['c'] * PROMPT_BLOCKING['nc']) == 0
    gf = final_norm_g.reshape(1, 1, D_MODEL)

    xp = x_prompt.reshape(bp * tp, D_MODEL)
    xs = jnp.pad(x_sample, ((0, 0), (0, ts_pad - ts), (0, 0))).reshape(bs * ts_pad, D_MODEL)
    zero = lambda *shape: jnp.zeros((1,) + shape, F32)
    p_states = (zero(bp, A_HEADS, A_DK, A_DV), zero(bp, 3, A_CONV_CH),
                zero(bp, B_HEADS, B_N, B_N), zero(bp, 1, B_IN),
                zero(bp, C_WIDTH), zero(bp, 3, C_WIDTH))
    s_states = (state_a_S, state_a_conv, state_b_S, state_b_shift, state_c_h, state_c_conv)
    lp = _prepare_params(p)
    p_new = s_new = None
    for l in range(DEPTH):
        final = l == DEPTH - 1
        xp, p_new = _layer(xp, p_states, p_new, lp, gf, layer=l, state_layer=0, bsz=bp, t_pad=tp,
                           t_valid=tp, blocking=PROMPT_BLOCKING, final=final)
        xs, s_new = _layer(xs, s_states, s_new, lp, gf, layer=l, state_layer=l, bsz=bs, t_pad=ts_pad,
                           t_valid=ts, blocking=SAMPLE_BLOCKING, final=final)
    squeeze_h = lambda st: st[:4] + (st[4].reshape(DEPTH, -1, C_WIDTH), st[5])
    y_prompt = xp.reshape(bp, tp, D_MODEL)
    y_sample = xs.reshape(bs, ts_pad, D_MODEL)[:, :ts]
    return (y_prompt, y_sample) + squeeze_h(p_new) + squeeze_h(s_new)
```

```python
import functools

import jax
import jax.numpy as jnp
from jax import lax
from jax.experimental import pallas as pl
from jax.experimental.pallas import tpu as pltpu

F32 = jnp.float32
BF16 = jnp.bfloat16
HIGHEST = lax.Precision.HIGHEST

EPS = 1e-6
D_MODEL = 1024
DEPTH = 2
A_HEADS = 4
A_DK = 128
A_DV = 128
A_QK_W = A_HEADS * A_DK
A_V_W = A_HEADS * A_DV
A_CONV_CH = 2 * A_QK_W + A_V_W
B_N = 64
B_HEADS = 8
B_W = B_HEADS * B_N
B_LORA_WA = 128
B_GATE_LORA = 128
B_IN = 3 * B_W + B_LORA_WA + B_GATE_LORA
B_LN_EPS = 64e-5
DECAY_SCALE = 0.6065306597126334
C_WIDTH = 512
C_BLOCKS = 8
C_POW = 8.0
N_BRANCH = 3
D_FF = 4 * D_MODEL

LANE = 128
SUBLANE = 8
CONV_TAPS = 4
CONV_PAD = SUBLANE

ZA_W = A_CONV_CH + A_V_W + LANE
ZA_BA = A_CONV_CH + A_V_W
ZB_W = B_IN
ZC_W = 2 * C_WIDTH
ZG_W = N_BRANCH * D_MODEL
Z_WIDTHS = (ZA_W, ZB_W, ZC_W, ZG_W)
A_IN = A_CONV_CH + A_V_W + 2 * A_HEADS
N_IN = A_IN + ZB_W + ZC_W + ZG_W

VMEM_LIMIT = 60 * 1024 * 1024
PROJ_PIECE = 512


def _dot(a, b):
    return jnp.dot(a.astype(BF16), b.astype(BF16), preferred_element_type=F32)


def _dot_nt(a, b):
    return lax.dot_general(a.astype(BF16), b.astype(BF16), (((1,), (1,)), ((), ())),
                           preferred_element_type=F32)


def _dot_tn(a, b):
    return lax.dot_general(a.astype(BF16), b.astype(BF16), (((0,), (0,)), ((), ())),
                           preferred_element_type=F32)


def _sigmoid(x):
    return 1.0 / (1.0 + jnp.exp(-x))


def _silu(x):
    return x * _sigmoid(x)


def _softplus(x):
    return jnp.maximum(x, 0.0) + jnp.log1p(jnp.exp(-jnp.abs(x)))


def _gelu_tanh(x):
    return 0.5 * x * (1.0 + jnp.tanh(0.7978845608028654 * (x + 0.044715 * (x * x * x))))


def _rows(shape):
    return lax.broadcasted_iota(jnp.int32, shape, 0)


def _cols(shape):
    return lax.broadcasted_iota(jnp.int32, shape, 1)


def _rows_in(shape, period):
    r = _rows(shape)
    return r if period >= shape[0] else r & (period - 1)


def _shift_rows(x, s, fill, period):
    return jnp.where(_rows_in(x.shape, period) >= s, pltpu.roll(x, s, 0), fill)


def _bf16_pieces(x, n):
    pieces = []
    for _ in range(n):
        p = x.astype(BF16)
        pieces.append(p)
        x = x - p.astype(F32)
    return pieces


def _cumsum_chunks(x, c):
    tri = (_rows((c, c)) >= _cols((c, c))).astype(BF16)
    pieces = _bf16_pieces(x, 3)
    chunks = [sum(jnp.dot(tri, p[r0:r0 + c], preferred_element_type=F32) for p in pieces)
              for r0 in range(0, x.shape[0], c)]
    return chunks[0] if len(chunks) == 1 else jnp.concatenate(chunks, axis=0)


def _unit_lower_inverses(ns):
    c = ns[0].shape[0]
    eye = (_rows(ns[0].shape) == _cols(ns[0].shape)).astype(F32)
    ts = [eye + n for n in ns]
    ps = list(ns)
    s = 2
    while s < c:
        ps = [_dot(p, p) for p in ps]
        ts = [t + _dot(t, p) for t, p in zip(ts, ps)]
        s *= 2
    return ts


def _unit_lower_solves(ns, xs):
    c = ns[0].shape[0]
    xs = [x + _dot(n, x) for n, x in zip(ns, xs)]
    ps = list(ns)
    s = 2
    while s < c:
        ps = [_dot(p, p) for p in ps]
        xs = [x + _dot(p, x) for p, x in zip(ps, xs)]
        s *= 2
    return xs


def _param_spec(shape, layer):
    nd = len(shape)
    return pl.BlockSpec((None,) + shape, lambda *_: (layer,) + (0,) * nd, pipeline_mode=pl.Buffered(1))


def _group_of_2d(b, t):
    return b


def _stacked_spec(layers, shape, group_of=_group_of_2d):
    nd = len(shape) - 1
    return pl.BlockSpec((layers,) + shape, lambda *g: (0, group_of(*g)) + (0,) * nd)


def _emit_state(last, out_ref, prev_ref, new_ref):
    @pl.when(last)
    def _():
        n_prev = out_ref.shape[0] - 1
        if n_prev:
            out_ref[0:n_prev] = prev_ref[...]
        out_ref[n_prev] = new_ref[...]


def _state_spec(shape, layer, group_of=_group_of_2d):
    nd = len(shape) - 1
    return pl.BlockSpec((None,) + shape, lambda *g: (layer, group_of(*g)) + (0,) * nd)


def _in_proj_kernel(prev_h_ref, prev_conv_ref, prev_aconv_ref, x_ref, g_ref, w_ref, conv0_ref, h0_ref,
                    aconv0_ref, convw_ref, convb_ref, wg_ref, bg_ref, l_ref, aconvw_ref,
                    za_ref, zb_ref, yc_ref, zg_ref, hout_ref, convout_ref, aconvout_ref,
                    w_rest, xpad, h_scr, xpad_a, *, l, ns, nt, valid_rows):
    step = pl.program_id(0)
    t = step % nt
    first = t == 0
    last = t == nt - 1
    @pl.when(step == 0)
    def _():
        w_rest[...] = w_ref[:, A_IN:]

    x = x_ref[...]
    h = x * lax.rsqrt(jnp.mean(x * x, axis=-1, keepdims=True) + EPS) * g_ref[...]
    h = h.astype(BF16)
    proj = lambda ref, off, width: jnp.dot(h, ref[:, off:off + width], preferred_element_type=F32)

    @pl.when(first)
    def _():
        xpad_a[:, CONV_PAD - CONV_TAPS + 1:CONV_PAD, :] = aconv0_ref[...]

    def conv_piece(lo):
        cols = slice(lo, lo + PROJ_PIECE)
        zq = proj(w_ref, lo, PROJ_PIECE)
        y = zq * aconvw_ref[CONV_TAPS - 1:CONV_TAPS, cols]
        for q in range(ns):
            xpad_a[q, CONV_PAD:CONV_PAD + l, cols] = zq[q * l:(q + 1) * l]
        for j in range(CONV_TAPS - 1):
            lo_row = CONV_PAD - CONV_TAPS + 1 + j
            parts = [xpad_a[q, lo_row:lo_row + l, cols] for q in range(ns)]
            view = parts[0] if ns == 1 else jnp.concatenate(parts, axis=0)
            y = y + view * aconvw_ref[j:j + 1, cols]
        za_ref[:, cols] = _silu(y)

    def gate_piece():
        za_ref[:, A_CONV_CH:] = proj(w_ref, A_CONV_CH, ZA_W - A_CONV_CH)

    def project(out_ref, w, off, width):
        def piece(lo):
            hi = min(lo + PROJ_PIECE, width)
            out_ref[:, lo:hi] = proj(w, off + lo, hi - lo)
        return [functools.partial(piece, lo) for lo in range(0, width, PROJ_PIECE)]

    zc = proj(w_rest, ZB_W, ZC_W)
    assert A_CONV_CH % PROJ_PIECE == 0
    pieces = ([functools.partial(conv_piece, lo) for lo in range(0, A_CONV_CH, PROJ_PIECE)] + [gate_piece]
              + project(zb_ref, w_rest, 0, ZB_W) + project(zg_ref, w_rest, ZB_W + ZC_W, ZG_W))
    stages = _rglru_block(
        zc, yc_ref,
        (prev_h_ref, prev_conv_ref, conv0_ref, h0_ref, convw_ref, convb_ref, wg_ref, bg_ref, l_ref,
         hout_ref, convout_ref),
        (xpad, h_scr), first, last, l=l, ns=ns, valid_rows=valid_rows)
    hand_over = None
    for hand_over in stages:
        if pieces:
            pieces.pop(0)()
    for piece in pieces:
        piece()
    hand_over()
    _carry_history(xpad_a, l, valid_rows, last, aconvout_ref, prev_aconv_ref, CONV_TAPS - 1)


def _in_proj(x, g, w, conv0, h0, aconv0, prm, aconvw, *, tm, bsz, t_pad, t_valid, layer, state_layer, prev):
    m = x.shape[0]
    l = min(tm, t_pad)
    ns = tm // l
    _, _, valid_rows = _mixer_grid(bsz, t_pad, t_valid, l, ns)
    nt = t_pad // l
    group_of = lambda i: i // nt
    spec = functools.partial(_param_spec, layer=layer)
    kern = functools.partial(_in_proj_kernel, l=l, ns=ns, nt=nt, valid_rows=valid_rows)
    state_shapes = ((1, C_WIDTH), (3, C_WIDTH), (3, A_CONV_CH))
    kern, prev_args, prev_specs, n_prev = _with_prev(kern, prev, state_shapes, ns, group_of)
    tile = lambda wd: pl.BlockSpec((tm, wd), lambda i: (i, 0))
    widths = (ZA_W, ZB_W, C_WIDTH, ZG_W)
    return pl.pallas_call(
        kern,
        grid=(m // tm,),
        in_specs=prev_specs + [
                  tile(D_MODEL), spec((1, D_MODEL)), spec((D_MODEL, N_IN)),
                  _state_spec((ns, 3, C_WIDTH), state_layer, group_of),
                  _state_spec((ns, 1, C_WIDTH), state_layer, group_of),
                  _state_spec((ns, 3, A_CONV_CH), state_layer, group_of),
                  spec((CONV_TAPS, C_WIDTH)), spec((1, C_WIDTH)),
                  spec((C_WIDTH, 2 * C_WIDTH)), spec((1, 2 * C_WIDTH)),
                  spec((1, C_WIDTH)), spec((CONV_TAPS, A_CONV_CH))],
        out_specs=[tile(wd) for wd in widths]
        + [_stacked_spec(n_prev + 1, (ns,) + shp, group_of) for shp in state_shapes],
        out_shape=[jax.ShapeDtypeStruct((m, wd), F32) for wd in widths]
        + [jax.ShapeDtypeStruct((n_prev + 1, bsz) + shp, F32) for shp in state_shapes],
        scratch_shapes=[pltpu.VMEM((D_MODEL, N_IN - A_IN), BF16),
                        pltpu.VMEM((ns, CONV_PAD + l, C_WIDTH), F32),
                        pltpu.VMEM((ns, 1, C_WIDTH), F32),
                        pltpu.VMEM((ns, CONV_PAD + l, A_CONV_CH), F32)],
        compiler_params=pltpu.CompilerParams(dimension_semantics=("arbitrary",),
                                             vmem_limit_bytes=VMEM_LIMIT),
        name="in_proj",
    )(*prev_args, x, g, w, conv0, h0, aconv0, *prm, aconvw)


def _with_history(x, xpad, first, hist_ref, ns, l, depth):
    @pl.when(first)
    def _():
        xpad[:, CONV_PAD - depth:CONV_PAD, :] = hist_ref[...]

    for s in range(ns):
        xpad[s, CONV_PAD:CONV_PAD + l, :] = x[s * l:(s + 1) * l]
    views = []
    for j in range(depth, 0, -1):
        parts = [xpad[s, CONV_PAD - j:CONV_PAD - j + l, :] for s in range(ns)]
        views.append(parts[0] if ns == 1 else jnp.concatenate(parts, axis=0))
    return views


def _carry_history(xpad, l, valid_rows, last, out_ref, prev_ref, depth):
    _emit_state(last, out_ref, prev_ref,
                xpad.at[:, CONV_PAD - depth + valid_rows:CONV_PAD + valid_rows, :])

    nxt = xpad[:, CONV_PAD - depth + l:CONV_PAD + l, :]
    xpad[:, CONV_PAD - depth:CONV_PAD, :] = nxt


def _causal_conv(x, xpad, w_ref, first, hist_ref, ns, l):
    y = x * w_ref[CONV_TAPS - 1:CONV_TAPS, :]
    for j, view in enumerate(_with_history(x, xpad, first, hist_ref, ns, l, CONV_TAPS - 1)):
        y = y + view * w_ref[j:j + 1, :]
    return y


def _gdn_kernel(prev_s_ref, za_ref, s0_ref, alog_ref, dt_ref, ng_ref,
                y_ref, sout_ref, s_scr, *, c, nc, ns, nt, valid_rows, group):
    t = pl.program_id(1)
    first = t == 0
    last = t == nt - 1
    l = nc * c
    rows = ns * l

    @pl.when(first)
    def _():
        s_scr[...] = s0_ref[...]

    qkv = za_ref[:, 0:A_CONV_CH]

    ba = za_ref[:, ZA_BA:ZA_BA + LANE]
    ba = jnp.where(_cols(ba.shape) < 2 * A_HEADS, ba, 0.0)
    beta_all = _sigmoid(ba)
    g_all = -jnp.exp(alog_ref[...]) * _softplus(ba + dt_ref[...])
    masked = valid_rows < l
    if masked:
        valid = _rows_in((rows, LANE), l) < valid_rows
        beta_all = jnp.where(valid, beta_all, 0.0)
        g_all = jnp.where(valid, g_all, 0.0)
    gc_all = g_all
    s = 1
    while s < c:
        gc_all = gc_all + _shift_rows(gc_all, s, 0.0, c)
        s *= 2

    rr = _rows((c, c))
    cc = _cols((c, c))
    causal = rr >= cc
    strict = rr > cc
    ng = ng_ref[...]
    pick = (_cols((SUBLANE * A_HEADS, LANE))
            == A_HEADS + _rows((SUBLANE * A_HEADS, LANE)) // SUBLANE).astype(F32)

    all_chunks = [(s, n) for n in range(nc) for s in range(ns)]
    heads = range(A_HEADS)
    state = {(s, h): s_scr[s, h] for s in range(ns) for h in heads}

    def advance(chunks):
        inst = [(s, n, h) for (s, n) in chunks for h in heads]
        lower, attn, rhs2, kdec, qg, glast = {}, {}, {}, {}, {}, {}
        for (s, n) in chunks:
            r0 = s * l + n * c
            rs = slice(r0, r0 + c)
            gc_rows = lax.dot_general(pick, gc_all[rs], (((1,), (1,)), ((), ())), precision=HIGHEST,
                                      preferred_element_type=F32)
            for h in heads:
                q = qkv[rs, h * A_DK:(h + 1) * A_DK]
                k = qkv[rs, A_QK_W + h * A_DK:A_QK_W + (h + 1) * A_DK]
                v = qkv[rs, 2 * A_QK_W + h * A_DV:2 * A_QK_W + (h + 1) * A_DV]
                q = q * lax.rsqrt(jnp.sum(q * q, axis=-1, keepdims=True) + EPS) * (A_DK ** -0.5)
                k = k * lax.rsqrt(jnp.sum(k * k, axis=-1, keepdims=True) + EPS)
                if masked:
                    valid_c = _rows((c, A_DK)) < valid_rows - n * c
                    k = jnp.where(valid_c, k, 0.0)
                    v = jnp.where(valid_c, v, 0.0)
                beta = beta_all[rs, h:h + 1]
                gc = gc_all[rs, A_HEADS + h:A_HEADS + h + 1]
                gc_row = gc_rows[SUBLANE * h:SUBLANE * h + 1, :]
                dec = jnp.where(causal, jnp.exp(jnp.where(causal, gc - gc_row, 0.0)), 0.0)
                kb = k * beta
                egc = jnp.exp(gc)
                sc = _dot_nt(jnp.concatenate([kb, q], axis=0), k)
                key = (s, n, h)
                lower[key] = jnp.where(strict, -sc[:c] * dec, 0.0)
                attn[key] = sc[c:] * dec
                rhs2[key] = jnp.concatenate([v * beta, kb * egc], axis=1)
                gc_last = gc[c - 1:c, :]
                kdec[key] = k * jnp.exp(gc_last - gc)
                glast[key] = jnp.exp(gc_last)
                qg[key] = q * egc
        tinv = dict(zip(inst, _unit_lower_inverses([lower[i] for i in inst])))
        uw = {i: _dot(tinv[i], rhs2[i]) for i in inst}
        ao = {i: _dot(attn[i], uw[i]) for i in inst}
        bm = {i: _dot_tn(kdec[i], uw[i]) for i in inst}
        for (s, n) in chunks:
            r0 = s * l + n * c
            for h in heads:
                i = (s, n, h)
                st = state[(s, h)]
                o = _dot(qg[i] - ao[i][:, A_DV:], st) + ao[i][:, :A_DV]
                state[(s, h)] = st * glast[i] - _dot(bm[i][:, A_DV:], st) + bm[i][:, :A_DV]
                o = o * lax.rsqrt(jnp.mean(o * o, axis=-1, keepdims=True) + EPS)
                z = za_ref[r0:r0 + c, A_CONV_CH + h * A_DV:A_CONV_CH + (h + 1) * A_DV]
                y_ref[r0:r0 + c, h * A_DV:(h + 1) * A_DV] = o * ng * _silu(z)

    for g in range(0, len(all_chunks), group):
        advance(all_chunks[g:g + group])
    for s in range(ns):
        for h in heads:
            s_scr[s, h] = state[(s, h)]

    _emit_state(last, sout_ref, prev_s_ref, s_scr)


def _with_prev(kern, prev, shapes, ns, group_of=_group_of_2d):
    if prev is None:
        return functools.partial(kern, *([None] * len(shapes))), [], [], 0
    n_prev = prev[0].shape[0]
    return kern, list(prev), [_stacked_spec(n_prev, (ns,) + shp, group_of) for shp in shapes], n_prev


def _mixer_grid(bsz, t_pad, t_valid, l, ns):
    assert t_pad % l == 0 and bsz % ns == 0 and (ns == 1 or t_pad == l)
    nt = t_pad // l
    valid_rows = t_valid - (nt - 1) * l
    assert 0 < valid_rows <= l and (nt == 1 or valid_rows == l)
    return (bsz // ns, nt), ns * l, valid_rows


def _gdn(za, s0, alog_row, dt_row, ng, *, bsz, t_pad, t_valid, c, nc, ns, group,
         layer, state_layer, prev):
    grid, rows, valid_rows = _mixer_grid(bsz, t_pad, t_valid, nc * c, ns)
    spec = functools.partial(_param_spec, layer=layer)
    nt = grid[1]
    kern = functools.partial(_gdn_kernel, c=c, nc=nc, ns=ns, nt=nt, valid_rows=valid_rows,
                             group=group)
    state_shapes = ((A_HEADS, A_DK, A_DV),)
    kern, prev_args, prev_specs, n_prev = _with_prev(kern, prev, state_shapes, ns)
    return pl.pallas_call(
        kern,
        grid=grid,
        in_specs=prev_specs + [
                  pl.BlockSpec((rows, ZA_W), lambda b, t: (b * nt + t, 0)),
                  _state_spec((ns, A_HEADS, A_DK, A_DV), state_layer),
                  spec((1, LANE)),
                  spec((1, LANE)),
                  spec((1, A_DV))],
        out_specs=[pl.BlockSpec((rows, A_V_W), lambda b, t: (b * nt + t, 0)),
                   ] + [_stacked_spec(n_prev + 1, (ns,) + shp) for shp in state_shapes],
        out_shape=[jax.ShapeDtypeStruct((bsz * t_pad, A_V_W), F32)]
        + [jax.ShapeDtypeStruct((n_prev + 1, bsz) + shp, F32) for shp in state_shapes],
        scratch_shapes=[pltpu.VMEM((ns, A_HEADS, A_DK, A_DV), F32)],
        compiler_params=pltpu.CompilerParams(dimension_semantics=("arbitrary", "arbitrary"),
                                             vmem_limit_bytes=VMEM_LIMIT),
        name="gdn",
    )(*prev_args, za, s0, alog_row, dt_row, ng)


def _rwkv_kernel(prev_s_ref, prev_shift_ref, zb_ref, shift0_ref, s0_ref, mu_ref, w0_ref, wup_ref, a0_ref,
                 aup_ref, gup_ref, kk_ref, ka_ref, rk_ref, lnw_ref, lnb_ref, hsum_ref,
                 y_ref, sout_ref, shiftout_ref, xpad, s_scr, prep, *, c, nc, ns, nt, valid_rows, group):
    t = pl.program_id(1)
    first = t == 0
    last = t == nt - 1
    l = nc * c
    rows = ns * l

    @pl.when(first)
    def _():
        s_scr[...] = s0_ref[...]

    zb = zb_ref[...]
    (prev,) = _with_history(zb, xpad, first, shift0_ref, ns, l, 1)
    _carry_history(xpad, l, valid_rows, last, shiftout_ref, prev_shift_ref, 1)
    zs = zb + (prev - zb) * mu_ref[...]

    r_all = zs[:, 0:B_W]
    k_all = zs[:, B_W:2 * B_W]
    v_all = zs[:, 2 * B_W:3 * B_W]
    xwa = zs[:, 3 * B_W:3 * B_W + B_LORA_WA]
    xg = zs[:, 3 * B_W + B_LORA_WA:B_IN]
    lw = -DECAY_SCALE * _sigmoid(w0_ref[...] + _dot(jnp.tanh(xwa), wup_ref[...]))
    a_all = _sigmoid(a0_ref[...] + _dot(xwa, aup_ref[...]))
    g_all = _dot(_sigmoid(xg), gup_ref[...])
    kk_all = k_all * kk_ref[...]
    k_all = k_all * (1.0 + (a_all - 1.0) * ka_ref[...])
    masked = valid_rows < l
    if masked:
        valid = _rows_in((rows, B_W), l) < valid_rows
        lw = jnp.where(valid, lw, 0.0)
        kk_all = jnp.where(valid, kk_all, 0.0)
        k_all = jnp.where(valid, k_all, 0.0)
        v_all = jnp.where(valid, v_all, 0.0)
    gc_all = _cumsum_chunks(lw, c)

    def head_sums(x, pieces=1):
        return sum(jnp.dot(p, hsum_ref[...], preferred_element_type=F32) for p in _bf16_pieces(x, pieces))

    kkn_all = kk_all * lax.rsqrt(head_sums(kk_all * kk_all, pieces=2) + EPS)
    bonus_all = head_sums(r_all * k_all * rk_ref[...]) * v_all

    rr = _rows((2 * c, 2 * c))
    cc = _cols((2 * c, 2 * c))
    tt = jnp.where(rr >= c, rr - c, rr)
    ss = jnp.where(cc >= c, cc - c, cc)
    mask2 = jnp.logical_or(tt > ss, jnp.logical_and(rr >= c, tt == ss))

    prep_names = ('r', 'k', 'v', 'kkn', 'a', 'lw', 'gc', 'bonus', 'g')
    for idx, val in enumerate((r_all, k_all, v_all, kkn_all, a_all, lw, gc_all, bonus_all, g_all)):
        prep[idx] = val
    heads = range(B_HEADS)
    sls = [slice(h * B_N, (h + 1) * B_N) for h in heads]

    def at_rows(name, rs):
        return prep[prep_names.index(name), rs, :]

    def advance(g, carry):
        members = range(group)
        inst = [(j, h) for j in members for h in heads]
        amat, rhs_uw, r_abs, tails, e_last, vs, rows_of, seq_of = {}, {}, {}, {}, {}, {}, {}, {}
        for j in members:
            q = g * group + j
            rs = pl.ds(pl.multiple_of(q * c, c), c)
            rows_of[j] = rs
            seq_of[j] = 0 if ns == 1 else q
            gc = at_rows('gc', rs)
            lwc = at_rows('lw', rs)
            gmid = gc[c // 2:c // 2 + 1, :]
            glast = gc[c - 1:c, :]
            e_last[j] = jnp.exp(glast)
            r = at_rows('r', rs)
            k = at_rows('k', rs)
            v = at_rows('v', rs)
            alpha = -at_rows('kkn', rs)
            bet = at_rows('kkn', rs) * at_rows('a', rs)
            e_inv = jnp.exp(gmid - gc)
            e_tail = jnp.exp(glast - gc)
            lhs_a = alpha * jnp.exp(gc - lwc - gmid)
            lhs_r = r * jnp.exp(gc - gmid)
            rhs_b = bet * e_inv
            rhs_k = k * e_inv
            abs_a = alpha * jnp.exp(gc - lwc)
            abs_r = r * jnp.exp(gc)
            tail_b = bet * e_tail
            tail_k = k * e_tail
            for h in heads:
                sl = sls[h]
                lhs = jnp.concatenate([lhs_a[:, sl], lhs_r[:, sl]], axis=0)
                rhs = jnp.concatenate([rhs_b[:, sl], rhs_k[:, sl]], axis=0)
                key = (j, h)
                amat[key] = jnp.where(mask2, _dot_nt(lhs, rhs), 0.0)
                rhs_uw[key] = abs_a[:, sl]
                r_abs[key] = abs_r[:, sl]
                tails[key] = jnp.concatenate([tail_b[:, sl], tail_k[:, sl]], axis=0)
                vs[key] = v[:, sl]

        from_v = {i: _dot(amat[i][:, c:], vs[i]) for i in inst}
        uw = dict(zip(inst, _unit_lower_solves(
            [amat[i][:c, :c] for i in inst],
            [jnp.concatenate([from_v[i][:c], rhs_uw[i]], axis=1) for i in inst])))
        ro = {i: _dot(amat[i][c:, :c], uw[i]) for i in inst}
        mp = {i: _dot_tn(uw[i][:, B_N:], tails[i][:c]) for i in inst}
        bb = {i: _dot_tn(jnp.concatenate([uw[i][:, :B_N], vs[i]], axis=0), tails[i]) for i in inst}
        state = {}
        outs = []
        for j in members:
            per_head = []
            for h in heads:
                i = (j, h)
                skey = (0 if ns == 1 else j, h)
                st = state[skey] if skey in state else s_scr[seq_of[j], h]
                per_head.append(_dot_nt(r_abs[i] + ro[i][:, B_N:], st) + from_v[i][c:] + ro[i][:, :B_N])
                state[skey] = st * e_last[j][:, sls[h]] + _dot(st, mp[i]) + bb[i]
            outs.append(jnp.concatenate(per_head, axis=1))
        for j, o in zip(members, outs):
            d = o - head_sums(o) * (1.0 / B_N)
            var = head_sums(d * d) * (1.0 / B_N)
            o = d * lax.rsqrt(var + B_LN_EPS) * lnw_ref[...] + lnb_ref[...]
            y_ref[rows_of[j], :] = (o + at_rows('bonus', rows_of[j])) * at_rows('g', rows_of[j])
        for (sk, h), st in state.items():
            s_scr[seq_of[sk], h] = st
        return carry

    assert (ns * nc) % group == 0 and (ns == 1 or nc == 1)
    lax.fori_loop(0, ns * nc // group, advance, 0)

    _emit_state(last, sout_ref, prev_s_ref, s_scr)


def _rwkv(zb, shift0, s0, prm, *, bsz, t_pad, t_valid, c, nc, ns, group, layer, state_layer, prev):
    grid, rows, valid_rows = _mixer_grid(bsz, t_pad, t_valid, nc * c, ns)
    nt = grid[1]
    kern = functools.partial(_rwkv_kernel, c=c, nc=nc, ns=ns, nt=nt, valid_rows=valid_rows,
                             group=group)
    spec = functools.partial(_param_spec, layer=layer)
    row = lambda wd: spec((1, wd))
    state_shapes = ((B_HEADS, B_N, B_N), (1, B_IN))
    kern, prev_args, prev_specs, n_prev = _with_prev(kern, prev, state_shapes, ns)
    return pl.pallas_call(
        kern,
        grid=grid,
        in_specs=prev_specs + [
                  pl.BlockSpec((rows, ZB_W), lambda b, t: (b * nt + t, 0)),
                  _state_spec((ns, 1, B_IN), state_layer),
                  _state_spec((ns, B_HEADS, B_N, B_N), state_layer),
                  row(B_IN), row(B_W), spec((B_LORA_WA, B_W)), row(B_W),
                  spec((B_LORA_WA, B_W)), spec((B_GATE_LORA, B_W)),
                  row(B_W), row(B_W), row(B_W), row(B_W), row(B_W), _param_spec((B_W, B_W), 0)],
        out_specs=[pl.BlockSpec((rows, B_W), lambda b, t: (b * nt + t, 0)),
                   ] + [_stacked_spec(n_prev + 1, (ns,) + shp) for shp in state_shapes],
        out_shape=[jax.ShapeDtypeStruct((bsz * t_pad, B_W), F32)]
        + [jax.ShapeDtypeStruct((n_prev + 1, bsz) + shp, F32) for shp in state_shapes],
        scratch_shapes=[pltpu.VMEM((ns, CONV_PAD + nc * c, B_IN), F32),
                        pltpu.VMEM((ns, B_HEADS, B_N, B_N), F32),
                        pltpu.VMEM((9, rows, B_W), F32)],
        compiler_params=pltpu.CompilerParams(dimension_semantics=("arbitrary", "arbitrary"),
                                             vmem_limit_bytes=VMEM_LIMIT),
        name="rwkv",
    )(*prev_args, zb, shift0, s0, *prm)


def _rglru_block(zc, y_ref, refs, scratch, first, last, *, l, ns, valid_rows):
    prev_h_ref, prev_conv_ref, conv0_ref, h0_ref, convw_ref, convb_ref, wg_ref, bg_ref, l_ref, \
        hout_ref, convout_ref = refs
    xpad, h_scr = scratch
    rows = ns * l

    @pl.when(first)
    def _():
        h_scr[...] = h0_ref[...]

    xc = _causal_conv(zc[:, 0:C_WIDTH], xpad, convw_ref, first, conv0_ref, ns, l) + convb_ref[...]

    def hand_over():
        _carry_history(xpad, l, valid_rows, last, convout_ref, prev_conv_ref, CONV_TAPS - 1)
        _emit_state(last, hout_ref, prev_h_ref, h_scr)

    yield None
    gates = _sigmoid(_dot(xc, wg_ref[...]) + bg_ref[...])
    yield None
    r = gates[:, :C_WIDTH]
    i = gates[:, C_WIDTH:]
    log_a = -C_POW * r * _softplus(-l_ref[...])
    a = jnp.exp(log_a)
    var = -jnp.tanh(log_a) * (a * a + 1.0)
    b = jnp.where(var > 0.0, var * lax.rsqrt(var), 0.0) * (i * xc)
    yield None
    if valid_rows < l:
        valid = _rows_in((rows, C_WIDTH), l) < valid_rows
        a = jnp.where(valid, a, 1.0)
        b = jnp.where(valid, b, 0.0)
    s = 1
    while s < SUBLANE:
        b = a * _shift_rows(b, s, 0.0, SUBLANE) + b
        a = a * _shift_rows(a, s, 1.0, SUBLANE)
        s *= 2
        yield None
    gate = _gelu_tanh(zc[:, C_WIDTH:])
    yield None
    tiles_per_stage = max(1, rows // SUBLANE // 4)
    done = 0
    for q in range(ns):
        carry = h_scr[q]
        for r0 in range(q * l, (q + 1) * l, SUBLANE):
            rs = slice(r0, r0 + SUBLANE)
            hseq = a[rs] * carry + b[rs]
            carry = hseq[SUBLANE - 1:SUBLANE, :]
            y_ref[rs, :] = hseq * gate[rs]
            done += 1
            if done % tiles_per_stage == 0:
                yield None
        h_scr[q] = carry
    yield hand_over


FF_CHUNK = 1024


def _merge_ffn_kernel(x_ref, ya_ref, yb_ref, yc_ref, zg_ref, wbr_ref, wout_ref, g2_ref,
                      wup_ref, wdown_ref, gf_ref, o_ref, *, final):
    merged = None
    for n, y_ref in enumerate((ya_ref, yb_ref, yc_ref)):
        proj = _dot(y_ref[...], wbr_ref[n])
        term = _sigmoid(zg_ref[:, n * D_MODEL:(n + 1) * D_MODEL]) * proj
        merged = term if merged is None else merged + term
    x = x_ref[...] + _dot(merged, wout_ref[...])
    h = x * lax.rsqrt(jnp.mean(x * x, axis=-1, keepdims=True) + EPS) * g2_ref[...]
    h = h.astype(BF16)
    acc = x
    for j in range(0, D_FF, FF_CHUNK):
        up = jnp.dot(h, wup_ref[:, j:j + FF_CHUNK], preferred_element_type=F32)
        acc = acc + _dot(jnp.square(jnp.maximum(up, 0.0)), wdown_ref[j:j + FF_CHUNK, :])
    if final:
        acc = acc * lax.rsqrt(jnp.mean(acc * acc, axis=-1, keepdims=True) + EPS) * gf_ref[...]
    o_ref[...] = acc


def _merge_ffn(x, ya, yb, yc, zg, wbr, wout, g2, wup, wdown, gf, tm, final, layer):
    m = x.shape[0]
    spec = functools.partial(_param_spec, layer=layer)
    tile = lambda wd: pl.BlockSpec((tm, wd), lambda i: (i, 0))
    return pl.pallas_call(
        functools.partial(_merge_ffn_kernel, final=final),
        grid=(m // tm,),
        in_specs=[tile(D_MODEL), tile(A_V_W), tile(B_W), tile(C_WIDTH), tile(ZG_W),
                  spec((N_BRANCH, A_V_W, D_MODEL)), spec((D_MODEL, D_MODEL)),
                  spec((1, D_MODEL)), spec((D_MODEL, D_FF)),
                  spec((D_FF, D_MODEL)), _param_spec((1, D_MODEL), 0)],
        out_specs=tile(D_MODEL),
        out_shape=jax.ShapeDtypeStruct((m, D_MODEL), F32),
        compiler_params=pltpu.CompilerParams(dimension_semantics=("arbitrary",),
                                             vmem_limit_bytes=VMEM_LIMIT),
        name="merge_ffn",
    )(x, ya, yb, yc, zg, wbr, wout, g2, wup, wdown, gf)


def _block_diag(w):
    g, n, _ = w.shape
    eye = jnp.eye(g, dtype=w.dtype)
    return (eye[:, None, :, None] * w[:, :, None, :]).reshape(g * n, g * n)


def _prepare_params(p):
    depth = p['w_in'].shape[0]
    row = lambda a: a.reshape(depth, 1, -1).astype(F32)
    w_in = p['w_in'].astype(BF16)
    lane_row = lambda a: jnp.zeros((depth, 1, LANE), F32).at[:, 0, A_HEADS:2 * A_HEADS].set(a)
    zeros_lora = jnp.zeros((depth, B_LORA_WA // 2, B_W), F32)
    block_diag = jax.vmap(_block_diag)
    gdn = (p['a_conv_w'], lane_row(p['a_A_log']), lane_row(p['a_dt_bias']), row(p['a_norm_g']))
    rwkv = (row(p['b_mu']), row(p['b_w0']),
            jnp.concatenate([p['b_w_up'], zeros_lora], axis=1).astype(BF16),
            row(p['b_a0']),
            jnp.concatenate([zeros_lora, p['b_a_up']], axis=1).astype(BF16),
            p['b_g_up'].astype(BF16),
            row(p['b_k_k']), row(p['b_k_a']), row(p['b_r_k']), row(p['b_ln_w']), row(p['b_ln_b']),
            jnp.kron(jnp.eye(B_HEADS, dtype=F32), jnp.ones((B_N, B_N), F32)).astype(BF16)[None])
    rglru = (p['c_conv_w'], row(p['c_conv_b']),
             jnp.concatenate([block_diag(p['c_wa']), block_diag(p['c_wx'])], axis=2).astype(BF16),
             jnp.concatenate([row(p['c_ba']), row(p['c_bx'])], axis=2), row(p['c_L']))
    ffn = (p['w_branch'].astype(BF16), p['w_out'].astype(BF16), row(p['norm2_g']),
           p['w_up'].astype(BF16), p['w_down'].astype(BF16))
    return dict(norm1_g=row(p['norm1_g']), w_in=w_in, gdn=gdn, rwkv=rwkv, rglru=rglru, ffn=ffn)


PROMPT_BLOCKING = dict(c=64, nc=8, ns=1, gdn_group=4, rwkv_group=4, in_rows=256)
SAMPLE_BLOCKING = dict(c=SUBLANE, nc=1, ns=8, gdn_group=8, rwkv_group=4, in_rows=128)
FFN_TILE = 512


def _layer(x, states, new_states, lp, gf, *, layer, state_layer, bsz, t_pad, t_valid, blocking, final):
    a_s, a_conv, b_s, b_shift, c_h, c_conv = states
    prev = (lambda *idx: None) if new_states is None else (lambda *idx: tuple(new_states[i] for i in idx))
    kw = dict(bsz=bsz, t_pad=t_pad, t_valid=t_valid, layer=layer, state_layer=state_layer)
    chunked = dict(c=blocking['c'], nc=blocking['nc'], ns=blocking['ns'], **kw)
    a_convw, *gdn_prm = lp['gdn']
    za, zb, yc, zg, c_h, c_conv, a_conv = _in_proj(
        x, lp['norm1_g'], lp['w_in'], c_conv, c_h.reshape(-1, bsz, 1, C_WIDTH), a_conv, lp['rglru'], a_convw,
        tm=blocking['in_rows'], prev=prev(4, 5, 1), **kw)
    ya, a_s = _gdn(za, a_s, *gdn_prm, group=blocking['gdn_group'], prev=prev(0), **chunked)
    yb, b_s, b_shift = _rwkv(zb, b_shift, b_s, lp['rwkv'], group=blocking['rwkv_group'], prev=prev(2, 3),
                             **chunked)
    x = _merge_ffn(x, ya, yb, yc, zg, *lp['ffn'], gf, min(FFN_TILE, x.shape[0]), final, layer)
    return x, (a_s, a_conv, b_s, b_shift, c_h, c_conv)


def kernel(x_prompt, x_sample, state_a_S, state_a_conv, state_b_S, state_b_shift, state_c_h, state_c_conv, norm1_g, w_in, a_conv_w, a_A_log, a_dt_bias, a_norm_g, b_mu, b_w0, b_w_up, b_a0, b_a_up, b_g_up, b_k_k, b_k_a, b_r_k, b_ln_w, b_ln_b, c_conv_w, c_conv_b, c_wa, c_ba, c_wx, c_bx, c_L, w_branch, w_out, norm2_g, w_up, w_down, final_norm_g):
    p = dict(norm1_g=norm1_g, w_in=w_in, a_conv_w=a_conv_w, a_A_log=a_A_log, a_dt_bias=a_dt_bias,
             a_norm_g=a_norm_g, b_mu=b_mu, b_w0=b_w0, b_w_up=b_w_up, b_a0=b_a0, b_a_up=b_a_up,
             b_g_up=b_g_up, b_k_k=b_k_k, b_k_a=b_k_a, b_r_k=b_r_k, b_ln_w=b_ln_w, b_ln_b=b_ln_b,
             c_conv_w=c_conv_w, c_conv_b=c_conv_b, c_wa=c_wa, c_ba=c_ba, c_wx=c_wx, c_bx=c_bx,
             c_L=c_L, w_branch=w_branch, w_out=w_out, norm2_g=norm2_g, w_up=w_up, w_down=w_down)
    bp, tp, _ = x_prompt.shape
    bs, ts, _ = x_sample.shape
    ts_pad = -(-ts // SUBLANE) * SUBLANE
    assert ts_pad == SAMPLE_BLOCKING['c'] and tp % (PROMPT_BLOCKING['c'] * PROMPT_BLOCKING['nc']) == 0
    gf = final_norm_g.reshape(1, 1, D_MODEL)

    xp = x_prompt.reshape(bp * tp, D_MODEL)
    xs = jnp.pad(x_sample, ((0, 0), (0, ts_pad - ts), (0, 0))).reshape(bs * ts_pad, D_MODEL)
    zero = lambda *shape: jnp.zeros((1,) + shape, F32)
    p_states = (zero(bp, A_HEADS, A_DK, A_DV), zero(bp, 3, A_CONV_CH),
                zero(bp, B_HEADS, B_N, B_N), zero(bp, 1, B_IN),
                zero(bp, C_WIDTH), zero(bp, 3, C_WIDTH))
    s_states = (state_a_S, state_a_conv, state_b_S, state_b_shift, state_c_h, state_c_conv)
    lp = _prepare_params(p)
    p_new = s_new = None
    for l in range(DEPTH):
        final = l == DEPTH - 1
        xp, p_new = _layer(xp, p_states, p_new, lp, gf, layer=l, state_layer=0, bsz=bp, t_pad=tp,
                           t_valid=tp, blocking=PROMPT_BLOCKING, final=final)
        xs, s_new = _layer(xs, s_states, s_new, lp, gf, layer=l, state_layer=l, bsz=bs, t_pad=ts_pad,
                           t_valid=ts, blocking=SAMPLE_BLOCKING, final=final)
    squeeze_h = lambda st: st[:4] + (st[4].reshape(DEPTH, -1, C_WIDTH), st[5])
    y_prompt = xp.reshape(bp, tp, D_MODEL)
    y_sample = xs.reshape(bs, ts_pad, D_MODEL)[:, :ts]
    return (y_prompt, y_sample) + squeeze_h(p_new) + squeeze_h(s_new)
```

```python
import functools

import jax
import jax.numpy as jnp
from jax import lax
from jax.experimental import pallas as pl
from jax.experimental.pallas import tpu as pltpu

F32 = jnp.float32
BF16 = jnp.bfloat16

EPS = 1e-6
D_MODEL = 1024
DEPTH = 2
A_HEADS = 4
A_DK = 128
A_DV = 128
A_QK_W = A_HEADS * A_DK
A_V_W = A_HEADS * A_DV
A_CONV_CH = 2 * A_QK_W + A_V_W
B_N = 64
B_HEADS = 8
B_W = B_HEADS * B_N
B_LORA_WA = 128
B_GATE_LORA = 128
B_IN = 3 * B_W + B_LORA_WA + B_GATE_LORA
B_LN_EPS = 64e-5
DECAY_SCALE = 0.6065306597126334
C_WIDTH = 512
C_BLOCKS = 8
C_POW = 8.0
N_BRANCH = 3
D_FF = 4 * D_MODEL

LANE = 128
SUBLANE = 8
CONV_TAPS = 4
CONV_PAD = SUBLANE

ZA_W = A_CONV_CH + A_V_W + LANE
ZA_BA = A_CONV_CH + A_V_W
ZB_W = B_IN
ZC_W = 2 * C_WIDTH
ZG_W = N_BRANCH * D_MODEL
Z_WIDTHS = (ZA_W, ZB_W, ZC_W, ZG_W)
A_IN = A_CONV_CH + A_V_W + 2 * A_HEADS
N_IN = A_IN + ZB_W + ZC_W + ZG_W

VMEM_LIMIT = 60 * 1024 * 1024
PROJ_PIECE = 512
RGLRU_PIECE = 128


def _dot(a, b):
    return jnp.dot(a.astype(BF16), b.astype(BF16), preferred_element_type=F32)


def _dot_nt(a, b):
    return lax.dot_general(a.astype(BF16), b.astype(BF16), (((1,), (1,)), ((), ())),
                           preferred_element_type=F32)


def _dot_tn(a, b):
    return lax.dot_general(a.astype(BF16), b.astype(BF16), (((0,), (0,)), ((), ())),
                           preferred_element_type=F32)


def _sigmoid(x):
    return 1.0 / (1.0 + jnp.exp(-x))


def _silu(x):
    return x * _sigmoid(x)


def _softplus(x):
    return jnp.maximum(x, 0.0) + jnp.log1p(jnp.exp(-jnp.abs(x)))


def _gelu_tanh(x):
    return 0.5 * x * (1.0 + jnp.tanh(0.7978845608028654 * (x + 0.044715 * (x * x * x))))


def _rows(shape):
    return lax.broadcasted_iota(jnp.int32, shape, 0)


def _cols(shape):
    return lax.broadcasted_iota(jnp.int32, shape, 1)


def _rows_in(shape, period):
    r = _rows(shape)
    return r if period >= shape[0] else r & (period - 1)


def _shift_rows(x, s, fill, period):
    return jnp.where(_rows_in(x.shape, period) >= s, pltpu.roll(x, s, 0), fill)


def _bf16_pieces(x, n):
    pieces = []
    for _ in range(n):
        p = x.astype(BF16)
        pieces.append(p)
        x = x - p.astype(F32)
    return pieces


def _cumsum_chunks(x, c):
    tri = (_rows((c, c)) >= _cols((c, c))).astype(BF16)
    pieces = _bf16_pieces(x, 3)
    chunks = [sum(jnp.dot(tri, p[r0:r0 + c], preferred_element_type=F32) for p in pieces)
              for r0 in range(0, x.shape[0], c)]
    return chunks[0] if len(chunks) == 1 else jnp.concatenate(chunks, axis=0)


def _unit_lower_inverses(ns):
    c = ns[0].shape[0]
    eye = (_rows(ns[0].shape) == _cols(ns[0].shape)).astype(F32)
    ts = [eye + n for n in ns]
    ps = list(ns)
    s = 2
    while s < c:
        ps = [_dot(p, p) for p in ps]
        ts = [t + _dot(t, p) for t, p in zip(ts, ps)]
        s *= 2
    return ts


def _unit_lower_solves(ns, xs):
    c = ns[0].shape[0]
    xs = [x + _dot(n, x) for n, x in zip(ns, xs)]
    ps = list(ns)
    s = 2
    while s < c:
        ps = [_dot(p, p) for p in ps]
        xs = [x + _dot(p, x) for p, x in zip(ps, xs)]
        s *= 2
    return xs


def _param_spec(shape, layer):
    nd = len(shape)
    return pl.BlockSpec((None,) + shape, lambda *_: (layer,) + (0,) * nd, pipeline_mode=pl.Buffered(1))


def _group_of_2d(b, t):
    return b


def _stacked_spec(layers, shape, group_of=_group_of_2d):
    nd = len(shape) - 1
    return pl.BlockSpec((layers,) + shape, lambda *g: (0, group_of(*g)) + (0,) * nd)


def _emit_state(last, out_ref, prev_ref, new_ref):
    @pl.when(last)
    def _():
        n_prev = out_ref.shape[0] - 1
        if n_prev:
            out_ref[0:n_prev] = prev_ref[...]
        out_ref[n_prev] = new_ref[...]


def _state_spec(shape, layer, group_of=_group_of_2d):
    nd = len(shape) - 1
    return pl.BlockSpec((None,) + shape, lambda *g: (layer, group_of(*g)) + (0,) * nd)


def _in_proj_kernel(prev_h_ref, prev_conv_ref, prev_aconv_ref, x_ref, g_ref, w_ref, conv0_ref, h0_ref,
                    aconv0_ref, convw_ref, convb_ref, wg_ref, bg_ref, l_ref, aconvw_ref,
                    za_ref, zb_ref, yc_ref, zg_ref, hout_ref, convout_ref, aconvout_ref,
                    w_rest, xpad, h_scr, xpad_a, *, l, ns, nt, valid_rows):
    step = pl.program_id(0)
    t = step % nt
    first = t == 0
    last = t == nt - 1
    @pl.when(step == 0)
    def _():
        w_rest[...] = w_ref[:, A_IN:]

    x = x_ref[...]
    h = x * lax.rsqrt(jnp.mean(x * x, axis=-1, keepdims=True) + EPS) * g_ref[...]
    h = h.astype(BF16)
    proj = lambda ref, off, width: jnp.dot(h, ref[:, off:off + width], preferred_element_type=F32)

    @pl.when(first)
    def _():
        xpad_a[:, CONV_PAD - CONV_TAPS + 1:CONV_PAD, :] = aconv0_ref[...]

    @pl.when(first)
    def _():
        h_scr[...] = h0_ref[...]
        xpad[:, CONV_PAD - CONV_TAPS + 1:CONV_PAD, :] = conv0_ref[...]

    def causal_conv(z, pad, w, cols):
        y = z * w[CONV_TAPS - 1:CONV_TAPS, cols]
        for q in range(ns):
            pad[q, CONV_PAD:CONV_PAD + l, cols] = z[q * l:(q + 1) * l]
        for j in range(CONV_TAPS - 1):
            lo_row = CONV_PAD - CONV_TAPS + 1 + j
            parts = [pad[q, lo_row:lo_row + l, cols] for q in range(ns)]
            view = parts[0] if ns == 1 else jnp.concatenate(parts, axis=0)
            y = y + view * w[j:j + 1, cols]
        return y

    def conv_piece(lo):
        cols = slice(lo, lo + PROJ_PIECE)
        za_ref[:, cols] = _silu(causal_conv(proj(w_ref, lo, PROJ_PIECE), xpad_a, aconvw_ref, cols))

    def gate_piece():
        za_ref[:, A_CONV_CH:] = proj(w_ref, A_CONV_CH, ZA_W - A_CONV_CH)

    def rglru_piece(lo):
        cols = slice(lo, lo + RGLRU_PIECE)
        xc = (causal_conv(proj(w_rest, ZB_W + lo, RGLRU_PIECE), xpad, convw_ref, cols) + convb_ref[:, cols])
        gate_cols = lambda off: slice(off + lo, off + lo + RGLRU_PIECE)
        r = _sigmoid(_dot(xc, wg_ref[cols, gate_cols(0)]) + bg_ref[:, gate_cols(0)])
        i = _sigmoid(_dot(xc, wg_ref[cols, gate_cols(C_WIDTH)]) + bg_ref[:, gate_cols(C_WIDTH)])
        log_a = -C_POW * r * _softplus(-l_ref[:, cols])
        a = jnp.exp(log_a)
        var = -jnp.tanh(log_a) * (a * a + 1.0)
        b = jnp.where(var > 0.0, var * lax.rsqrt(var), 0.0) * (i * xc)
        if valid_rows < l:
            valid = _rows_in(a.shape, l) < valid_rows
            a = jnp.where(valid, a, 1.0)
            b = jnp.where(valid, b, 0.0)
        s = 1
        while s < SUBLANE:
            b = a * _shift_rows(b, s, 0.0, SUBLANE) + b
            a = a * _shift_rows(a, s, 1.0, SUBLANE)
            s *= 2
        gate = _gelu_tanh(proj(w_rest, ZB_W + C_WIDTH + lo, RGLRU_PIECE))
        for q in range(ns):
            carry = h_scr[q, :, cols]
            for r0 in range(q * l, (q + 1) * l, SUBLANE):
                rs = slice(r0, r0 + SUBLANE)
                hseq = a[rs] * carry + b[rs]
                carry = hseq[SUBLANE - 1:SUBLANE, :]
                yc_ref[rs, cols] = hseq * gate[rs]
            h_scr[q, :, cols] = carry

    def project(out_ref, w, off, width):
        def piece(lo):
            hi = min(lo + PROJ_PIECE, width)
            out_ref[:, lo:hi] = proj(w, off + lo, hi - lo)
        return [functools.partial(piece, lo) for lo in range(0, width, PROJ_PIECE)]

    assert A_CONV_CH % PROJ_PIECE == 0 and C_WIDTH % RGLRU_PIECE == 0
    busy = ([functools.partial(rglru_piece, lo) for lo in range(0, C_WIDTH, RGLRU_PIECE)]
            + [functools.partial(conv_piece, lo) for lo in range(0, A_CONV_CH, PROJ_PIECE)])
    plain = ([gate_piece] + project(zb_ref, w_rest, 0, ZB_W) + project(zg_ref, w_rest, ZB_W + ZC_W, ZG_W))
    while busy or plain:
        for queue in (busy, plain):
            if queue:
                queue.pop(0)()
    _carry_history(xpad, l, valid_rows, last, convout_ref, prev_conv_ref, CONV_TAPS - 1)
    _emit_state(last, hout_ref, prev_h_ref, h_scr)
    _carry_history(xpad_a, l, valid_rows, last, aconvout_ref, prev_aconv_ref, CONV_TAPS - 1)


def _in_proj(x, g, w, conv0, h0, aconv0, prm, aconvw, *, tm, bsz, t_pad, t_valid, layer, state_layer, prev):
    m = x.shape[0]
    l = min(tm, t_pad)
    ns = tm // l
    _, _, valid_rows = _mixer_grid(bsz, t_pad, t_valid, l, ns)
    nt = t_pad // l
    group_of = lambda i: i // nt
    spec = functools.partial(_param_spec, layer=layer)
    kern = functools.partial(_in_proj_kernel, l=l, ns=ns, nt=nt, valid_rows=valid_rows)
    state_shapes = ((1, C_WIDTH), (3, C_WIDTH), (3, A_CONV_CH))
    kern, prev_args, prev_specs, n_prev = _with_prev(kern, prev, state_shapes, ns, group_of)
    tile = lambda wd: pl.BlockSpec((tm, wd), lambda i: (i, 0))
    widths = (ZA_W, ZB_W, C_WIDTH, ZG_W)
    return pl.pallas_call(
        kern,
        grid=(m // tm,),
        in_specs=prev_specs + [
                  tile(D_MODEL), spec((1, D_MODEL)), spec((D_MODEL, N_IN)),
                  _state_spec((ns, 3, C_WIDTH), state_layer, group_of),
                  _state_spec((ns, 1, C_WIDTH), state_layer, group_of),
                  _state_spec((ns, 3, A_CONV_CH), state_layer, group_of),
                  spec((CONV_TAPS, C_WIDTH)), spec((1, C_WIDTH)),
                  spec((C_WIDTH, 2 * C_WIDTH)), spec((1, 2 * C_WIDTH)),
                  spec((1, C_WIDTH)), spec((CONV_TAPS, A_CONV_CH))],
        out_specs=[tile(wd) for wd in widths]
        + [_stacked_spec(n_prev + 1, (ns,) + shp, group_of) for shp in state_shapes],
        out_shape=[jax.ShapeDtypeStruct((m, wd), F32) for wd in widths]
        + [jax.ShapeDtypeStruct((n_prev + 1, bsz) + shp, F32) for shp in state_shapes],
        scratch_shapes=[pltpu.VMEM((D_MODEL, N_IN - A_IN), BF16),
                        pltpu.VMEM((ns, CONV_PAD + l, C_WIDTH), F32),
                        pltpu.VMEM((ns, 1, C_WIDTH), F32),
                        pltpu.VMEM((ns, CONV_PAD + l, A_CONV_CH), F32)],
        compiler_params=pltpu.CompilerParams(dimension_semantics=("arbitrary",),
                                             vmem_limit_bytes=VMEM_LIMIT),
        name="in_proj",
    )(*prev_args, x, g, w, conv0, h0, aconv0, *prm, aconvw)


def _with_history(x, xpad, first, hist_ref, ns, l, depth):
    @pl.when(first)
    def _():
        xpad[:, CONV_PAD - depth:CONV_PAD, :] = hist_ref[...]

    for s in range(ns):
        xpad[s, CONV_PAD:CONV_PAD + l, :] = x[s * l:(s + 1) * l]
    views = []
    for j in range(depth, 0, -1):
        parts = [xpad[s, CONV_PAD - j:CONV_PAD - j + l, :] for s in range(ns)]
        views.append(parts[0] if ns == 1 else jnp.concatenate(parts, axis=0))
    return views


def _carry_history(xpad, l, valid_rows, last, out_ref, prev_ref, depth):
    _emit_state(last, out_ref, prev_ref,
                xpad.at[:, CONV_PAD - depth + valid_rows:CONV_PAD + valid_rows, :])

    nxt = xpad[:, CONV_PAD - depth + l:CONV_PAD + l, :]
    xpad[:, CONV_PAD - depth:CONV_PAD, :] = nxt


def _gdn_kernel(prev_s_ref, za_ref, s0_ref, alog_ref, dt_ref, ng_ref,
                y_ref, sout_ref, s_scr, *, c, nc, ns, nt, valid_rows, group):
    t = pl.program_id(1)
    first = t == 0
    last = t == nt - 1
    l = nc * c
    rows = ns * l

    @pl.when(first)
    def _():
        s_scr[...] = s0_ref[...]

    qkv = za_ref[:, 0:A_CONV_CH]

    ba = za_ref[:, ZA_BA:ZA_BA + LANE]
    ba = jnp.where(_cols(ba.shape) < 2 * A_HEADS, ba, 0.0)
    beta_all = _sigmoid(ba)
    g_all = -jnp.exp(alog_ref[...]) * _softplus(ba + dt_ref[...])
    masked = valid_rows < l
    if masked:
        valid = _rows_in((rows, LANE), l) < valid_rows
        beta_all = jnp.where(valid, beta_all, 0.0)
        g_all = jnp.where(valid, g_all, 0.0)
    gc_all = g_all
    s = 1
    while s < c:
        gc_all = gc_all + _shift_rows(gc_all, s, 0.0, c)
        s *= 2

    rr = _rows((c, c))
    cc = _cols((c, c))
    causal = rr >= cc
    strict = rr > cc
    ng = ng_ref[...]
    pick = (_cols((SUBLANE * A_HEADS, LANE))
            == A_HEADS + _rows((SUBLANE * A_HEADS, LANE)) // SUBLANE).astype(BF16)

    all_chunks = [(s, n) for n in range(nc) for s in range(ns)]
    heads = range(A_HEADS)
    state = {(s, h): s_scr[s, h] for s in range(ns) for h in heads}

    def advance(chunks):
        inst = [(s, n, h) for (s, n) in chunks for h in heads]
        lower, attn, rhs2, kdec, qg, glast = {}, {}, {}, {}, {}, {}
        for (s, n) in chunks:
            r0 = s * l + n * c
            rs = slice(r0, r0 + c)
            gc_rows = sum(lax.dot_general(pick, piece, (((1,), (1,)), ((), ())), preferred_element_type=F32)
                          for piece in _bf16_pieces(gc_all[rs], 3))
            for h in heads:
                q = qkv[rs, h * A_DK:(h + 1) * A_DK]
                k = qkv[rs, A_QK_W + h * A_DK:A_QK_W + (h + 1) * A_DK]
                v = qkv[rs, 2 * A_QK_W + h * A_DV:2 * A_QK_W + (h + 1) * A_DV]
                q = q * lax.rsqrt(jnp.sum(q * q, axis=-1, keepdims=True) + EPS) * (A_DK ** -0.5)
                k = k * lax.rsqrt(jnp.sum(k * k, axis=-1, keepdims=True) + EPS)
                if masked:
                    valid_c = _rows((c, A_DK)) < valid_rows - n * c
                    k = jnp.where(valid_c, k, 0.0)
                    v = jnp.where(valid_c, v, 0.0)
                beta = beta_all[rs, h:h + 1]
                gc = gc_all[rs, A_HEADS + h:A_HEADS + h + 1]
                gc_row = gc_rows[SUBLANE * h:SUBLANE * h + 1, :]
                dec = jnp.where(causal, jnp.exp(jnp.where(causal, gc - gc_row, 0.0)), 0.0)
                kb = k * beta
                egc = jnp.exp(gc)
                sc = _dot_nt(jnp.concatenate([kb, q], axis=0), k)
                key = (s, n, h)
                lower[key] = jnp.where(strict, -sc[:c] * dec, 0.0)
                attn[key] = sc[c:] * dec
                rhs2[key] = jnp.concatenate([v * beta, kb * egc], axis=1)
                gc_last = gc[c - 1:c, :]
                kdec[key] = k * jnp.exp(gc_last - gc)
                glast[key] = jnp.exp(gc_last)
                qg[key] = q * egc
        tinv = dict(zip(inst, _unit_lower_inverses([lower[i] for i in inst])))
        uw = {i: _dot(tinv[i], rhs2[i]) for i in inst}
        ao = {i: _dot(attn[i], uw[i]) for i in inst}
        bm = {i: _dot_tn(kdec[i], uw[i]) for i in inst}
        for (s, n) in chunks:
            r0 = s * l + n * c
            for h in heads:
                i = (s, n, h)
                st = state[(s, h)]
                o = _dot(qg[i] - ao[i][:, A_DV:], st) + ao[i][:, :A_DV]
                state[(s, h)] = st * glast[i] - _dot(bm[i][:, A_DV:], st) + bm[i][:, :A_DV]
                o = o * lax.rsqrt(jnp.mean(o * o, axis=-1, keepdims=True) + EPS)
                z = za_ref[r0:r0 + c, A_CONV_CH + h * A_DV:A_CONV_CH + (h + 1) * A_DV]
                y_ref[r0:r0 + c, h * A_DV:(h + 1) * A_DV] = o * ng * _silu(z)

    for g in range(0, len(all_chunks), group):
        advance(all_chunks[g:g + group])
    for s in range(ns):
        for h in heads:
            s_scr[s, h] = state[(s, h)]

    _emit_state(last, sout_ref, prev_s_ref, s_scr)


def _with_prev(kern, prev, shapes, ns, group_of=_group_of_2d):
    if prev is None:
        return functools.partial(kern, *([None] * len(shapes))), [], [], 0
    n_prev = prev[0].shape[0]
    return kern, list(prev), [_stacked_spec(n_prev, (ns,) + shp, group_of) for shp in shapes], n_prev


def _mixer_grid(bsz, t_pad, t_valid, l, ns):
    assert t_pad % l == 0 and bsz % ns == 0 and (ns == 1 or t_pad == l)
    nt = t_pad // l
    valid_rows = t_valid - (nt - 1) * l
    assert 0 < valid_rows <= l and (nt == 1 or valid_rows == l)
    return (bsz // ns, nt), ns * l, valid_rows


def _gdn(za, s0, alog_row, dt_row, ng, *, bsz, t_pad, t_valid, c, nc, ns, group,
         layer, state_layer, prev):
    grid, rows, valid_rows = _mixer_grid(bsz, t_pad, t_valid, nc * c, ns)
    spec = functools.partial(_param_spec, layer=layer)
    nt = grid[1]
    kern = functools.partial(_gdn_kernel, c=c, nc=nc, ns=ns, nt=nt, valid_rows=valid_rows,
                             group=group)
    state_shapes = ((A_HEADS, A_DK, A_DV),)
    kern, prev_args, prev_specs, n_prev = _with_prev(kern, prev, state_shapes, ns)
    return pl.pallas_call(
        kern,
        grid=grid,
        in_specs=prev_specs + [
                  pl.BlockSpec((rows, ZA_W), lambda b, t: (b * nt + t, 0)),
                  _state_spec((ns, A_HEADS, A_DK, A_DV), state_layer),
                  spec((1, LANE)),
                  spec((1, LANE)),
                  spec((1, A_DV))],
        out_specs=[pl.BlockSpec((rows, A_V_W), lambda b, t: (b * nt + t, 0)),
                   ] + [_stacked_spec(n_prev + 1, (ns,) + shp) for shp in state_shapes],
        out_shape=[jax.ShapeDtypeStruct((bsz * t_pad, A_V_W), F32)]
        + [jax.ShapeDtypeStruct((n_prev + 1, bsz) + shp, F32) for shp in state_shapes],
        scratch_shapes=[pltpu.VMEM((ns, A_HEADS, A_DK, A_DV), F32)],
        compiler_params=pltpu.CompilerParams(dimension_semantics=("arbitrary", "arbitrary"),
                                             vmem_limit_bytes=VMEM_LIMIT),
        name="gdn",
    )(*prev_args, za, s0, alog_row, dt_row, ng)


def _rwkv_kernel(prev_s_ref, prev_shift_ref, zb_ref, shift0_ref, s0_ref, mu_ref, w0_ref, wup_ref, a0_ref,
                 aup_ref, gup_ref, kk_ref, ka_ref, rk_ref, lnw_ref, lnb_ref, hsum_ref,
                 y_ref, sout_ref, shiftout_ref, xpad, s_scr, prep, *, c, nc, ns, nt, valid_rows, group):
    t = pl.program_id(1)
    first = t == 0
    last = t == nt - 1
    l = nc * c
    rows = ns * l

    @pl.when(first)
    def _():
        s_scr[...] = s0_ref[...]

    zb = zb_ref[...]
    (prev,) = _with_history(zb, xpad, first, shift0_ref, ns, l, 1)
    _carry_history(xpad, l, valid_rows, last, shiftout_ref, prev_shift_ref, 1)
    zs = zb + (prev - zb) * mu_ref[...]

    r_all = zs[:, 0:B_W]
    k_all = zs[:, B_W:2 * B_W]
    v_all = zs[:, 2 * B_W:3 * B_W]
    xwa = zs[:, 3 * B_W:3 * B_W + B_LORA_WA]
    xg = zs[:, 3 * B_W + B_LORA_WA:B_IN]
    lw = -DECAY_SCALE * _sigmoid(w0_ref[...] + _dot(jnp.tanh(xwa), wup_ref[...]))
    a_all = _sigmoid(a0_ref[...] + _dot(xwa, aup_ref[...]))
    g_all = _dot(_sigmoid(xg), gup_ref[...])
    kk_all = k_all * kk_ref[...]
    k_all = k_all * (1.0 + (a_all - 1.0) * ka_ref[...])
    masked = valid_rows < l
    if masked:
        valid = _rows_in((rows, B_W), l) < valid_rows
        lw = jnp.where(valid, lw, 0.0)
        kk_all = jnp.where(valid, kk_all, 0.0)
        k_all = jnp.where(valid, k_all, 0.0)
        v_all = jnp.where(valid, v_all, 0.0)
    gc_all = _cumsum_chunks(lw, c)

    def head_sums(x, pieces=1):
        return sum(jnp.dot(p, hsum_ref[...], preferred_element_type=F32) for p in _bf16_pieces(x, pieces))

    kkn_all = kk_all * lax.rsqrt(head_sums(kk_all * kk_all, pieces=2) + EPS)
    bonus_all = head_sums(r_all * k_all * rk_ref[...]) * v_all

    rr = _rows((2 * c, 2 * c))
    cc = _cols((2 * c, 2 * c))
    tt = jnp.where(rr >= c, rr - c, rr)
    ss = jnp.where(cc >= c, cc - c, cc)
    mask2 = jnp.logical_or(tt > ss, jnp.logical_and(rr >= c, tt == ss))

    prep_names = ('r', 'k', 'v', 'kkn', 'a', 'lw', 'gc', 'bonus', 'g')
    for idx, val in enumerate((r_all, k_all, v_all, kkn_all, a_all, lw, gc_all, bonus_all, g_all)):
        prep[idx] = val
    heads = range(B_HEADS)
    sls = [slice(h * B_N, (h + 1) * B_N) for h in heads]

    def at_rows(name, rs):
        return prep[prep_names.index(name), rs, :]

    def advance(g, carry):
        members = range(group)
        inst = [(j, h) for j in members for h in heads]
        amat, rhs_uw, r_abs, tails, e_last, vs, rows_of, seq_of = {}, {}, {}, {}, {}, {}, {}, {}
        for j in members:
            q = g * group + j
            rs = pl.ds(pl.multiple_of(q * c, c), c)
            rows_of[j] = rs
            seq_of[j] = 0 if ns == 1 else q
            gc = at_rows('gc', rs)
            lwc = at_rows('lw', rs)
            gmid = gc[c // 2:c // 2 + 1, :]
            glast = gc[c - 1:c, :]
            e_last[j] = jnp.exp(glast)
            r = at_rows('r', rs)
            k = at_rows('k', rs)
            v = at_rows('v', rs)
            alpha = -at_rows('kkn', rs)
            bet = at_rows('kkn', rs) * at_rows('a', rs)
            e_inv = jnp.exp(gmid - gc)
            e_tail = jnp.exp(glast - gc)
            lhs_a = alpha * jnp.exp(gc - lwc - gmid)
            lhs_r = r * jnp.exp(gc - gmid)
            rhs_b = bet * e_inv
            rhs_k = k * e_inv
            abs_a = alpha * jnp.exp(gc - lwc)
            abs_r = r * jnp.exp(gc)
            tail_b = bet * e_tail
            tail_k = k * e_tail
            for h in heads:
                sl = sls[h]
                lhs = jnp.concatenate([lhs_a[:, sl], lhs_r[:, sl]], axis=0)
                rhs = jnp.concatenate([rhs_b[:, sl], rhs_k[:, sl]], axis=0)
                key = (j, h)
                amat[key] = jnp.where(mask2, _dot_nt(lhs, rhs), 0.0)
                rhs_uw[key] = abs_a[:, sl]
                r_abs[key] = abs_r[:, sl]
                tails[key] = jnp.concatenate([tail_b[:, sl], tail_k[:, sl]], axis=0)
                vs[key] = v[:, sl]

        from_v = {i: _dot(amat[i][:, c:], vs[i]) for i in inst}
        uw = dict(zip(inst, _unit_lower_solves(
            [amat[i][:c, :c] for i in inst],
            [jnp.concatenate([from_v[i][:c], rhs_uw[i]], axis=1) for i in inst])))
        ro = {i: _dot(amat[i][c:, :c], uw[i]) for i in inst}
        mp = {i: _dot_tn(uw[i][:, B_N:], tails[i][:c]) for i in inst}
        bb = {i: _dot_tn(jnp.concatenate([uw[i][:, :B_N], vs[i]], axis=0), tails[i]) for i in inst}
        state = {}
        outs = []
        for j in members:
            per_head = []
            for h in heads:
                i = (j, h)
                skey = (0 if ns == 1 else j, h)
                st = state[skey] if skey in state else s_scr[seq_of[j], h]
                per_head.append(_dot_nt(r_abs[i] + ro[i][:, B_N:], st) + from_v[i][c:] + ro[i][:, :B_N])
                state[skey] = st * e_last[j][:, sls[h]] + _dot(st, mp[i]) + bb[i]
            outs.append(jnp.concatenate(per_head, axis=1))
        for j, o in zip(members, outs):
            d = o - head_sums(o) * (1.0 / B_N)
            var = head_sums(d * d) * (1.0 / B_N)
            o = d * lax.rsqrt(var + B_LN_EPS) * lnw_ref[...] + lnb_ref[...]
            y_ref[rows_of[j], :] = (o + at_rows('bonus', rows_of[j])) * at_rows('g', rows_of[j])
        for (sk, h), st in state.items():
            s_scr[seq_of[sk], h] = st
        return carry

    assert (ns * nc) % group == 0 and (ns == 1 or nc == 1)
    lax.fori_loop(0, ns * nc // group, advance, 0)

    _emit_state(last, sout_ref, prev_s_ref, s_scr)


def _rwkv(zb, shift0, s0, prm, *, bsz, t_pad, t_valid, c, nc, ns, group, layer, state_layer, prev):
    grid, rows, valid_rows = _mixer_grid(bsz, t_pad, t_valid, nc * c, ns)
    nt = grid[1]
    kern = functools.partial(_rwkv_kernel, c=c, nc=nc, ns=ns, nt=nt, valid_rows=valid_rows,
                             group=group)
    spec = functools.partial(_param_spec, layer=layer)
    row = lambda wd: spec((1, wd))
    state_shapes = ((B_HEADS, B_N, B_N), (1, B_IN))
    kern, prev_args, prev_specs, n_prev = _with_prev(kern, prev, state_shapes, ns)
    return pl.pallas_call(
        kern,
        grid=grid,
        in_specs=prev_specs + [
                  pl.BlockSpec((rows, ZB_W), lambda b, t: (b * nt + t, 0)),
                  _state_spec((ns, 1, B_IN), state_layer),
                  _state_spec((ns, B_HEADS, B_N, B_N), state_layer),
                  row(B_IN), row(B_W), spec((B_LORA_WA, B_W)), row(B_W),
                  spec((B_LORA_WA, B_W)), spec((B_GATE_LORA, B_W)),
                  row(B_W), row(B_W), row(B_W), row(B_W), row(B_W), _param_spec((B_W, B_W), 0)],
        out_specs=[pl.BlockSpec((rows, B_W), lambda b, t: (b * nt + t, 0)),
                   ] + [_stacked_spec(n_prev + 1, (ns,) + shp) for shp in state_shapes],
        out_shape=[jax.ShapeDtypeStruct((bsz * t_pad, B_W), F32)]
        + [jax.ShapeDtypeStruct((n_prev + 1, bsz) + shp, F32) for shp in state_shapes],
        scratch_shapes=[pltpu.VMEM((ns, CONV_PAD + nc * c, B_IN), F32),
                        pltpu.VMEM((ns, B_HEADS, B_N, B_N), F32),
                        pltpu.VMEM((9, rows, B_W), F32)],
        compiler_params=pltpu.CompilerParams(dimension_semantics=("arbitrary", "arbitrary"),
                                             vmem_limit_bytes=VMEM_LIMIT),
        name="rwkv",
    )(*prev_args, zb, shift0, s0, *prm)


FF_CHUNK = 1024


def _merge_ffn_kernel(x_ref, ya_ref, yb_ref, yc_ref, zg_ref, wbr_ref, wout_ref, g2_ref,
                      wup_ref, wdown_ref, gf_ref, o_ref, *, final):
    merged = None
    for n, y_ref in enumerate((ya_ref, yb_ref, yc_ref)):
        proj = _dot(y_ref[...], wbr_ref[n])
        term = _sigmoid(zg_ref[:, n * D_MODEL:(n + 1) * D_MODEL]) * proj
        merged = term if merged is None else merged + term
    x = x_ref[...] + _dot(merged, wout_ref[...])
    h = x * lax.rsqrt(jnp.mean(x * x, axis=-1, keepdims=True) + EPS) * g2_ref[...]
    h = h.astype(BF16)
    acc = x
    for j in range(0, D_FF, FF_CHUNK):
        up = jnp.dot(h, wup_ref[:, j:j + FF_CHUNK], preferred_element_type=F32)
        acc = acc + _dot(jnp.square(jnp.maximum(up, 0.0)), wdown_ref[j:j + FF_CHUNK, :])
    if final:
        acc = acc * lax.rsqrt(jnp.mean(acc * acc, axis=-1, keepdims=True) + EPS) * gf_ref[...]
    o_ref[...] = acc


def _merge_ffn(x, ya, yb, yc, zg, wbr, wout, g2, wup, wdown, gf, tm, final, layer):
    m = x.shape[0]
    spec = functools.partial(_param_spec, layer=layer)
    tile = lambda wd: pl.BlockSpec((tm, wd), lambda i: (i, 0))
    return pl.pallas_call(
        functools.partial(_merge_ffn_kernel, final=final),
        grid=(m // tm,),
        in_specs=[tile(D_MODEL), tile(A_V_W), tile(B_W), tile(C_WIDTH), tile(ZG_W),
                  spec((N_BRANCH, A_V_W, D_MODEL)), spec((D_MODEL, D_MODEL)),
                  spec((1, D_MODEL)), spec((D_MODEL, D_FF)),
                  spec((D_FF, D_MODEL)), _param_spec((1, D_MODEL), 0)],
        out_specs=tile(D_MODEL),
        out_shape=jax.ShapeDtypeStruct((m, D_MODEL), F32),
        compiler_params=pltpu.CompilerParams(dimension_semantics=("arbitrary",),
                                             vmem_limit_bytes=VMEM_LIMIT),
        name="merge_ffn",
    )(x, ya, yb, yc, zg, wbr, wout, g2, wup, wdown, gf)


def _block_diag(w):
    g, n, _ = w.shape
    eye = jnp.eye(g, dtype=w.dtype)
    return (eye[:, None, :, None] * w[:, :, None, :]).reshape(g * n, g * n)


def _prepare_params(p):
    depth = p['w_in'].shape[0]
    row = lambda a: a.reshape(depth, 1, -1).astype(F32)
    w_in = p['w_in'].astype(BF16)
    lane_row = lambda a: jnp.zeros((depth, 1, LANE), F32).at[:, 0, A_HEADS:2 * A_HEADS].set(a)
    zeros_lora = jnp.zeros((depth, B_LORA_WA // 2, B_W), F32)
    block_diag = jax.vmap(_block_diag)
    gdn = (p['a_conv_w'], lane_row(p['a_A_log']), lane_row(p['a_dt_bias']), row(p['a_norm_g']))
    rwkv = (row(p['b_mu']), row(p['b_w0']),
            jnp.concatenate([p['b_w_up'], zeros_lora], axis=1).astype(BF16),
            row(p['b_a0']),
            jnp.concatenate([zeros_lora, p['b_a_up']], axis=1).astype(BF16),
            p['b_g_up'].astype(BF16),
            row(p['b_k_k']), row(p['b_k_a']), row(p['b_r_k']), row(p['b_ln_w']), row(p['b_ln_b']),
            jnp.kron(jnp.eye(B_HEADS, dtype=F32), jnp.ones((B_N, B_N), F32)).astype(BF16)[None])
    rglru = (p['c_conv_w'], row(p['c_conv_b']),
             jnp.concatenate([block_diag(p['c_wa']), block_diag(p['c_wx'])], axis=2).astype(BF16),
             jnp.concatenate([row(p['c_ba']), row(p['c_bx'])], axis=2), row(p['c_L']))
    ffn = (p['w_branch'].astype(BF16), p['w_out'].astype(BF16), row(p['norm2_g']),
           p['w_up'].astype(BF16), p['w_down'].astype(BF16))
    return dict(norm1_g=row(p['norm1_g']), w_in=w_in, gdn=gdn, rwkv=rwkv, rglru=rglru, ffn=ffn)


PROMPT_BLOCKING = dict(c=64, nc=8, ns=1, gdn_group=4, rwkv_group=4, in_rows=256)
SAMPLE_BLOCKING = dict(c=SUBLANE, nc=1, ns=8, gdn_group=8, rwkv_group=4, in_rows=128)
FFN_TILE = 512


def _layer(x, states, new_states, lp, gf, *, layer, state_layer, bsz, t_pad, t_valid, blocking, final):
    a_s, a_conv, b_s, b_shift, c_h, c_conv = states
    prev = (lambda *idx: None) if new_states is None else (lambda *idx: tuple(new_states[i] for i in idx))
    kw = dict(bsz=bsz, t_pad=t_pad, t_valid=t_valid, layer=layer, state_layer=state_layer)
    chunked = dict(c=blocking['c'], nc=blocking['nc'], ns=blocking['ns'], **kw)
    a_convw, *gdn_prm = lp['gdn']
    za, zb, yc, zg, c_h, c_conv, a_conv = _in_proj(
        x, lp['norm1_g'], lp['w_in'], c_conv, c_h.reshape(-1, bsz, 1, C_WIDTH), a_conv, lp['rglru'], a_convw,
        tm=blocking['in_rows'], prev=prev(4, 5, 1), **kw)
    ya, a_s = _gdn(za, a_s, *gdn_prm, group=blocking['gdn_group'], prev=prev(0), **chunked)
    yb, b_s, b_shift = _rwkv(zb, b_shift, b_s, lp['rwkv'], group=blocking['rwkv_group'], prev=prev(2, 3),
                             **chunked)
    x = _merge_ffn(x, ya, yb, yc, zg, *lp['ffn'], gf, min(FFN_TILE, x.shape[0]), final, layer)
    return x, (a_s, a_conv, b_s, b_shift, c_h, c_conv)


def kernel(x_prompt, x_sample, state_a_S, state_a_conv, state_b_S, state_b_shift, state_c_h, state_c_conv, norm1_g, w_in, a_conv_w, a_A_log, a_dt_bias, a_norm_g, b_mu, b_w0, b_w_up, b_a0, b_a_up, b_g_up, b_k_k, b_k_a, b_r_k, b_ln_w, b_ln_b, c_conv_w, c_conv_b, c_wa, c_ba, c_wx, c_bx, c_L, w_branch, w_out, norm2_g, w_up, w_down, final_norm_g):
    p = dict(norm1_g=norm1_g, w_in=w_in, a_conv_w=a_conv_w, a_A_log=a_A_log, a_dt_bias=a_dt_bias,
             a_norm_g=a_norm_g, b_mu=b_mu, b_w0=b_w0, b_w_up=b_w_up, b_a0=b_a0, b_a_up=b_a_up,
             b_g_up=b_g_up, b_k_k=b_k_k, b_k_a=b_k_a, b_r_k=b_r_k, b_ln_w=b_ln_w, b_ln_b=b_ln_b,
             c_conv_w=c_conv_w, c_conv_b=c_conv_b, c_wa=c_wa, c_ba=c_ba, c_wx=c_wx, c_bx=c_bx,
             c_L=c_L, w_branch=w_branch, w_out=w_out, norm2_g=norm2_g, w_up=w_up, w_down=w_down)
    bp, tp, _ = x_prompt.shape
    bs, ts, _ = x_sample.shape
    ts_pad = -(-ts // SUBLANE) * SUBLANE
    assert ts_pad == SAMPLE_BLOCKING['c'] and tp % (PROMPT_BLOCKING['c'] * PROMPT_BLOCKING['nc']) == 0
    gf = final_norm_g.reshape(1, 1, D_MODEL)

    xp = x_prompt.reshape(bp * tp, D_MODEL)
    xs = jnp.pad(x_sample, ((0, 0), (0, ts_pad - ts), (0, 0))).reshape(bs * ts_pad, D_MODEL)
    zero = lambda *shape: jnp.zeros((1,) + shape, F32)
    p_states = (zero(bp, A_HEADS, A_DK, A_DV), zero(bp, 3, A_CONV_CH),
                zero(bp, B_HEADS, B_N, B_N), zero(bp, 1, B_IN),
                zero(bp, C_WIDTH), zero(bp, 3, C_WIDTH))
    s_states = (state_a_S, state_a_conv, state_b_S, state_b_shift, state_c_h, state_c_conv)
    lp = _prepare_params(p)
    p_new = s_new = None
    for l in range(DEPTH):
        final = l == DEPTH - 1
        xp, p_new = _layer(xp, p_states, p_new, lp, gf, layer=l, state_layer=0, bsz=bp, t_pad=tp,
                           t_valid=tp, blocking=PROMPT_BLOCKING, final=final)
        xs, s_new = _layer(xs, s_states, s_new, lp, gf, layer=l, state_layer=l, bsz=bs, t_pad=ts_pad,
                           t_valid=ts, blocking=SAMPLE_BLOCKING, final=final)
    squeeze_h = lambda st: st[:4] + (st[4].reshape(DEPTH, -1, C_WIDTH), st[5])
    y_prompt = xp.reshape(bp, tp, D_MODEL)
    y_sample = xs.reshape(bs, ts_pad, D_MODEL)[:, :ts]
    return (y_prompt, y_sample) + squeeze_h(p_new) + squeeze_h(s_new)
```

```python
import functools

import jax
import jax.numpy as jnp
from jax import lax
from jax.experimental import pallas as pl
from jax.experimental.pallas import tpu as pltpu

F32 = jnp.float32
BF16 = jnp.bfloat16

EPS = 1e-6
D_MODEL = 1024
DEPTH = 2
A_HEADS = 4
A_DK = 128
A_DV = 128
A_QK_W = A_HEADS * A_DK
A_V_W = A_HEADS * A_DV
A_CONV_CH = 2 * A_QK_W + A_V_W
B_N = 64
B_HEADS = 8
B_W = B_HEADS * B_N
B_LORA_WA = 128
B_GATE_LORA = 128
B_IN = 3 * B_W + B_LORA_WA + B_GATE_LORA
B_LN_EPS = 64e-5
DECAY_SCALE = 0.6065306597126334
C_WIDTH = 512
C_BLOCKS = 8
C_POW = 8.0
N_BRANCH = 3
D_FF = 4 * D_MODEL

LANE = 128
SUBLANE = 8
CONV_TAPS = 4
CONV_PAD = SUBLANE

ZA_W = A_CONV_CH + A_V_W + LANE
ZA_BA = A_CONV_CH + A_V_W
ZB_W = B_IN
ZC_W = 2 * C_WIDTH
ZG_W = N_BRANCH * D_MODEL
Z_WIDTHS = (ZA_W, ZB_W, ZC_W, ZG_W)
A_IN = A_CONV_CH + A_V_W + 2 * A_HEADS
N_IN = A_IN + ZB_W + ZC_W + ZG_W

VMEM_LIMIT = 60 * 1024 * 1024
PROJ_PIECE = 256
RGLRU_PIECE = 128


def _dot(a, b):
    return jnp.dot(a.astype(BF16), b.astype(BF16), preferred_element_type=F32)


def _dot_nt(a, b):
    return lax.dot_general(a.astype(BF16), b.astype(BF16), (((1,), (1,)), ((), ())),
                           preferred_element_type=F32)


def _dot_tn(a, b):
    return lax.dot_general(a.astype(BF16), b.astype(BF16), (((0,), (0,)), ((), ())),
                           preferred_element_type=F32)


def _sigmoid(x):
    return 1.0 / (1.0 + jnp.exp(-x))


def _silu(x):
    return x * _sigmoid(x)


def _softplus(x):
    return jnp.maximum(x, 0.0) + jnp.log1p(jnp.exp(-jnp.abs(x)))


def _gelu_tanh(x):
    return 0.5 * x * (1.0 + jnp.tanh(0.7978845608028654 * (x + 0.044715 * (x * x * x))))


def _rows(shape):
    return lax.broadcasted_iota(jnp.int32, shape, 0)


def _cols(shape):
    return lax.broadcasted_iota(jnp.int32, shape, 1)


def _rows_in(shape, period):
    r = _rows(shape)
    return r if period >= shape[0] else r & (period - 1)


def _shift_rows(x, s, fill, period):
    return jnp.where(_rows_in(x.shape, period) >= s, pltpu.roll(x, s, 0), fill)


def _bf16_pieces(x, n):
    pieces = []
    for _ in range(n):
        p = x.astype(BF16)
        pieces.append(p)
        x = x - p.astype(F32)
    return pieces


def _cumsum_chunks(x, c):
    tri = (_rows((c, c)) >= _cols((c, c))).astype(BF16)
    pieces = _bf16_pieces(x, 3)
    chunks = [sum(jnp.dot(tri, p[r0:r0 + c], preferred_element_type=F32) for p in pieces)
              for r0 in range(0, x.shape[0], c)]
    return chunks[0] if len(chunks) == 1 else jnp.concatenate(chunks, axis=0)


def _unit_lower_inverses(ns):
    c = ns[0].shape[0]
    eye = (_rows(ns[0].shape) == _cols(ns[0].shape)).astype(F32)
    ts = [eye + n for n in ns]
    ps = list(ns)
    s = 2
    while s < c:
        ps = [_dot(p, p) for p in ps]
        ts = [t + _dot(t, p) for t, p in zip(ts, ps)]
        s *= 2
    return ts


def _unit_lower_solves(ns, xs):
    c = ns[0].shape[0]
    xs = [x + _dot(n, x) for n, x in zip(ns, xs)]
    ps = list(ns)
    s = 2
    while s < c:
        ps = [_dot(p, p) for p in ps]
        xs = [x + _dot(p, x) for p, x in zip(ps, xs)]
        s *= 2
    return xs


def _param_spec(shape, layer):
    nd = len(shape)
    return pl.BlockSpec((None,) + shape, lambda *_: (layer,) + (0,) * nd, pipeline_mode=pl.Buffered(1))


def _group_of_2d(b, t):
    return b


def _stacked_spec(layers, shape, group_of=_group_of_2d):
    nd = len(shape) - 1
    return pl.BlockSpec((layers,) + shape, lambda *g: (0, group_of(*g)) + (0,) * nd)


def _emit_state(last, out_ref, prev_ref, new_ref):
    @pl.when(last)
    def _():
        n_prev = out_ref.shape[0] - 1
        if n_prev:
            out_ref[0:n_prev] = prev_ref[...]
        out_ref[n_prev] = new_ref[...]


def _state_spec(shape, layer, group_of=_group_of_2d):
    nd = len(shape) - 1
    return pl.BlockSpec((None,) + shape, lambda *g: (layer, group_of(*g)) + (0,) * nd)


def _in_proj_kernel(prev_h_ref, prev_conv_ref, prev_aconv_ref, x_ref, g_ref, w_ref, conv0_ref, h0_ref,
                    aconv0_ref, convw_ref, convb_ref, wg_ref, bg_ref, l_ref, aconvw_ref,
                    za_ref, zb_ref, yc_ref, zg_ref, hout_ref, convout_ref, aconvout_ref,
                    w_rest, xpad, h_scr, xpad_a, *, l, ns, nt, valid_rows):
    step = pl.program_id(0)
    t = step % nt
    first = t == 0
    last = t == nt - 1
    @pl.when(step == 0)
    def _():
        w_rest[...] = w_ref[:, A_IN:]

    x = x_ref[...]
    h = x * lax.rsqrt(jnp.mean(x * x, axis=-1, keepdims=True) + EPS) * g_ref[...]
    h = h.astype(BF16)
    proj = lambda ref, off, width: jnp.dot(h, ref[:, off:off + width], preferred_element_type=F32)

    @pl.when(first)
    def _():
        xpad_a[:, CONV_PAD - CONV_TAPS + 1:CONV_PAD, :] = aconv0_ref[...]

    @pl.when(first)
    def _():
        h_scr[...] = h0_ref[...]
        xpad[:, CONV_PAD - CONV_TAPS + 1:CONV_PAD, :] = conv0_ref[...]

    def causal_conv(z, pad, w, cols):
        y = z * w[CONV_TAPS - 1:CONV_TAPS, cols]
        for q in range(ns):
            pad[q, CONV_PAD:CONV_PAD + l, cols] = z[q * l:(q + 1) * l]
        for j in range(CONV_TAPS - 1):
            lo_row = CONV_PAD - CONV_TAPS + 1 + j
            parts = [pad[q, lo_row:lo_row + l, cols] for q in range(ns)]
            view = parts[0] if ns == 1 else jnp.concatenate(parts, axis=0)
            y = y + view * w[j:j + 1, cols]
        return y

    def conv_piece(lo):
        cols = slice(lo, lo + PROJ_PIECE)
        za_ref[:, cols] = _silu(causal_conv(proj(w_ref, lo, PROJ_PIECE), xpad_a, aconvw_ref, cols))

    def gate_piece():
        za_ref[:, A_CONV_CH:] = proj(w_ref, A_CONV_CH, ZA_W - A_CONV_CH)

    def rglru_piece(lo):
        cols = slice(lo, lo + RGLRU_PIECE)
        xc = (causal_conv(proj(w_rest, ZB_W + lo, RGLRU_PIECE), xpad, convw_ref, cols) + convb_ref[:, cols])
        gate_cols = lambda off: slice(off + lo, off + lo + RGLRU_PIECE)
        r = _sigmoid(_dot(xc, wg_ref[cols, gate_cols(0)]) + bg_ref[:, gate_cols(0)])
        i = _sigmoid(_dot(xc, wg_ref[cols, gate_cols(C_WIDTH)]) + bg_ref[:, gate_cols(C_WIDTH)])
        log_a = -C_POW * r * _softplus(-l_ref[:, cols])
        a = jnp.exp(log_a)
        var = -jnp.tanh(log_a) * (a * a + 1.0)
        b = jnp.where(var > 0.0, var * lax.rsqrt(var), 0.0) * (i * xc)
        if valid_rows < l:
            valid = _rows_in(a.shape, l) < valid_rows
            a = jnp.where(valid, a, 1.0)
            b = jnp.where(valid, b, 0.0)
        s = 1
        while s < SUBLANE:
            b = a * _shift_rows(b, s, 0.0, SUBLANE) + b
            a = a * _shift_rows(a, s, 1.0, SUBLANE)
            s *= 2
        gate = _gelu_tanh(proj(w_rest, ZB_W + C_WIDTH + lo, RGLRU_PIECE))
        for q in range(ns):
            carry = h_scr[q, :, cols]
            for r0 in range(q * l, (q + 1) * l, SUBLANE):
                rs = slice(r0, r0 + SUBLANE)
                hseq = a[rs] * carry + b[rs]
                carry = hseq[SUBLANE - 1:SUBLANE, :]
                yc_ref[rs, cols] = hseq * gate[rs]
            h_scr[q, :, cols] = carry

    def project(out_ref, w, off, width):
        def piece(lo):
            hi = min(lo + PROJ_PIECE, width)
            out_ref[:, lo:hi] = proj(w, off + lo, hi - lo)
        return [functools.partial(piece, lo) for lo in range(0, width, PROJ_PIECE)]

    assert A_CONV_CH % PROJ_PIECE == 0 and C_WIDTH % RGLRU_PIECE == 0
    busy = ([functools.partial(rglru_piece, lo) for lo in range(0, C_WIDTH, RGLRU_PIECE)]
            + [functools.partial(conv_piece, lo) for lo in range(0, A_CONV_CH, PROJ_PIECE)])
    plain = ([gate_piece] + project(zb_ref, w_rest, 0, ZB_W) + project(zg_ref, w_rest, ZB_W + ZC_W, ZG_W))
    per_busy = -(-len(plain) // len(busy))
    for k, busy_piece in enumerate(busy):
        busy_piece()
        for plain_piece in plain[k * per_busy:(k + 1) * per_busy]:
            plain_piece()
    for plain_piece in plain[len(busy) * per_busy:]:
        plain_piece()
    _carry_history(xpad, l, valid_rows, last, convout_ref, prev_conv_ref, CONV_TAPS - 1)
    _emit_state(last, hout_ref, prev_h_ref, h_scr)
    _carry_history(xpad_a, l, valid_rows, last, aconvout_ref, prev_aconv_ref, CONV_TAPS - 1)


def _in_proj(x, g, w, conv0, h0, aconv0, prm, aconvw, *, tm, bsz, t_pad, t_valid, layer, state_layer, prev):
    m = x.shape[0]
    l = min(tm, t_pad)
    ns = tm // l
    _, _, valid_rows = _mixer_grid(bsz, t_pad, t_valid, l, ns)
    nt = t_pad // l
    group_of = lambda i: i // nt
    spec = functools.partial(_param_spec, layer=layer)
    kern = functools.partial(_in_proj_kernel, l=l, ns=ns, nt=nt, valid_rows=valid_rows)
    state_shapes = ((1, C_WIDTH), (3, C_WIDTH), (3, A_CONV_CH))
    kern, prev_args, prev_specs, n_prev = _with_prev(kern, prev, state_shapes, ns, group_of)
    tile = lambda wd: pl.BlockSpec((tm, wd), lambda i: (i, 0))
    widths = (ZA_W, ZB_W, C_WIDTH, ZG_W)
    return pl.pallas_call(
        kern,
        grid=(m // tm,),
        in_specs=prev_specs + [
                  tile(D_MODEL), spec((1, D_MODEL)), spec((D_MODEL, N_IN)),
                  _state_spec((ns, 3, C_WIDTH), state_layer, group_of),
                  _state_spec((ns, 1, C_WIDTH), state_layer, group_of),
                  _state_spec((ns, 3, A_CONV_CH), state_layer, group_of),
                  spec((CONV_TAPS, C_WIDTH)), spec((1, C_WIDTH)),
                  spec((C_WIDTH, 2 * C_WIDTH)), spec((1, 2 * C_WIDTH)),
                  spec((1, C_WIDTH)), spec((CONV_TAPS, A_CONV_CH))],
        out_specs=[tile(wd) for wd in widths]
        + [_stacked_spec(n_prev + 1, (ns,) + shp, group_of) for shp in state_shapes],
        out_shape=[jax.ShapeDtypeStruct((m, wd), F32) for wd in widths]
        + [jax.ShapeDtypeStruct((n_prev + 1, bsz) + shp, F32) for shp in state_shapes],
        scratch_shapes=[pltpu.VMEM((D_MODEL, N_IN - A_IN), BF16),
                        pltpu.VMEM((ns, CONV_PAD + l, C_WIDTH), F32),
                        pltpu.VMEM((ns, 1, C_WIDTH), F32),
                        pltpu.VMEM((ns, CONV_PAD + l, A_CONV_CH), F32)],
        compiler_params=pltpu.CompilerParams(dimension_semantics=("arbitrary",),
                                             vmem_limit_bytes=VMEM_LIMIT),
        name="in_proj",
    )(*prev_args, x, g, w, conv0, h0, aconv0, *prm, aconvw)


def _with_history(x, xpad, first, hist_ref, ns, l, depth):
    @pl.when(first)
    def _():
        xpad[:, CONV_PAD - depth:CONV_PAD, :] = hist_ref[...]

    for s in range(ns):
        xpad[s, CONV_PAD:CONV_PAD + l, :] = x[s * l:(s + 1) * l]
    views = []
    for j in range(depth, 0, -1):
        parts = [xpad[s, CONV_PAD - j:CONV_PAD - j + l, :] for s in range(ns)]
        views.append(parts[0] if ns == 1 else jnp.concatenate(parts, axis=0))
    return views


def _carry_history(xpad, l, valid_rows, last, out_ref, prev_ref, depth):
    _emit_state(last, out_ref, prev_ref,
                xpad.at[:, CONV_PAD - depth + valid_rows:CONV_PAD + valid_rows, :])

    nxt = xpad[:, CONV_PAD - depth + l:CONV_PAD + l, :]
    xpad[:, CONV_PAD - depth:CONV_PAD, :] = nxt


def _gdn_kernel(prev_s_ref, za_ref, s0_ref, alog_ref, dt_ref, ng_ref,
                y_ref, sout_ref, s_scr, *, c, nc, ns, nt, valid_rows, group):
    t = pl.program_id(1)
    first = t == 0
    last = t == nt - 1
    l = nc * c
    rows = ns * l

    @pl.when(first)
    def _():
        s_scr[...] = s0_ref[...]

    qkv = za_ref[:, 0:A_CONV_CH]

    ba = za_ref[:, ZA_BA:ZA_BA + LANE]
    ba = jnp.where(_cols(ba.shape) < 2 * A_HEADS, ba, 0.0)
    beta_all = _sigmoid(ba)
    g_all = -jnp.exp(alog_ref[...]) * _softplus(ba + dt_ref[...])
    masked = valid_rows < l
    if masked:
        valid = _rows_in((rows, LANE), l) < valid_rows
        beta_all = jnp.where(valid, beta_all, 0.0)
        g_all = jnp.where(valid, g_all, 0.0)
    gc_all = g_all
    s = 1
    while s < c:
        gc_all = gc_all + _shift_rows(gc_all, s, 0.0, c)
        s *= 2

    rr = _rows((c, c))
    cc = _cols((c, c))
    causal = rr >= cc
    strict = rr > cc
    ng = ng_ref[...]
    pick = (_cols((SUBLANE * A_HEADS, LANE))
            == A_HEADS + _rows((SUBLANE * A_HEADS, LANE)) // SUBLANE).astype(BF16)

    all_chunks = [(s, n) for n in range(nc) for s in range(ns)]
    heads = range(A_HEADS)
    state = {(s, h): s_scr[s, h] for s in range(ns) for h in heads}

    def advance(chunks):
        inst = [(s, n, h) for (s, n) in chunks for h in heads]
        lower, attn, rhs2, kdec, qg, glast = {}, {}, {}, {}, {}, {}
        for (s, n) in chunks:
            r0 = s * l + n * c
            rs = slice(r0, r0 + c)
            gc_rows = sum(lax.dot_general(pick, piece, (((1,), (1,)), ((), ())), preferred_element_type=F32)
                          for piece in _bf16_pieces(gc_all[rs], 3))
            for h in heads:
                q = qkv[rs, h * A_DK:(h + 1) * A_DK]
                k = qkv[rs, A_QK_W + h * A_DK:A_QK_W + (h + 1) * A_DK]
                v = qkv[rs, 2 * A_QK_W + h * A_DV:2 * A_QK_W + (h + 1) * A_DV]
                q = q * lax.rsqrt(jnp.sum(q * q, axis=-1, keepdims=True) + EPS) * (A_DK ** -0.5)
                k = k * lax.rsqrt(jnp.sum(k * k, axis=-1, keepdims=True) + EPS)
                if masked:
                    valid_c = _rows((c, A_DK)) < valid_rows - n * c
                    k = jnp.where(valid_c, k, 0.0)
                    v = jnp.where(valid_c, v, 0.0)
                beta = beta_all[rs, h:h + 1]
                gc = gc_all[rs, A_HEADS + h:A_HEADS + h + 1]
                gc_row = gc_rows[SUBLANE * h:SUBLANE * h + 1, :]
                dec = jnp.where(causal, jnp.exp(jnp.where(causal, gc - gc_row, 0.0)), 0.0)
                kb = k * beta
                egc = jnp.exp(gc)
                sc = _dot_nt(jnp.concatenate([kb, q], axis=0), k)
                key = (s, n, h)
                lower[key] = jnp.where(strict, -sc[:c] * dec, 0.0)
                attn[key] = sc[c:] * dec
                rhs2[key] = jnp.concatenate([v * beta, kb * egc], axis=1)
                gc_last = gc[c - 1:c, :]
                kdec[key] = k * jnp.exp(gc_last - gc)
                glast[key] = jnp.exp(gc_last)
                qg[key] = q * egc
        tinv = dict(zip(inst, _unit_lower_inverses([lower[i] for i in inst])))
        uw = {i: _dot(tinv[i], rhs2[i]) for i in inst}
        ao = {i: _dot(attn[i], uw[i]) for i in inst}
        bm = {i: _dot_tn(kdec[i], uw[i]) for i in inst}
        for (s, n) in chunks:
            r0 = s * l + n * c
            for h in heads:
                i = (s, n, h)
                st = state[(s, h)]
                o = _dot(qg[i] - ao[i][:, A_DV:], st) + ao[i][:, :A_DV]
                state[(s, h)] = st * glast[i] - _dot(bm[i][:, A_DV:], st) + bm[i][:, :A_DV]
                o = o * lax.rsqrt(jnp.mean(o * o, axis=-1, keepdims=True) + EPS)
                z = za_ref[r0:r0 + c, A_CONV_CH + h * A_DV:A_CONV_CH + (h + 1) * A_DV]
                y_ref[r0:r0 + c, h * A_DV:(h + 1) * A_DV] = o * ng * _silu(z)

    for g in range(0, len(all_chunks), group):
        advance(all_chunks[g:g + group])
    for s in range(ns):
        for h in heads:
            s_scr[s, h] = state[(s, h)]

    _emit_state(last, sout_ref, prev_s_ref, s_scr)


def _with_prev(kern, prev, shapes, ns, group_of=_group_of_2d):
    if prev is None:
        return functools.partial(kern, *([None] * len(shapes))), [], [], 0
    n_prev = prev[0].shape[0]
    return kern, list(prev), [_stacked_spec(n_prev, (ns,) + shp, group_of) for shp in shapes], n_prev


def _mixer_grid(bsz, t_pad, t_valid, l, ns):
    assert t_pad % l == 0 and bsz % ns == 0 and (ns == 1 or t_pad == l)
    nt = t_pad // l
    valid_rows = t_valid - (nt - 1) * l
    assert 0 < valid_rows <= l and (nt == 1 or valid_rows == l)
    return (bsz // ns, nt), ns * l, valid_rows


def _gdn(za, s0, alog_row, dt_row, ng, *, bsz, t_pad, t_valid, c, nc, ns, group,
         layer, state_layer, prev):
    grid, rows, valid_rows = _mixer_grid(bsz, t_pad, t_valid, nc * c, ns)
    spec = functools.partial(_param_spec, layer=layer)
    nt = grid[1]
    kern = functools.partial(_gdn_kernel, c=c, nc=nc, ns=ns, nt=nt, valid_rows=valid_rows,
                             group=group)
    state_shapes = ((A_HEADS, A_DK, A_DV),)
    kern, prev_args, prev_specs, n_prev = _with_prev(kern, prev, state_shapes, ns)
    return pl.pallas_call(
        kern,
        grid=grid,
        in_specs=prev_specs + [
                  pl.BlockSpec((rows, ZA_W), lambda b, t: (b * nt + t, 0)),
                  _state_spec((ns, A_HEADS, A_DK, A_DV), state_layer),
                  spec((1, LANE)),
                  spec((1, LANE)),
                  spec((1, A_DV))],
        out_specs=[pl.BlockSpec((rows, A_V_W), lambda b, t: (b * nt + t, 0)),
                   ] + [_stacked_spec(n_prev + 1, (ns,) + shp) for shp in state_shapes],
        out_shape=[jax.ShapeDtypeStruct((bsz * t_pad, A_V_W), F32)]
        + [jax.ShapeDtypeStruct((n_prev + 1, bsz) + shp, F32) for shp in state_shapes],
        scratch_shapes=[pltpu.VMEM((ns, A_HEADS, A_DK, A_DV), F32)],
        compiler_params=pltpu.CompilerParams(dimension_semantics=("arbitrary", "arbitrary"),
                                             vmem_limit_bytes=VMEM_LIMIT),
        name="gdn",
    )(*prev_args, za, s0, alog_row, dt_row, ng)


def _rwkv_kernel(prev_s_ref, prev_shift_ref, zb_ref, shift0_ref, s0_ref, mu_ref, w0_ref, wup_ref, a0_ref,
                 aup_ref, gup_ref, kk_ref, ka_ref, rk_ref, lnw_ref, lnb_ref, hsum_ref,
                 y_ref, sout_ref, shiftout_ref, xpad, s_scr, prep, *, c, nc, ns, nt, valid_rows, group):
    t = pl.program_id(1)
    first = t == 0
    last = t == nt - 1
    l = nc * c
    rows = ns * l

    @pl.when(first)
    def _():
        s_scr[...] = s0_ref[...]

    zb = zb_ref[...]
    (prev,) = _with_history(zb, xpad, first, shift0_ref, ns, l, 1)
    _carry_history(xpad, l, valid_rows, last, shiftout_ref, prev_shift_ref, 1)
    zs = zb + (prev - zb) * mu_ref[...]

    r_all = zs[:, 0:B_W]
    k_all = zs[:, B_W:2 * B_W]
    v_all = zs[:, 2 * B_W:3 * B_W]
    xwa = zs[:, 3 * B_W:3 * B_W + B_LORA_WA]
    xg = zs[:, 3 * B_W + B_LORA_WA:B_IN]
    lw = -DECAY_SCALE * _sigmoid(w0_ref[...] + _dot(jnp.tanh(xwa), wup_ref[...]))
    a_all = _sigmoid(a0_ref[...] + _dot(xwa, aup_ref[...]))
    g_all = _dot(_sigmoid(xg), gup_ref[...])
    kk_all = k_all * kk_ref[...]
    k_all = k_all * (1.0 + (a_all - 1.0) * ka_ref[...])
    masked = valid_rows < l
    if masked:
        valid = _rows_in((rows, B_W), l) < valid_rows
        lw = jnp.where(valid, lw, 0.0)
        kk_all = jnp.where(valid, kk_all, 0.0)
        k_all = jnp.where(valid, k_all, 0.0)
        v_all = jnp.where(valid, v_all, 0.0)
    gc_all = _cumsum_chunks(lw, c)

    def head_sums(x, pieces=1):
        return sum(jnp.dot(p, hsum_ref[...], preferred_element_type=F32) for p in _bf16_pieces(x, pieces))

    kkn_all = kk_all * lax.rsqrt(head_sums(kk_all * kk_all, pieces=2) + EPS)
    bonus_all = head_sums(r_all * k_all * rk_ref[...]) * v_all

    rr = _rows((2 * c, 2 * c))
    cc = _cols((2 * c, 2 * c))
    tt = jnp.where(rr >= c, rr - c, rr)
    ss = jnp.where(cc >= c, cc - c, cc)
    mask2 = jnp.logical_or(tt > ss, jnp.logical_and(rr >= c, tt == ss))

    prep_names = ('r', 'k', 'v', 'kkn', 'a', 'lw', 'gc', 'bonus', 'g')
    for idx, val in enumerate((r_all, k_all, v_all, kkn_all, a_all, lw, gc_all, bonus_all, g_all)):
        prep[idx] = val
    heads = range(B_HEADS)
    sls = [slice(h * B_N, (h + 1) * B_N) for h in heads]

    def at_rows(name, rs):
        return prep[prep_names.index(name), rs, :]

    def advance(g, carry):
        members = range(group)
        inst = [(j, h) for j in members for h in heads]
        amat, rhs_uw, r_abs, tails, e_last, vs, rows_of, seq_of = {}, {}, {}, {}, {}, {}, {}, {}
        for j in members:
            q = g * group + j
            rs = pl.ds(pl.multiple_of(q * c, c), c)
            rows_of[j] = rs
            seq_of[j] = 0 if ns == 1 else q
            gc = at_rows('gc', rs)
            lwc = at_rows('lw', rs)
            gmid = gc[c // 2:c // 2 + 1, :]
            glast = gc[c - 1:c, :]
            e_last[j] = jnp.exp(glast)
            r = at_rows('r', rs)
            k = at_rows('k', rs)
            v = at_rows('v', rs)
            alpha = -at_rows('kkn', rs)
            bet = at_rows('kkn', rs) * at_rows('a', rs)
            e_inv = jnp.exp(gmid - gc)
            e_tail = jnp.exp(glast - gc)
            lhs_a = alpha * jnp.exp(gc - lwc - gmid)
            lhs_r = r * jnp.exp(gc - gmid)
            rhs_b = bet * e_inv
            rhs_k = k * e_inv
            abs_a = alpha * jnp.exp(gc - lwc)
            abs_r = r * jnp.exp(gc)
            tail_b = bet * e_tail
            tail_k = k * e_tail
            for h in heads:
                sl = sls[h]
                lhs = jnp.concatenate([lhs_a[:, sl], lhs_r[:, sl]], axis=0)
                rhs = jnp.concatenate([rhs_b[:, sl], rhs_k[:, sl]], axis=0)
                key = (j, h)
                amat[key] = jnp.where(mask2, _dot_nt(lhs, rhs), 0.0)
                rhs_uw[key] = abs_a[:, sl]
                r_abs[key] = abs_r[:, sl]
                tails[key] = jnp.concatenate([tail_b[:, sl], tail_k[:, sl]], axis=0)
                vs[key] = v[:, sl]

        from_v = {i: _dot(amat[i][:, c:], vs[i]) for i in inst}
        uw = dict(zip(inst, _unit_lower_solves(
            [amat[i][:c, :c] for i in inst],
            [jnp.concatenate([from_v[i][:c], rhs_uw[i]], axis=1) for i in inst])))
        ro = {i: _dot(amat[i][c:, :c], uw[i]) for i in inst}
        mp = {i: _dot_tn(uw[i][:, B_N:], tails[i][:c]) for i in inst}
        bb = {i: _dot_tn(jnp.concatenate([uw[i][:, :B_N], vs[i]], axis=0), tails[i]) for i in inst}
        state = {}
        outs = []
        for j in members:
            per_head = []
            for h in heads:
                i = (j, h)
                skey = (0 if ns == 1 else j, h)
                st = state[skey] if skey in state else s_scr[seq_of[j], h]
                per_head.append(_dot_nt(r_abs[i] + ro[i][:, B_N:], st) + from_v[i][c:] + ro[i][:, :B_N])
                state[skey] = st * e_last[j][:, sls[h]] + _dot(st, mp[i]) + bb[i]
            outs.append(jnp.concatenate(per_head, axis=1))
        for j, o in zip(members, outs):
            d = o - head_sums(o) * (1.0 / B_N)
            var = head_sums(d * d) * (1.0 / B_N)
            o = d * lax.rsqrt(var + B_LN_EPS) * lnw_ref[...] + lnb_ref[...]
            y_ref[rows_of[j], :] = (o + at_rows('bonus', rows_of[j])) * at_rows('g', rows_of[j])
        for (sk, h), st in state.items():
            s_scr[seq_of[sk], h] = st
        return carry

    assert (ns * nc) % group == 0 and (ns == 1 or nc == 1)
    lax.fori_loop(0, ns * nc // group, advance, 0)

    _emit_state(last, sout_ref, prev_s_ref, s_scr)


def _rwkv(zb, shift0, s0, prm, *, bsz, t_pad, t_valid, c, nc, ns, group, layer, state_layer, prev):
    grid, rows, valid_rows = _mixer_grid(bsz, t_pad, t_valid, nc * c, ns)
    nt = grid[1]
    kern = functools.partial(_rwkv_kernel, c=c, nc=nc, ns=ns, nt=nt, valid_rows=valid_rows,
                             group=group)
    spec = functools.partial(_param_spec, layer=layer)
    row = lambda wd: spec((1, wd))
    state_shapes = ((B_HEADS, B_N, B_N), (1, B_IN))
    kern, prev_args, prev_specs, n_prev = _with_prev(kern, prev, state_shapes, ns)
    return pl.pallas_call(
        kern,
        grid=grid,
        in_specs=prev_specs + [
                  pl.BlockSpec((rows, ZB_W), lambda b, t: (b * nt + t, 0)),
                  _state_spec((ns, 1, B_IN), state_layer),
                  _state_spec((ns, B_HEADS, B_N, B_N), state_layer),
                  row(B_IN), row(B_W), spec((B_LORA_WA, B_W)), row(B_W),
                  spec((B_LORA_WA, B_W)), spec((B_GATE_LORA, B_W)),
                  row(B_W), row(B_W), row(B_W), row(B_W), row(B_W), _param_spec((B_W, B_W), 0)],
        out_specs=[pl.BlockSpec((rows, B_W), lambda b, t: (b * nt + t, 0)),
                   ] + [_stacked_spec(n_prev + 1, (ns,) + shp) for shp in state_shapes],
        out_shape=[jax.ShapeDtypeStruct((bsz * t_pad, B_W), F32)]
        + [jax.ShapeDtypeStruct((n_prev + 1, bsz) + shp, F32) for shp in state_shapes],
        scratch_shapes=[pltpu.VMEM((ns, CONV_PAD + nc * c, B_IN), F32),
                        pltpu.VMEM((ns, B_HEADS, B_N, B_N), F32),
                        pltpu.VMEM((9, rows, B_W), F32)],
        compiler_params=pltpu.CompilerParams(dimension_semantics=("arbitrary", "arbitrary"),
                                             vmem_limit_bytes=VMEM_LIMIT),
        name="rwkv",
    )(*prev_args, zb, shift0, s0, *prm)


FF_CHUNK = 1024


def _merge_ffn_kernel(x_ref, ya_ref, yb_ref, yc_ref, zg_ref, wbr_ref, wout_ref, g2_ref,
                      wup_ref, wdown_ref, gf_ref, o_ref, *, final):
    merged = None
    for n, y_ref in enumerate((ya_ref, yb_ref, yc_ref)):
        proj = _dot(y_ref[...], wbr_ref[n])
        term = _sigmoid(zg_ref[:, n * D_MODEL:(n + 1) * D_MODEL]) * proj
        merged = term if merged is None else merged + term
    x = x_ref[...] + _dot(merged, wout_ref[...])
    h = x * lax.rsqrt(jnp.mean(x * x, axis=-1, keepdims=True) + EPS) * g2_ref[...]
    h = h.astype(BF16)
    acc = x
    for j in range(0, D_FF, FF_CHUNK):
        up = jnp.dot(h, wup_ref[:, j:j + FF_CHUNK], preferred_element_type=F32)
        acc = acc + _dot(jnp.square(jnp.maximum(up, 0.0)), wdown_ref[j:j + FF_CHUNK, :])
    if final:
        acc = acc * lax.rsqrt(jnp.mean(acc * acc, axis=-1, keepdims=True) + EPS) * gf_ref[...]
    o_ref[...] = acc


def _merge_ffn(x, ya, yb, yc, zg, wbr, wout, g2, wup, wdown, gf, tm, final, layer):
    m = x.shape[0]
    spec = functools.partial(_param_spec, layer=layer)
    tile = lambda wd: pl.BlockSpec((tm, wd), lambda i: (i, 0))
    return pl.pallas_call(
        functools.partial(_merge_ffn_kernel, final=final),
        grid=(m // tm,),
        in_specs=[tile(D_MODEL), tile(A_V_W), tile(B_W), tile(C_WIDTH), tile(ZG_W),
                  spec((N_BRANCH, A_V_W, D_MODEL)), spec((D_MODEL, D_MODEL)),
                  spec((1, D_MODEL)), spec((D_MODEL, D_FF)),
                  spec((D_FF, D_MODEL)), _param_spec((1, D_MODEL), 0)],
        out_specs=tile(D_MODEL),
        out_shape=jax.ShapeDtypeStruct((m, D_MODEL), F32),
        compiler_params=pltpu.CompilerParams(dimension_semantics=("arbitrary",),
                                             vmem_limit_bytes=VMEM_LIMIT),
        name="merge_ffn",
    )(x, ya, yb, yc, zg, wbr, wout, g2, wup, wdown, gf)


def _block_diag(w):
    g, n, _ = w.shape
    eye = jnp.eye(g, dtype=w.dtype)
    return (eye[:, None, :, None] * w[:, :, None, :]).reshape(g * n, g * n)


def _prepare_params(p):
    depth = p['w_in'].shape[0]
    row = lambda a: a.reshape(depth, 1, -1).astype(F32)
    w_in = p['w_in'].astype(BF16)
    lane_row = lambda a: jnp.zeros((depth, 1, LANE), F32).at[:, 0, A_HEADS:2 * A_HEADS].set(a)
    zeros_lora = jnp.zeros((depth, B_LORA_WA // 2, B_W), F32)
    block_diag = jax.vmap(_block_diag)
    gdn = (p['a_conv_w'], lane_row(p['a_A_log']), lane_row(p['a_dt_bias']), row(p['a_norm_g']))
    rwkv = (row(p['b_mu']), row(p['b_w0']),
            jnp.concatenate([p['b_w_up'], zeros_lora], axis=1).astype(BF16),
            row(p['b_a0']),
            jnp.concatenate([zeros_lora, p['b_a_up']], axis=1).astype(BF16),
            p['b_g_up'].astype(BF16),
            row(p['b_k_k']), row(p['b_k_a']), row(p['b_r_k']), row(p['b_ln_w']), row(p['b_ln_b']),
            jnp.kron(jnp.eye(B_HEADS, dtype=F32), jnp.ones((B_N, B_N), F32)).astype(BF16)[None])
    rglru = (p['c_conv_w'], row(p['c_conv_b']),
             jnp.concatenate([block_diag(p['c_wa']), block_diag(p['c_wx'])], axis=2).astype(BF16),
             jnp.concatenate([row(p['c_ba']), row(p['c_bx'])], axis=2), row(p['c_L']))
    ffn = (p['w_branch'].astype(BF16), p['w_out'].astype(BF16), row(p['norm2_g']),
           p['w_up'].astype(BF16), p['w_down'].astype(BF16))
    return dict(norm1_g=row(p['norm1_g']), w_in=w_in, gdn=gdn, rwkv=rwkv, rglru=rglru, ffn=ffn)


PROMPT_BLOCKING = dict(c=64, nc=8, ns=1, gdn_group=4, rwkv_group=4, in_rows=256)
SAMPLE_BLOCKING = dict(c=SUBLANE, nc=1, ns=8, gdn_group=8, rwkv_group=4, in_rows=128)
FFN_TILE = 512


def _layer(x, states, new_states, lp, gf, *, layer, state_layer, bsz, t_pad, t_valid, blocking, final):
    a_s, a_conv, b_s, b_shift, c_h, c_conv = states
    prev = (lambda *idx: None) if new_states is None else (lambda *idx: tuple(new_states[i] for i in idx))
    kw = dict(bsz=bsz, t_pad=t_pad, t_valid=t_valid, layer=layer, state_layer=state_layer)
    chunked = dict(c=blocking['c'], nc=blocking['nc'], ns=blocking['ns'], **kw)
    a_convw, *gdn_prm = lp['gdn']
    za, zb, yc, zg, c_h, c_conv, a_conv = _in_proj(
        x, lp['norm1_g'], lp['w_in'], c_conv, c_h.reshape(-1, bsz, 1, C_WIDTH), a_conv, lp['rglru'], a_convw,
        tm=blocking['in_rows'], prev=prev(4, 5, 1), **kw)
    ya, a_s = _gdn(za, a_s, *gdn_prm, group=blocking['gdn_group'], prev=prev(0), **chunked)
    yb, b_s, b_shift = _rwkv(zb, b_shift, b_s, lp['rwkv'], group=blocking['rwkv_group'], prev=prev(2, 3),
                             **chunked)
    x = _merge_ffn(x, ya, yb, yc, zg, *lp['ffn'], gf, min(FFN_TILE, x.shape[0]), final, layer)
    return x, (a_s, a_conv, b_s, b_shift, c_h, c_conv)


def kernel(x_prompt, x_sample, state_a_S, state_a_conv, state_b_S, state_b_shift, state_c_h, state_c_conv, norm1_g, w_in, a_conv_w, a_A_log, a_dt_bias, a_norm_g, b_mu, b_w0, b_w_up, b_a0, b_a_up, b_g_up, b_k_k, b_k_a, b_r_k, b_ln_w, b_ln_b, c_conv_w, c_conv_b, c_wa, c_ba, c_wx, c_bx, c_L, w_branch, w_out, norm2_g, w_up, w_down, final_norm_g):
    p = dict(norm1_g=norm1_g, w_in=w_in, a_conv_w=a_conv_w, a_A_log=a_A_log, a_dt_bias=a_dt_bias,
             a_norm_g=a_norm_g, b_mu=b_mu, b_w0=b_w0, b_w_up=b_w_up, b_a0=b_a0, b_a_up=b_a_up,
             b_g_up=b_g_up, b_k_k=b_k_k, b_k_a=b_k_a, b_r_k=b_r_k, b_ln_w=b_ln_w, b_ln_b=b_ln_b,
             c_conv_w=c_conv_w, c_conv_b=c_conv_b, c_wa=c_wa, c_ba=c_ba, c_wx=c_wx, c_bx=c_bx,
             c_L=c_L, w_branch=w_branch, w_out=w_out, norm2_g=norm2_g, w_up=w_up, w_down=w_down)
    bp, tp, _ = x_prompt.shape
    bs, ts, _ = x_sample.shape
    ts_pad = -(-ts // SUBLANE) * SUBLANE
    assert ts_pad == SAMPLE_BLOCKING['c'] and tp % (PROMPT_BLOCKING['c'] * PROMPT_BLOCKING['nc']) == 0
    gf = final_norm_g.reshape(1, 1, D_MODEL)

    xp = x_prompt.reshape(bp * tp, D_MODEL)
    xs = jnp.pad(x_sample, ((0, 0), (0, ts_pad - ts), (0, 0))).reshape(bs * ts_pad, D_MODEL)
    zero = lambda *shape: jnp.zeros((1,) + shape, F32)
    p_states = (zero(bp, A_HEADS, A_DK, A_DV), zero(bp, 3, A_CONV_CH),
                zero(bp, B_HEADS, B_N, B_N), zero(bp, 1, B_IN),
                zero(bp, C_WIDTH), zero(bp, 3, C_WIDTH))
    s_states = (state_a_S, state_a_conv, state_b_S, state_b_shift, state_c_h, state_c_conv)
    lp = _prepare_params(p)
    p_new = s_new = None
    for l in range(DEPTH):
        final = l == DEPTH - 1
        xp, p_new = _layer(xp, p_states, p_new, lp, gf, layer=l, state_layer=0, bsz=bp, t_pad=tp,
                           t_valid=tp, blocking=PROMPT_BLOCKING, final=final)
        xs, s_new = _layer(xs, s_states, s_new, lp, gf, layer=l, state_layer=l, bsz=bs, t_pad=ts_pad,
                           t_valid=ts, blocking=SAMPLE_BLOCKING, final=final)
    squeeze_h = lambda st: st[:4] + (st[4].reshape(DEPTH, -1, C_WIDTH), st[5])
    y_prompt = xp.reshape(bp, tp, D_MODEL)
    y_sample = xs.reshape(bs, ts_pad, D_MODEL)[:, :ts]
    return (y_prompt, y_sample) + squeeze_h(p_new) + squeeze_h(s_new)
```

```python
import functools

import jax
import jax.numpy as jnp
from jax import lax
from jax.experimental import pallas as pl
from jax.experimental.pallas import tpu as pltpu

F32 = jnp.float32
BF16 = jnp.bfloat16

EPS = 1e-6
D_MODEL = 1024
DEPTH = 2
A_HEADS = 4
A_DK = 128
A_DV = 128
A_QK_W = A_HEADS * A_DK
A_V_W = A_HEADS * A_DV
A_CONV_CH = 2 * A_QK_W + A_V_W
B_N = 64
B_HEADS = 8
B_W = B_HEADS * B_N
B_LORA_WA = 128
B_GATE_LORA = 128
B_IN = 3 * B_W + B_LORA_WA + B_GATE_LORA
B_LN_EPS = 64e-5
DECAY_SCALE = 0.6065306597126334
C_WIDTH = 512
C_BLOCKS = 8
C_POW = 8.0
N_BRANCH = 3
D_FF = 4 * D_MODEL

LANE = 128
SUBLANE = 8
CONV_TAPS = 4
CONV_PAD = SUBLANE

ZA_W = A_CONV_CH + A_V_W + LANE
ZA_BA = A_CONV_CH + A_V_W
ZB_W = B_IN
ZC_W = 2 * C_WIDTH
ZG_W = N_BRANCH * D_MODEL
Z_WIDTHS = (ZA_W, ZB_W, ZC_W, ZG_W)
A_IN = A_CONV_CH + A_V_W + 2 * A_HEADS
N_IN = A_IN + ZB_W + ZC_W + ZG_W

VMEM_LIMIT = 60 * 1024 * 1024
PROJ_PIECE = 256
RGLRU_PIECE = 128


def _dot(a, b):
    return jnp.dot(a.astype(BF16), b.astype(BF16), preferred_element_type=F32)


def _dot_nt(a, b):
    return lax.dot_general(a.astype(BF16), b.astype(BF16), (((1,), (1,)), ((), ())),
                           preferred_element_type=F32)


def _dot_tn(a, b):
    return lax.dot_general(a.astype(BF16), b.astype(BF16), (((0,), (0,)), ((), ())),
                           preferred_element_type=F32)


def _sigmoid(x):
    return 1.0 / (1.0 + jnp.exp(-x))


def _silu(x):
    return x * _sigmoid(x)


def _softplus(x):
    return jnp.maximum(x, 0.0) + jnp.log1p(jnp.exp(-jnp.abs(x)))


def _gelu_tanh(x):
    return 0.5 * x * (1.0 + jnp.tanh(0.7978845608028654 * (x + 0.044715 * (x * x * x))))


def _rows(shape):
    return lax.broadcasted_iota(jnp.int32, shape, 0)


def _cols(shape):
    return lax.broadcasted_iota(jnp.int32, shape, 1)


def _rows_in(shape, period):
    r = _rows(shape)
    return r if period >= shape[0] else r & (period - 1)


def _shift_rows(x, s, fill, period):
    return jnp.where(_rows_in(x.shape, period) >= s, pltpu.roll(x, s, 0), fill)


def _bf16_pieces(x, n):
    pieces = []
    for _ in range(n):
        p = x.astype(BF16)
        pieces.append(p)
        x = x - p.astype(F32)
    return pieces


def _cumsum_chunks(x, c):
    tri = (_rows((c, c)) >= _cols((c, c))).astype(BF16)
    pieces = _bf16_pieces(x, 3)
    chunks = [sum(jnp.dot(tri, p[r0:r0 + c], preferred_element_type=F32) for p in pieces)
              for r0 in range(0, x.shape[0], c)]
    return chunks[0] if len(chunks) == 1 else jnp.concatenate(chunks, axis=0)


def _unit_lower_inverses(ns):
    c = ns[0].shape[0]
    eye = (_rows(ns[0].shape) == _cols(ns[0].shape)).astype(F32)
    ts = [eye + n for n in ns]
    ps = list(ns)
    s = 2
    while s < c:
        ps = [_dot(p, p) for p in ps]
        ts = [t + _dot(t, p) for t, p in zip(ts, ps)]
        s *= 2
    return ts


def _unit_lower_solves(ns, xs):
    c = ns[0].shape[0]
    xs = [x + _dot(n, x) for n, x in zip(ns, xs)]
    ps = list(ns)
    s = 2
    while s < c:
        ps = [_dot(p, p) for p in ps]
        xs = [x + _dot(p, x) for p, x in zip(ps, xs)]
        s *= 2
    return xs


def _param_spec(shape, layer):
    nd = len(shape)
    return pl.BlockSpec((None,) + shape, lambda *_: (layer,) + (0,) * nd, pipeline_mode=pl.Buffered(1))


def _group_of_2d(b, t):
    return b


def _stacked_spec(layers, shape, group_of=_group_of_2d):
    nd = len(shape) - 1
    return pl.BlockSpec((layers,) + shape, lambda *g: (0, group_of(*g)) + (0,) * nd)


def _emit_state(last, out_ref, prev_ref, new_ref):
    @pl.when(last)
    def _():
        n_prev = out_ref.shape[0] - 1
        if n_prev:
            out_ref[0:n_prev] = prev_ref[...]
        out_ref[n_prev] = new_ref[...]


def _state_spec(shape, layer, group_of=_group_of_2d):
    nd = len(shape) - 1
    return pl.BlockSpec((None,) + shape, lambda *g: (layer, group_of(*g)) + (0,) * nd)


def _in_proj_kernel(prev_h_ref, prev_conv_ref, prev_aconv_ref, x_ref, g_ref, w_ref, conv0_ref, h0_ref,
                    aconv0_ref, convw_ref, convb_ref, wg_ref, bg_ref, l_ref, aconvw_ref,
                    za_ref, zb_ref, yc_ref, zg_ref, hout_ref, convout_ref, aconvout_ref,
                    w_rest, xpad, h_scr, xpad_a, *, l, ns, nt, valid_rows):
    step = pl.program_id(0)
    t = step % nt
    first = t == 0
    last = t == nt - 1
    @pl.when(step == 0)
    def _():
        w_rest[...] = w_ref[:, A_IN:]

    x = x_ref[...]
    h = x * lax.rsqrt(jnp.mean(x * x, axis=-1, keepdims=True) + EPS) * g_ref[...]
    h = h.astype(BF16)
    proj = lambda ref, off, width: jnp.dot(h, ref[:, off:off + width], preferred_element_type=F32)

    @pl.when(first)
    def _():
        xpad_a[:, CONV_PAD - CONV_TAPS + 1:CONV_PAD, :] = aconv0_ref[...]

    @pl.when(first)
    def _():
        h_scr[...] = h0_ref[...]
        xpad[:, CONV_PAD - CONV_TAPS + 1:CONV_PAD, :] = conv0_ref[...]

    def causal_conv(z, pad, w, cols):
        y = z * w[CONV_TAPS - 1:CONV_TAPS, cols]
        for q in range(ns):
            pad[q, CONV_PAD:CONV_PAD + l, cols] = z[q * l:(q + 1) * l]
        for j in range(CONV_TAPS - 1):
            lo_row = CONV_PAD - CONV_TAPS + 1 + j
            parts = [pad[q, lo_row:lo_row + l, cols] for q in range(ns)]
            view = parts[0] if ns == 1 else jnp.concatenate(parts, axis=0)
            y = y + view * w[j:j + 1, cols]
        return y

    def conv_piece(lo):
        cols = slice(lo, lo + PROJ_PIECE)
        za_ref[:, cols] = _silu(causal_conv(proj(w_ref, lo, PROJ_PIECE), xpad_a, aconvw_ref, cols))

    def gate_piece():
        za_ref[:, A_CONV_CH:] = proj(w_ref, A_CONV_CH, ZA_W - A_CONV_CH)

    def rglru_piece(lo):
        cols = slice(lo, lo + RGLRU_PIECE)
        xc = (causal_conv(proj(w_rest, ZB_W + lo, RGLRU_PIECE), xpad, convw_ref, cols) + convb_ref[:, cols])
        gate_cols = lambda off: slice(off + lo, off + lo + RGLRU_PIECE)
        r = _sigmoid(_dot(xc, wg_ref[cols, gate_cols(0)]) + bg_ref[:, gate_cols(0)])
        i = _sigmoid(_dot(xc, wg_ref[cols, gate_cols(C_WIDTH)]) + bg_ref[:, gate_cols(C_WIDTH)])
        log_a = -C_POW * r * _softplus(-l_ref[:, cols])
        a = jnp.exp(log_a)
        var = -jnp.tanh(log_a) * (a * a + 1.0)
        b = jnp.where(var > 0.0, var * lax.rsqrt(var), 0.0) * (i * xc)
        if valid_rows < l:
            valid = _rows_in(a.shape, l) < valid_rows
            a = jnp.where(valid, a, 1.0)
            b = jnp.where(valid, b, 0.0)
        s = 1
        while s < SUBLANE:
            b = a * _shift_rows(b, s, 0.0, SUBLANE) + b
            a = a * _shift_rows(a, s, 1.0, SUBLANE)
            s *= 2
        gate = _gelu_tanh(proj(w_rest, ZB_W + C_WIDTH + lo, RGLRU_PIECE))
        for q in range(ns):
            carry = h_scr[q, :, cols]
            for r0 in range(q * l, (q + 1) * l, SUBLANE):
                rs = slice(r0, r0 + SUBLANE)
                hseq = a[rs] * carry + b[rs]
                carry = hseq[SUBLANE - 1:SUBLANE, :]
                yc_ref[rs, cols] = hseq * gate[rs]
            h_scr[q, :, cols] = carry

    def project(out_ref, w, off, width):
        def piece(lo):
            hi = min(lo + PROJ_PIECE, width)
            out_ref[:, lo:hi] = proj(w, off + lo, hi - lo)
        return [functools.partial(piece, lo) for lo in range(0, width, PROJ_PIECE)]

    assert A_CONV_CH % PROJ_PIECE == 0 and C_WIDTH % RGLRU_PIECE == 0
    busy = ([functools.partial(rglru_piece, lo) for lo in range(0, C_WIDTH, RGLRU_PIECE)]
            + [functools.partial(conv_piece, lo) for lo in range(0, A_CONV_CH, PROJ_PIECE)])
    plain = ([gate_piece] + project(zb_ref, w_rest, 0, ZB_W) + project(zg_ref, w_rest, ZB_W + ZC_W, ZG_W))
    per_busy = -(-len(plain) // len(busy))
    for k, busy_piece in enumerate(busy):
        busy_piece()
        for plain_piece in plain[k * per_busy:(k + 1) * per_busy]:
            plain_piece()
    for plain_piece in plain[len(busy) * per_busy:]:
        plain_piece()
    _carry_history(xpad, l, valid_rows, last, convout_ref, prev_conv_ref, CONV_TAPS - 1)
    _emit_state(last, hout_ref, prev_h_ref, h_scr)
    _carry_history(xpad_a, l, valid_rows, last, aconvout_ref, prev_aconv_ref, CONV_TAPS - 1)


def _in_proj(x, g, w, conv0, h0, aconv0, prm, aconvw, *, tm, bsz, t_pad, t_valid, layer, state_layer, prev):
    m = x.shape[0]
    l = min(tm, t_pad)
    ns = tm // l
    _, _, valid_rows = _mixer_grid(bsz, t_pad, t_valid, l, ns)
    nt = t_pad // l
    group_of = lambda i: i // nt
    spec = functools.partial(_param_spec, layer=layer)
    kern = functools.partial(_in_proj_kernel, l=l, ns=ns, nt=nt, valid_rows=valid_rows)
    state_shapes = ((1, C_WIDTH), (3, C_WIDTH), (3, A_CONV_CH))
    kern, prev_args, prev_specs, n_prev = _with_prev(kern, prev, state_shapes, ns, group_of)
    tile = lambda wd: pl.BlockSpec((tm, wd), lambda i: (i, 0))
    widths = (ZA_W, ZB_W, C_WIDTH, ZG_W)
    return pl.pallas_call(
        kern,
        grid=(m // tm,),
        in_specs=prev_specs + [
                  tile(D_MODEL), spec((1, D_MODEL)), spec((D_MODEL, N_IN)),
                  _state_spec((ns, 3, C_WIDTH), state_layer, group_of),
                  _state_spec((ns, 1, C_WIDTH), state_layer, group_of),
                  _state_spec((ns, 3, A_CONV_CH), state_layer, group_of),
                  spec((CONV_TAPS, C_WIDTH)), spec((1, C_WIDTH)),
                  spec((C_WIDTH, 2 * C_WIDTH)), spec((1, 2 * C_WIDTH)),
                  spec((1, C_WIDTH)), spec((CONV_TAPS, A_CONV_CH))],
        out_specs=[tile(wd) for wd in widths]
        + [_stacked_spec(n_prev + 1, (ns,) + shp, group_of) for shp in state_shapes],
        out_shape=[jax.ShapeDtypeStruct((m, wd), F32) for wd in widths]
        + [jax.ShapeDtypeStruct((n_prev + 1, bsz) + shp, F32) for shp in state_shapes],
        scratch_shapes=[pltpu.VMEM((D_MODEL, N_IN - A_IN), BF16),
                        pltpu.VMEM((ns, CONV_PAD + l, C_WIDTH), F32),
                        pltpu.VMEM((ns, 1, C_WIDTH), F32),
                        pltpu.VMEM((ns, CONV_PAD + l, A_CONV_CH), F32)],
        compiler_params=pltpu.CompilerParams(dimension_semantics=("arbitrary",),
                                             vmem_limit_bytes=VMEM_LIMIT),
        name="in_proj",
    )(*prev_args, x, g, w, conv0, h0, aconv0, *prm, aconvw)


def _with_history(x, xpad, first, hist_ref, ns, l, depth):
    @pl.when(first)
    def _():
        xpad[:, CONV_PAD - depth:CONV_PAD, :] = hist_ref[...]

    for s in range(ns):
        xpad[s, CONV_PAD:CONV_PAD + l, :] = x[s * l:(s + 1) * l]
    views = []
    for j in range(depth, 0, -1):
        parts = [xpad[s, CONV_PAD - j:CONV_PAD - j + l, :] for s in range(ns)]
        views.append(parts[0] if ns == 1 else jnp.concatenate(parts, axis=0))
    return views


def _carry_history(xpad, l, valid_rows, last, out_ref, prev_ref, depth):
    _emit_state(last, out_ref, prev_ref,
                xpad.at[:, CONV_PAD - depth + valid_rows:CONV_PAD + valid_rows, :])

    nxt = xpad[:, CONV_PAD - depth + l:CONV_PAD + l, :]
    xpad[:, CONV_PAD - depth:CONV_PAD, :] = nxt


def _gdn_kernel(prev_s_ref, za_ref, s0_ref, alog_ref, dt_ref, ng_ref,
                y_ref, sout_ref, s_scr, *, c, nc, ns, nt, valid_rows, group):
    t = pl.program_id(1)
    first = t == 0
    last = t == nt - 1
    l = nc * c
    rows = ns * l

    @pl.when(first)
    def _():
        s_scr[...] = s0_ref[...]

    qkv = za_ref[:, 0:A_CONV_CH]

    ba = za_ref[:, ZA_BA:ZA_BA + LANE]
    ba = jnp.where(_cols(ba.shape) < 2 * A_HEADS, ba, 0.0)
    beta_all = _sigmoid(ba)
    g_all = -jnp.exp(alog_ref[...]) * _softplus(ba + dt_ref[...])
    masked = valid_rows < l
    if masked:
        valid = _rows_in((rows, LANE), l) < valid_rows
        beta_all = jnp.where(valid, beta_all, 0.0)
        g_all = jnp.where(valid, g_all, 0.0)
    gc_all = g_all
    s = 1
    while s < c:
        gc_all = gc_all + _shift_rows(gc_all, s, 0.0, c)
        s *= 2

    rr = _rows((c, c))
    cc = _cols((c, c))
    causal = rr >= cc
    strict = rr > cc
    ng = ng_ref[...]
    pick = (_cols((SUBLANE * A_HEADS, LANE))
            == A_HEADS + _rows((SUBLANE * A_HEADS, LANE)) // SUBLANE).astype(BF16)

    all_chunks = [(s, n) for n in range(nc) for s in range(ns)]
    heads = range(A_HEADS)
    state = {(s, h): s_scr[s, h] for s in range(ns) for h in heads}

    def advance(chunks):
        inst = [(s, n, h) for (s, n) in chunks for h in heads]
        lower, attn, rhs2, kdec, qg, glast = {}, {}, {}, {}, {}, {}
        for (s, n) in chunks:
            r0 = s * l + n * c
            rs = slice(r0, r0 + c)
            gc_rows = sum(lax.dot_general(pick, piece, (((1,), (1,)), ((), ())), preferred_element_type=F32)
                          for piece in _bf16_pieces(gc_all[rs], 3))
            for h in heads:
                q = qkv[rs, h * A_DK:(h + 1) * A_DK]
                k = qkv[rs, A_QK_W + h * A_DK:A_QK_W + (h + 1) * A_DK]
                v = qkv[rs, 2 * A_QK_W + h * A_DV:2 * A_QK_W + (h + 1) * A_DV]
                q = q * lax.rsqrt(jnp.sum(q * q, axis=-1, keepdims=True) + EPS) * (A_DK ** -0.5)
                k = k * lax.rsqrt(jnp.sum(k * k, axis=-1, keepdims=True) + EPS)
                if masked:
                    valid_c = _rows((c, A_DK)) < valid_rows - n * c
                    k = jnp.where(valid_c, k, 0.0)
                    v = jnp.where(valid_c, v, 0.0)
                beta = beta_all[rs, h:h + 1]
                gc = gc_all[rs, A_HEADS + h:A_HEADS + h + 1]
                gc_row = gc_rows[SUBLANE * h:SUBLANE * h + 1, :]
                dec = jnp.where(causal, jnp.exp(jnp.where(causal, gc - gc_row, 0.0)), 0.0)
                kb = k * beta
                egc = jnp.exp(gc)
                sc = _dot_nt(jnp.concatenate([kb, q], axis=0), k)
                key = (s, n, h)
                lower[key] = jnp.where(strict, -sc[:c] * dec, 0.0)
                attn[key] = sc[c:] * dec
                rhs2[key] = jnp.concatenate([v * beta, kb * egc], axis=1)
                gc_last = gc[c - 1:c, :]
                kdec[key] = k * jnp.exp(gc_last - gc)
                glast[key] = jnp.exp(gc_last)
                qg[key] = q * egc
        tinv = dict(zip(inst, _unit_lower_inverses([lower[i] for i in inst])))
        uw = {i: _dot(tinv[i], rhs2[i]) for i in inst}
        ao = {i: _dot(attn[i], uw[i]) for i in inst}
        bm = {i: _dot_tn(kdec[i], uw[i]) for i in inst}
        for (s, n) in chunks:
            r0 = s * l + n * c
            for h in heads:
                i = (s, n, h)
                st = state[(s, h)]
                o = _dot(qg[i] - ao[i][:, A_DV:], st) + ao[i][:, :A_DV]
                state[(s, h)] = st * glast[i] - _dot(bm[i][:, A_DV:], st) + bm[i][:, :A_DV]
                o = o * lax.rsqrt(jnp.mean(o * o, axis=-1, keepdims=True) + EPS)
                z = za_ref[r0:r0 + c, A_CONV_CH + h * A_DV:A_CONV_CH + (h + 1) * A_DV]
                y_ref[r0:r0 + c, h * A_DV:(h + 1) * A_DV] = o * ng * _silu(z)

    for g in range(0, len(all_chunks), group):
        advance(all_chunks[g:g + group])
    for s in range(ns):
        for h in heads:
            s_scr[s, h] = state[(s, h)]

    _emit_state(last, sout_ref, prev_s_ref, s_scr)


def _with_prev(kern, prev, shapes, ns, group_of=_group_of_2d):
    if prev is None:
        return functools.partial(kern, *([None] * len(shapes))), [], [], 0
    n_prev = prev[0].shape[0]
    return kern, list(prev), [_stacked_spec(n_prev, (ns,) + shp, group_of) for shp in shapes], n_prev


def _mixer_grid(bsz, t_pad, t_valid, l, ns):
    assert t_pad % l == 0 and bsz % ns == 0 and (ns == 1 or t_pad == l)
    nt = t_pad // l
    valid_rows = t_valid - (nt - 1) * l
    assert 0 < valid_rows <= l and (nt == 1 or valid_rows == l)
    return (bsz // ns, nt), ns * l, valid_rows


def _gdn(za, s0, alog_row, dt_row, ng, *, bsz, t_pad, t_valid, c, nc, ns, group,
         layer, state_layer, prev):
    grid, rows, valid_rows = _mixer_grid(bsz, t_pad, t_valid, nc * c, ns)
    spec = functools.partial(_param_spec, layer=layer)
    nt = grid[1]
    kern = functools.partial(_gdn_kernel, c=c, nc=nc, ns=ns, nt=nt, valid_rows=valid_rows,
                             group=group)
    state_shapes = ((A_HEADS, A_DK, A_DV),)
    kern, prev_args, prev_specs, n_prev = _with_prev(kern, prev, state_shapes, ns)
    return pl.pallas_call(
        kern,
        grid=grid,
        in_specs=prev_specs + [
                  pl.BlockSpec((rows, ZA_W), lambda b, t: (b * nt + t, 0)),
                  _state_spec((ns, A_HEADS, A_DK, A_DV), state_layer),
                  spec((1, LANE)),
                  spec((1, LANE)),
                  spec((1, A_DV))],
        out_specs=[pl.BlockSpec((rows, A_V_W), lambda b, t: (b * nt + t, 0)),
                   ] + [_stacked_spec(n_prev + 1, (ns,) + shp) for shp in state_shapes],
        out_shape=[jax.ShapeDtypeStruct((bsz * t_pad, A_V_W), F32)]
        + [jax.ShapeDtypeStruct((n_prev + 1, bsz) + shp, F32) for shp in state_shapes],
        scratch_shapes=[pltpu.VMEM((ns, A_HEADS, A_DK, A_DV), F32)],
        compiler_params=pltpu.CompilerParams(dimension_semantics=("arbitrary", "arbitrary"),
                                             vmem_limit_bytes=VMEM_LIMIT),
        name="gdn",
    )(*prev_args, za, s0, alog_row, dt_row, ng)


def _rwkv_kernel(prev_s_ref, prev_shift_ref, zb_ref, shift0_ref, s0_ref, mu_ref, w0_ref, wup_ref, a0_ref,
                 aup_ref, gup_ref, kk_ref, ka_ref, rk_ref, lnw_ref, lnb_ref, hsum_ref,
                 y_ref, sout_ref, shiftout_ref, xpad, s_scr, prep, *, c, nc, ns, nt, valid_rows, group):
    t = pl.program_id(1)
    first = t == 0
    last = t == nt - 1
    l = nc * c
    rows = ns * l

    @pl.when(first)
    def _():
        s_scr[...] = s0_ref[...]

    zb = zb_ref[...]
    (prev,) = _with_history(zb, xpad, first, shift0_ref, ns, l, 1)
    _carry_history(xpad, l, valid_rows, last, shiftout_ref, prev_shift_ref, 1)
    zs = zb + (prev - zb) * mu_ref[...]

    r_all = zs[:, 0:B_W]
    k_all = zs[:, B_W:2 * B_W]
    v_all = zs[:, 2 * B_W:3 * B_W]
    xwa = zs[:, 3 * B_W:3 * B_W + B_LORA_WA]
    xg = zs[:, 3 * B_W + B_LORA_WA:B_IN]
    lw = -DECAY_SCALE * _sigmoid(w0_ref[...] + _dot(jnp.tanh(xwa), wup_ref[...]))
    a_all = _sigmoid(a0_ref[...] + _dot(xwa, aup_ref[...]))
    g_all = _dot(_sigmoid(xg), gup_ref[...])
    kk_all = k_all * kk_ref[...]
    k_all = k_all * (1.0 + (a_all - 1.0) * ka_ref[...])
    masked = valid_rows < l
    if masked:
        valid = _rows_in((rows, B_W), l) < valid_rows
        lw = jnp.where(valid, lw, 0.0)
        kk_all = jnp.where(valid, kk_all, 0.0)
        k_all = jnp.where(valid, k_all, 0.0)
        v_all = jnp.where(valid, v_all, 0.0)
    gc_all = _cumsum_chunks(lw, c)

    def head_sums(x, pieces=1):
        return sum(jnp.dot(p, hsum_ref[...], preferred_element_type=F32) for p in _bf16_pieces(x, pieces))

    kkn_all = kk_all * lax.rsqrt(head_sums(kk_all * kk_all, pieces=2) + EPS)
    bonus_all = head_sums(r_all * k_all * rk_ref[...]) * v_all

    rr = _rows((2 * c, 2 * c))
    cc = _cols((2 * c, 2 * c))
    tt = jnp.where(rr >= c, rr - c, rr)
    ss = jnp.where(cc >= c, cc - c, cc)
    mask2 = jnp.logical_or(tt > ss, jnp.logical_and(rr >= c, tt == ss))

    prep_names = ('r', 'k', 'v', 'kkn', 'a', 'lw', 'gc', 'bonus', 'g')
    for idx, val in enumerate((r_all, k_all, v_all, kkn_all, a_all, lw, gc_all, bonus_all, g_all)):
        prep[idx] = val
    heads = range(B_HEADS)
    sls = [slice(h * B_N, (h + 1) * B_N) for h in heads]

    def at_rows(name, rs):
        return prep[prep_names.index(name), rs, :]

    def advance(g, carry):
        members = range(group)
        inst = [(j, h) for j in members for h in heads]
        amat, rhs_uw, r_abs, tails, e_last, vs, rows_of, seq_of = {}, {}, {}, {}, {}, {}, {}, {}
        for j in members:
            q = g * group + j
            rs = pl.ds(pl.multiple_of(q * c, c), c)
            rows_of[j] = rs
            seq_of[j] = 0 if ns == 1 else q
            gc = at_rows('gc', rs)
            lwc = at_rows('lw', rs)
            gmid = gc[c // 2:c // 2 + 1, :]
            glast = gc[c - 1:c, :]
            e_last[j] = jnp.exp(glast)
            r = at_rows('r', rs)
            k = at_rows('k', rs)
            v = at_rows('v', rs)
            alpha = -at_rows('kkn', rs)
            bet = at_rows('kkn', rs) * at_rows('a', rs)
            e_inv = jnp.exp(gmid - gc)
            e_tail = jnp.exp(glast - gc)
            lhs_a = alpha * jnp.exp(gc - lwc - gmid)
            lhs_r = r * jnp.exp(gc - gmid)
            rhs_b = bet * e_inv
            rhs_k = k * e_inv
            abs_a = alpha * jnp.exp(gc - lwc)
            abs_r = r * jnp.exp(gc)
            tail_b = bet * e_tail
            tail_k = k * e_tail
            for h in heads:
                sl = sls[h]
                lhs = jnp.concatenate([lhs_a[:, sl], lhs_r[:, sl]], axis=0)
                rhs = jnp.concatenate([rhs_b[:, sl], rhs_k[:, sl]], axis=0)
                key = (j, h)
                amat[key] = jnp.where(mask2, _dot_nt(lhs, rhs), 0.0)
                rhs_uw[key] = abs_a[:, sl]
                r_abs[key] = abs_r[:, sl]
                tails[key] = jnp.concatenate([tail_b[:, sl], tail_k[:, sl]], axis=0)
                vs[key] = v[:, sl]

        from_v = {i: _dot(amat[i][:, c:], vs[i]) for i in inst}
        uw = dict(zip(inst, _unit_lower_solves(
            [amat[i][:c, :c] for i in inst],
            [jnp.concatenate([from_v[i][:c], rhs_uw[i]], axis=1) for i in inst])))
        ro = {i: _dot(amat[i][c:, :c], uw[i]) for i in inst}
        zeros = jnp.zeros((c, B_N), F32)
        both = {i: _dot_tn(jnp.concatenate([uw[i], jnp.concatenate([vs[i], zeros], axis=1)], axis=0), tails[i])
                for i in inst}
        bb = {i: both[i][:B_N] for i in inst}
        mp = {i: both[i][B_N:] for i in inst}
        state = {}
        outs = []
        for j in members:
            per_head = []
            for h in heads:
                i = (j, h)
                skey = (0 if ns == 1 else j, h)
                st = state[skey] if skey in state else s_scr[seq_of[j], h]
                per_head.append(_dot_nt(r_abs[i] + ro[i][:, B_N:], st) + from_v[i][c:] + ro[i][:, :B_N])
                state[skey] = st * e_last[j][:, sls[h]] + _dot(st, mp[i]) + bb[i]
            outs.append(jnp.concatenate(per_head, axis=1))
        for j, o in zip(members, outs):
            d = o - head_sums(o) * (1.0 / B_N)
            var = head_sums(d * d) * (1.0 / B_N)
            o = d * lax.rsqrt(var + B_LN_EPS) * lnw_ref[...] + lnb_ref[...]
            y_ref[rows_of[j], :] = (o + at_rows('bonus', rows_of[j])) * at_rows('g', rows_of[j])
        for (sk, h), st in state.items():
            s_scr[seq_of[sk], h] = st
        return carry

    assert (ns * nc) % group == 0 and (ns == 1 or nc == 1)
    lax.fori_loop(0, ns * nc // group, advance, 0)

    _emit_state(last, sout_ref, prev_s_ref, s_scr)


def _rwkv(zb, shift0, s0, prm, *, bsz, t_pad, t_valid, c, nc, ns, group, layer, state_layer, prev):
    grid, rows, valid_rows = _mixer_grid(bsz, t_pad, t_valid, nc * c, ns)
    nt = grid[1]
    kern = functools.partial(_rwkv_kernel, c=c, nc=nc, ns=ns, nt=nt, valid_rows=valid_rows,
                             group=group)
    spec = functools.partial(_param_spec, layer=layer)
    row = lambda wd: spec((1, wd))
    state_shapes = ((B_HEADS, B_N, B_N), (1, B_IN))
    kern, prev_args, prev_specs, n_prev = _with_prev(kern, prev, state_shapes, ns)
    return pl.pallas_call(
        kern,
        grid=grid,
        in_specs=prev_specs + [
                  pl.BlockSpec((rows, ZB_W), lambda b, t: (b * nt + t, 0)),
                  _state_spec((ns, 1, B_IN), state_layer),
                  _state_spec((ns, B_HEADS, B_N, B_N), state_layer),
                  row(B_IN), row(B_W), spec((B_LORA_WA, B_W)), row(B_W),
                  spec((B_LORA_WA, B_W)), spec((B_GATE_LORA, B_W)),
                  row(B_W), row(B_W), row(B_W), row(B_W), row(B_W), _param_spec((B_W, B_W), 0)],
        out_specs=[pl.BlockSpec((rows, B_W), lambda b, t: (b * nt + t, 0)),
                   ] + [_stacked_spec(n_prev + 1, (ns,) + shp) for shp in state_shapes],
        out_shape=[jax.ShapeDtypeStruct((bsz * t_pad, B_W), F32)]
        + [jax.ShapeDtypeStruct((n_prev + 1, bsz) + shp, F32) for shp in state_shapes],
        scratch_shapes=[pltpu.VMEM((ns, CONV_PAD + nc * c, B_IN), F32),
                        pltpu.VMEM((ns, B_HEADS, B_N, B_N), F32),
                        pltpu.VMEM((9, rows, B_W), F32)],
        compiler_params=pltpu.CompilerParams(dimension_semantics=("arbitrary", "arbitrary"),
                                             vmem_limit_bytes=VMEM_LIMIT),
        name="rwkv",
    )(*prev_args, zb, shift0, s0, *prm)


FF_CHUNK = 1024


def _merge_ffn_kernel(x_ref, ya_ref, yb_ref, yc_ref, zg_ref, wbr_ref, wout_ref, g2_ref,
                      wup_ref, wdown_ref, gf_ref, o_ref, *, final):
    merged = None
    for n, y_ref in enumerate((ya_ref, yb_ref, yc_ref)):
        proj = _dot(y_ref[...], wbr_ref[n])
        term = _sigmoid(zg_ref[:, n * D_MODEL:(n + 1) * D_MODEL]) * proj
        merged = term if merged is None else merged + term
    x = x_ref[...] + _dot(merged, wout_ref[...])
    h = x * lax.rsqrt(jnp.mean(x * x, axis=-1, keepdims=True) + EPS) * g2_ref[...]
    h = h.astype(BF16)
    acc = x
    for j in range(0, D_FF, FF_CHUNK):
        up = jnp.dot(h, wup_ref[:, j:j + FF_CHUNK], preferred_element_type=F32)
        acc = acc + _dot(jnp.square(jnp.maximum(up, 0.0)), wdown_ref[j:j + FF_CHUNK, :])
    if final:
        acc = acc * lax.rsqrt(jnp.mean(acc * acc, axis=-1, keepdims=True) + EPS) * gf_ref[...]
    o_ref[...] = acc


def _merge_ffn(x, ya, yb, yc, zg, wbr, wout, g2, wup, wdown, gf, tm, final, layer):
    m = x.shape[0]
    spec = functools.partial(_param_spec, layer=layer)
    tile = lambda wd: pl.BlockSpec((tm, wd), lambda i: (i, 0))
    return pl.pallas_call(
        functools.partial(_merge_ffn_kernel, final=final),
        grid=(m // tm,),
        in_specs=[tile(D_MODEL), tile(A_V_W), tile(B_W), tile(C_WIDTH), tile(ZG_W),
                  spec((N_BRANCH, A_V_W, D_MODEL)), spec((D_MODEL, D_MODEL)),
                  spec((1, D_MODEL)), spec((D_MODEL, D_FF)),
                  spec((D_FF, D_MODEL)), _param_spec((1, D_MODEL), 0)],
        out_specs=tile(D_MODEL),
        out_shape=jax.ShapeDtypeStruct((m, D_MODEL), F32),
        compiler_params=pltpu.CompilerParams(dimension_semantics=("arbitrary",),
                                             vmem_limit_bytes=VMEM_LIMIT),
        name="merge_ffn",
    )(x, ya, yb, yc, zg, wbr, wout, g2, wup, wdown, gf)


def _block_diag(w):
    g, n, _ = w.shape
    eye = jnp.eye(g, dtype=w.dtype)
    return (eye[:, None, :, None] * w[:, :, None, :]).reshape(g * n, g * n)


def _prepare_params(p):
    depth = p['w_in'].shape[0]
    row = lambda a: a.reshape(depth, 1, -1).astype(F32)
    w_in = p['w_in'].astype(BF16)
    lane_row = lambda a: jnp.zeros((depth, 1, LANE), F32).at[:, 0, A_HEADS:2 * A_HEADS].set(a)
    zeros_lora = jnp.zeros((depth, B_LORA_WA // 2, B_W), F32)
    block_diag = jax.vmap(_block_diag)
    gdn = (p['a_conv_w'], lane_row(p['a_A_log']), lane_row(p['a_dt_bias']), row(p['a_norm_g']))
    rwkv = (row(p['b_mu']), row(p['b_w0']),
            jnp.concatenate([p['b_w_up'], zeros_lora], axis=1).astype(BF16),
            row(p['b_a0']),
            jnp.concatenate([zeros_lora, p['b_a_up']], axis=1).astype(BF16),
            p['b_g_up'].astype(BF16),
            row(p['b_k_k']), row(p['b_k_a']), row(p['b_r_k']), row(p['b_ln_w']), row(p['b_ln_b']),
            jnp.kron(jnp.eye(B_HEADS, dtype=F32), jnp.ones((B_N, B_N), F32)).astype(BF16)[None])
    rglru = (p['c_conv_w'], row(p['c_conv_b']),
             jnp.concatenate([block_diag(p['c_wa']), block_diag(p['c_wx'])], axis=2).astype(BF16),
             jnp.concatenate([row(p['c_ba']), row(p['c_bx'])], axis=2), row(p['c_L']))
    ffn = (p['w_branch'].astype(BF16), p['w_out'].astype(BF16), row(p['norm2_g']),
           p['w_up'].astype(BF16), p['w_down'].astype(BF16))
    return dict(norm1_g=row(p['norm1_g']), w_in=w_in, gdn=gdn, rwkv=rwkv, rglru=rglru, ffn=ffn)


PROMPT_BLOCKING = dict(c=64, nc=8, ns=1, gdn_group=4, rwkv_group=4, in_rows=256)
SAMPLE_BLOCKING = dict(c=SUBLANE, nc=1, ns=8, gdn_group=8, rwkv_group=4, in_rows=256)
FFN_TILE = 512


def _layer(x, states, new_states, lp, gf, *, layer, state_layer, bsz, t_pad, t_valid, blocking, final):
    a_s, a_conv, b_s, b_shift, c_h, c_conv = states
    prev = (lambda *idx: None) if new_states is None else (lambda *idx: tuple(new_states[i] for i in idx))
    kw = dict(bsz=bsz, t_pad=t_pad, t_valid=t_valid, layer=layer, state_layer=state_layer)
    chunked = dict(c=blocking['c'], nc=blocking['nc'], ns=blocking['ns'], **kw)
    a_convw, *gdn_prm = lp['gdn']
    za, zb, yc, zg, c_h, c_conv, a_conv = _in_proj(
        x, lp['norm1_g'], lp['w_in'], c_conv, c_h.reshape(-1, bsz, 1, C_WIDTH), a_conv, lp['rglru'], a_convw,
        tm=blocking['in_rows'], prev=prev(4, 5, 1), **kw)
    ya, a_s = _gdn(za, a_s, *gdn_prm, group=blocking['gdn_group'], prev=prev(0), **chunked)
    yb, b_s, b_shift = _rwkv(zb, b_shift, b_s, lp['rwkv'], group=blocking['rwkv_group'], prev=prev(2, 3),
                             **chunked)
    x = _merge_ffn(x, ya, yb, yc, zg, *lp['ffn'], gf, min(FFN_TILE, x.shape[0]), final, layer)
    return x, (a_s, a_conv, b_s, b_shift, c_h, c_conv)


def kernel(x_prompt, x_sample, state_a_S, state_a_conv, state_b_S, state_b_shift, state_c_h, state_c_conv, norm1_g, w_in, a_conv_w, a_A_log, a_dt_bias, a_norm_g, b_mu, b_w0, b_w_up, b_a0, b_a_up, b_g_up, b_k_k, b_k_a, b_r_k, b_ln_w, b_ln_b, c_conv_w, c_conv_b, c_wa, c_ba, c_wx, c_bx, c_L, w_branch, w_out, norm2_g, w_up, w_down, final_norm_g):
    p = dict(norm1_g=norm1_g, w_in=w_in, a_conv_w=a_conv_w, a_A_log=a_A_log, a_dt_bias=a_dt_bias,
             a_norm_g=a_norm_g, b_mu=b_mu, b_w0=b_w0, b_w_up=b_w_up, b_a0=b_a0, b_a_up=b_a_up,
             b_g_up=b_g_up, b_k_k=b_k_k, b_k_a=b_k_a, b_r_k=b_r_k, b_ln_w=b_ln_w, b_ln_b=b_ln_b,
             c_conv_w=c_conv_w, c_conv_b=c_conv_b, c_wa=c_wa, c_ba=c_ba, c_wx=c_wx, c_bx=c_bx,
             c_L=c_L, w_branch=w_branch, w_out=w_out, norm2_g=norm2_g, w_up=w_up, w_down=w_down)
    bp, tp, _ = x_prompt.shape
    bs, ts, _ = x_sample.shape
    ts_pad = -(-ts // SUBLANE) * SUBLANE
    assert ts_pad == SAMPLE_BLOCKING['c'] and tp % (PROMPT_BLOCKING['c'] * PROMPT_BLOCKING['nc']) == 0
    gf = final_norm_g.reshape(1, 1, D_MODEL)

    xp = x_prompt.reshape(bp * tp, D_MODEL)
    xs = jnp.pad(x_sample, ((0, 0), (0, ts_pad - ts), (0, 0))).reshape(bs * ts_pad, D_MODEL)
    zero = lambda *shape: jnp.zeros((1,) + shape, F32)
    p_states = (zero(bp, A_HEADS, A_DK, A_DV), zero(bp, 3, A_CONV_CH),
                zero(bp, B_HEADS, B_N, B_N), zero(bp, 1, B_IN),
                zero(bp, C_WIDTH), zero(bp, 3, C_WIDTH))
    s_states = (state_a_S, state_a_conv, state_b_S, state_b_shift, state_c_h, state_c_conv)
    lp = _prepare_params(p)
    p_new = s_new = None
    for l in range(DEPTH):
        final = l == DEPTH - 1
        xp, p_new = _layer(xp, p_states, p_new, lp, gf, layer=l, state_layer=0, bsz=bp, t_pad=tp,
                           t_valid=tp, blocking=PROMPT_BLOCKING, final=final)
        xs, s_new = _layer(xs, s_states, s_new, lp, gf, layer=l, state_layer=l, bsz=bs, t_pad=ts_pad,
                           t_valid=ts, blocking=SAMPLE_BLOCKING, final=final)
    squeeze_h = lambda st: st[:4] + (st[4].reshape(DEPTH, -1, C_WIDTH), st[5])
    y_prompt = xp.reshape(bp, tp, D_MODEL)
    y_sample = xs.reshape(bs, ts_pad, D_MODEL)[:, :ts]
    return (y_prompt, y_sample) + squeeze_h(p_new) + squeeze_h(s_new)
```

```python
import functools

import jax
import jax.numpy as jnp
from jax import lax
from jax.experimental import pallas as pl
from jax.experimental.pallas import tpu as pltpu

F32 = jnp.float32
BF16 = jnp.bfloat16

EPS = 1e-6
D_MODEL = 1024
DEPTH = 2
A_HEADS = 4
A_DK = 128
A_DV = 128
A_QK_W = A_HEADS * A_DK
A_V_W = A_HEADS * A_DV
A_CONV_CH = 2 * A_QK_W + A_V_W
B_N = 64
B_HEADS = 8
B_W = B_HEADS * B_N
B_LORA_WA = 128
B_GATE_LORA = 128
B_IN = 3 * B_W + B_LORA_WA + B_GATE_LORA
B_LN_EPS = 64e-5
DECAY_SCALE = 0.6065306597126334
C_WIDTH = 512
C_BLOCKS = 8
C_POW = 8.0
N_BRANCH = 3
D_FF = 4 * D_MODEL

LANE = 128
SUBLANE = 8
CONV_TAPS = 4
CONV_PAD = SUBLANE

ZA_W = A_CONV_CH + A_V_W + LANE
ZA_BA = A_CONV_CH + A_V_W
ZB_W = B_IN
ZC_W = 2 * C_WIDTH
ZG_W = N_BRANCH * D_MODEL
Z_WIDTHS = (ZA_W, ZB_W, ZC_W, ZG_W)
A_IN = A_CONV_CH + A_V_W + 2 * A_HEADS
N_IN = A_IN + ZB_W + ZC_W + ZG_W

VMEM_LIMIT = 60 * 1024 * 1024
PROJ_PIECE = 256
RGLRU_PIECE = 128


def _dot(a, b):
    return jnp.dot(a.astype(BF16), b.astype(BF16), preferred_element_type=F32)


def _dot_nt(a, b):
    return lax.dot_general(a.astype(BF16), b.astype(BF16), (((1,), (1,)), ((), ())),
                           preferred_element_type=F32)


def _dot_tn(a, b):
    return lax.dot_general(a.astype(BF16), b.astype(BF16), (((0,), (0,)), ((), ())),
                           preferred_element_type=F32)


def _sigmoid(x):
    return 1.0 / (1.0 + jnp.exp(-x))


def _silu(x):
    return x * _sigmoid(x)


def _softplus(x):
    return jnp.maximum(x, 0.0) + jnp.log1p(jnp.exp(-jnp.abs(x)))


def _gelu_tanh(x):
    return 0.5 * x * (1.0 + jnp.tanh(0.7978845608028654 * (x + 0.044715 * (x * x * x))))


def _rows(shape):
    return lax.broadcasted_iota(jnp.int32, shape, 0)


def _cols(shape):
    return lax.broadcasted_iota(jnp.int32, shape, 1)


def _rows_in(shape, period):
    r = _rows(shape)
    return r if period >= shape[0] else r & (period - 1)


def _shift_rows(x, s, fill, period):
    return jnp.where(_rows_in(x.shape, period) >= s, pltpu.roll(x, s, 0), fill)


def _bf16_pieces(x, n):
    pieces = []
    for _ in range(n):
        p = x.astype(BF16)
        pieces.append(p)
        x = x - p.astype(F32)
    return pieces


def _cumsum_chunks(x, c):
    tri = (_rows((c, c)) >= _cols((c, c))).astype(BF16)
    pieces = _bf16_pieces(x, 3)
    chunks = [sum(jnp.dot(tri, p[r0:r0 + c], preferred_element_type=F32) for p in pieces)
              for r0 in range(0, x.shape[0], c)]
    return chunks[0] if len(chunks) == 1 else jnp.concatenate(chunks, axis=0)


def _unit_lower_inverses(ns):
    c = ns[0].shape[0]
    eye = (_rows(ns[0].shape) == _cols(ns[0].shape)).astype(F32)
    ts = [eye + n for n in ns]
    ps = list(ns)
    s = 2
    while s < c:
        ps = [_dot(p, p) for p in ps]
        ts = [t + _dot(t, p) for t, p in zip(ts, ps)]
        s *= 2
    return ts


def _unit_lower_solves(ns, xs):
    c = ns[0].shape[0]
    xs = [x + _dot(n, x) for n, x in zip(ns, xs)]
    ps = list(ns)
    s = 2
    while s < c:
        ps = [_dot(p, p) for p in ps]
        xs = [x + _dot(p, x) for p, x in zip(ps, xs)]
        s *= 2
    return xs


def _param_spec(shape, layer):
    nd = len(shape)
    return pl.BlockSpec((None,) + shape, lambda *_: (layer,) + (0,) * nd, pipeline_mode=pl.Buffered(1))


def _group_of_2d(b, t):
    return b


def _stacked_spec(layers, shape, group_of=_group_of_2d):
    nd = len(shape) - 1
    return pl.BlockSpec((layers,) + shape, lambda *g: (0, group_of(*g)) + (0,) * nd)


def _emit_state(last, out_ref, prev_ref, new_ref):
    @pl.when(last)
    def _():
        n_prev = out_ref.shape[0] - 1
        if n_prev:
            out_ref[0:n_prev] = prev_ref[...]
        out_ref[n_prev] = new_ref[...]


def _state_spec(shape, layer, group_of=_group_of_2d):
    nd = len(shape) - 1
    return pl.BlockSpec((None,) + shape, lambda *g: (layer, group_of(*g)) + (0,) * nd)


def _in_proj_kernel(prev_h_ref, prev_conv_ref, prev_aconv_ref, x_ref, g_ref, w_ref, conv0_ref, h0_ref,
                    aconv0_ref, convw_ref, convb_ref, wg_ref, bg_ref, l_ref, aconvw_ref,
                    za_ref, zb_ref, yc_ref, zg_ref, hout_ref, convout_ref, aconvout_ref,
                    w_rest, xpad, h_scr, xpad_a, *, l, ns, nt, valid_rows):
    step = pl.program_id(0)
    t = step % nt
    first = t == 0
    last = t == nt - 1
    @pl.when(step == 0)
    def _():
        w_rest[...] = w_ref[:, A_IN:]

    x = x_ref[...]
    h = x * lax.rsqrt(jnp.mean(x * x, axis=-1, keepdims=True) + EPS) * g_ref[...]
    h = h.astype(BF16)
    proj = lambda ref, off, width: jnp.dot(h, ref[:, off:off + width], preferred_element_type=F32)

    @pl.when(first)
    def _():
        xpad_a[:, CONV_PAD - CONV_TAPS + 1:CONV_PAD, :] = aconv0_ref[...]

    @pl.when(first)
    def _():
        h_scr[...] = h0_ref[...]
        xpad[:, CONV_PAD - CONV_TAPS + 1:CONV_PAD, :] = conv0_ref[...]

    def causal_conv(z, pad, w, cols):
        y = z * w[CONV_TAPS - 1:CONV_TAPS, cols]
        for q in range(ns):
            pad[q, CONV_PAD:CONV_PAD + l, cols] = z[q * l:(q + 1) * l]
        for j in range(CONV_TAPS - 1):
            lo_row = CONV_PAD - CONV_TAPS + 1 + j
            parts = [pad[q, lo_row:lo_row + l, cols] for q in range(ns)]
            view = parts[0] if ns == 1 else jnp.concatenate(parts, axis=0)
            y = y + view * w[j:j + 1, cols]
        return y

    def conv_piece(lo):
        cols = slice(lo, lo + PROJ_PIECE)
        za_ref[:, cols] = _silu(causal_conv(proj(w_ref, lo, PROJ_PIECE), xpad_a, aconvw_ref, cols))

    def gate_piece():
        za_ref[:, A_CONV_CH:] = proj(w_ref, A_CONV_CH, ZA_W - A_CONV_CH)

    def rglru_piece(lo):
        cols = slice(lo, lo + RGLRU_PIECE)
        xc = (causal_conv(proj(w_rest, ZB_W + lo, RGLRU_PIECE), xpad, convw_ref, cols) + convb_ref[:, cols])
        gate_cols = lambda off: slice(off + lo, off + lo + RGLRU_PIECE)
        r = _sigmoid(_dot(xc, wg_ref[cols, gate_cols(0)]) + bg_ref[:, gate_cols(0)])
        i = _sigmoid(_dot(xc, wg_ref[cols, gate_cols(C_WIDTH)]) + bg_ref[:, gate_cols(C_WIDTH)])
        log_a = -C_POW * r * _softplus(-l_ref[:, cols])
        a = jnp.exp(log_a)
        var = -jnp.tanh(log_a) * (a * a + 1.0)
        b = jnp.where(var > 0.0, var * lax.rsqrt(var), 0.0) * (i * xc)
        if valid_rows < l:
            valid = _rows_in(a.shape, l) < valid_rows
            a = jnp.where(valid, a, 1.0)
            b = jnp.where(valid, b, 0.0)
        s = 1
        while s < SUBLANE:
            b = a * _shift_rows(b, s, 0.0, SUBLANE) + b
            a = a * _shift_rows(a, s, 1.0, SUBLANE)
            s *= 2
        gate = _gelu_tanh(proj(w_rest, ZB_W + C_WIDTH + lo, RGLRU_PIECE))
        for q in range(ns):
            carry = h_scr[q, :, cols]
            for r0 in range(q * l, (q + 1) * l, SUBLANE):
                rs = slice(r0, r0 + SUBLANE)
                hseq = a[rs] * carry + b[rs]
                carry = hseq[SUBLANE - 1:SUBLANE, :]
                yc_ref[rs, cols] = hseq * gate[rs]
            h_scr[q, :, cols] = carry

    def project(out_ref, w, off, width):
        def piece(lo):
            hi = min(lo + PROJ_PIECE, width)
            out_ref[:, lo:hi] = proj(w, off + lo, hi - lo)
        return [functools.partial(piece, lo) for lo in range(0, width, PROJ_PIECE)]

    assert A_CONV_CH % PROJ_PIECE == 0 and C_WIDTH % RGLRU_PIECE == 0
    busy = ([functools.partial(rglru_piece, lo) for lo in range(0, C_WIDTH, RGLRU_PIECE)]
            + [functools.partial(conv_piece, lo) for lo in range(0, A_CONV_CH, PROJ_PIECE)])
    plain = ([gate_piece] + project(zb_ref, w_rest, 0, ZB_W) + project(zg_ref, w_rest, ZB_W + ZC_W, ZG_W))
    per_busy = -(-len(plain) // len(busy))
    for k, busy_piece in enumerate(busy):
        busy_piece()
        for plain_piece in plain[k * per_busy:(k + 1) * per_busy]:
            plain_piece()
    for plain_piece in plain[len(busy) * per_busy:]:
        plain_piece()
    _carry_history(xpad, l, valid_rows, last, convout_ref, prev_conv_ref, CONV_TAPS - 1)
    _emit_state(last, hout_ref, prev_h_ref, h_scr)
    _carry_history(xpad_a, l, valid_rows, last, aconvout_ref, prev_aconv_ref, CONV_TAPS - 1)


def _in_proj(x, g, w, conv0, h0, aconv0, prm, aconvw, *, tm, bsz, t_pad, t_valid, layer, state_layer, prev):
    m = x.shape[0]
    l = min(tm, t_pad)
    ns = tm // l
    _, _, valid_rows = _mixer_grid(bsz, t_pad, t_valid, l, ns)
    nt = t_pad // l
    group_of = lambda i: i // nt
    spec = functools.partial(_param_spec, layer=layer)
    kern = functools.partial(_in_proj_kernel, l=l, ns=ns, nt=nt, valid_rows=valid_rows)
    state_shapes = ((1, C_WIDTH), (3, C_WIDTH), (3, A_CONV_CH))
    kern, prev_args, prev_specs, n_prev = _with_prev(kern, prev, state_shapes, ns, group_of)
    tile = lambda wd: pl.BlockSpec((tm, wd), lambda i: (i, 0))
    widths = (ZA_W, ZB_W, C_WIDTH, ZG_W)
    return pl.pallas_call(
        kern,
        grid=(m // tm,),
        in_specs=prev_specs + [
                  tile(D_MODEL), spec((1, D_MODEL)), spec((D_MODEL, N_IN)),
                  _state_spec((ns, 3, C_WIDTH), state_layer, group_of),
                  _state_spec((ns, 1, C_WIDTH), state_layer, group_of),
                  _state_spec((ns, 3, A_CONV_CH), state_layer, group_of),
                  spec((CONV_TAPS, C_WIDTH)), spec((1, C_WIDTH)),
                  spec((C_WIDTH, 2 * C_WIDTH)), spec((1, 2 * C_WIDTH)),
                  spec((1, C_WIDTH)), spec((CONV_TAPS, A_CONV_CH))],
        out_specs=[tile(wd) for wd in widths]
        + [_stacked_spec(n_prev + 1, (ns,) + shp, group_of) for shp in state_shapes],
        out_shape=[jax.ShapeDtypeStruct((m, wd), F32) for wd in widths]
        + [jax.ShapeDtypeStruct((n_prev + 1, bsz) + shp, F32) for shp in state_shapes],
        scratch_shapes=[pltpu.VMEM((D_MODEL, N_IN - A_IN), BF16),
                        pltpu.VMEM((ns, CONV_PAD + l, C_WIDTH), F32),
                        pltpu.VMEM((ns, 1, C_WIDTH), F32),
                        pltpu.VMEM((ns, CONV_PAD + l, A_CONV_CH), F32)],
        compiler_params=pltpu.CompilerParams(dimension_semantics=("arbitrary",),
                                             vmem_limit_bytes=VMEM_LIMIT),
        name="in_proj",
    )(*prev_args, x, g, w, conv0, h0, aconv0, *prm, aconvw)


def _with_history(x, xpad, first, hist_ref, ns, l, depth):
    @pl.when(first)
    def _():
        xpad[:, CONV_PAD - depth:CONV_PAD, :] = hist_ref[...]

    for s in range(ns):
        xpad[s, CONV_PAD:CONV_PAD + l, :] = x[s * l:(s + 1) * l]
    views = []
    for j in range(depth, 0, -1):
        parts = [xpad[s, CONV_PAD - j:CONV_PAD - j + l, :] for s in range(ns)]
        views.append(parts[0] if ns == 1 else jnp.concatenate(parts, axis=0))
    return views


def _carry_history(xpad, l, valid_rows, last, out_ref, prev_ref, depth):
    _emit_state(last, out_ref, prev_ref,
                xpad.at[:, CONV_PAD - depth + valid_rows:CONV_PAD + valid_rows, :])

    nxt = xpad[:, CONV_PAD - depth + l:CONV_PAD + l, :]
    xpad[:, CONV_PAD - depth:CONV_PAD, :] = nxt


def _gdn_kernel(prev_s_ref, za_ref, s0_ref, alog_ref, dt_ref, ng_ref,
                y_ref, sout_ref, s_scr, *, c, nc, ns, nt, valid_rows, group):
    t = pl.program_id(1)
    first = t == 0
    last = t == nt - 1
    l = nc * c
    rows = ns * l

    @pl.when(first)
    def _():
        s_scr[...] = s0_ref[...]

    qkv = za_ref[:, 0:A_CONV_CH]

    ba = za_ref[:, ZA_BA:ZA_BA + LANE]
    ba = jnp.where(_cols(ba.shape) < 2 * A_HEADS, ba, 0.0)
    beta_all = _sigmoid(ba)
    g_all = -jnp.exp(alog_ref[...]) * _softplus(ba + dt_ref[...])
    masked = valid_rows < l
    if masked:
        valid = _rows_in((rows, LANE), l) < valid_rows
        beta_all = jnp.where(valid, beta_all, 0.0)
        g_all = jnp.where(valid, g_all, 0.0)
    gc_all = g_all
    s = 1
    while s < c:
        gc_all = gc_all + _shift_rows(gc_all, s, 0.0, c)
        s *= 2

    rr = _rows((c, c))
    cc = _cols((c, c))
    causal = rr >= cc
    strict = rr > cc
    ng = ng_ref[...]
    pick = (_cols((SUBLANE * A_HEADS, LANE))
            == A_HEADS + _rows((SUBLANE * A_HEADS, LANE)) // SUBLANE).astype(BF16)

    all_chunks = [(s, n) for n in range(nc) for s in range(ns)]
    heads = range(A_HEADS)
    state = {(s, h): s_scr[s, h] for s in range(ns) for h in heads}

    def advance(chunks):
        inst = [(s, n, h) for (s, n) in chunks for h in heads]
        lower, attn, rhs2, kdec, qg, glast = {}, {}, {}, {}, {}, {}
        for (s, n) in chunks:
            r0 = s * l + n * c
            rs = slice(r0, r0 + c)
            gc_rows = sum(lax.dot_general(pick, piece, (((1,), (1,)), ((), ())), preferred_element_type=F32)
                          for piece in _bf16_pieces(gc_all[rs], 3))
            for h in heads:
                q = qkv[rs, h * A_DK:(h + 1) * A_DK]
                k = qkv[rs, A_QK_W + h * A_DK:A_QK_W + (h + 1) * A_DK]
                v = qkv[rs, 2 * A_QK_W + h * A_DV:2 * A_QK_W + (h + 1) * A_DV]
                q = q * lax.rsqrt(jnp.sum(q * q, axis=-1, keepdims=True) + EPS) * (A_DK ** -0.5)
                k = k * lax.rsqrt(jnp.sum(k * k, axis=-1, keepdims=True) + EPS)
                if masked:
                    valid_c = _rows((c, A_DK)) < valid_rows - n * c
                    k = jnp.where(valid_c, k, 0.0)
                    v = jnp.where(valid_c, v, 0.0)
                beta = beta_all[rs, h:h + 1]
                gc = gc_all[rs, A_HEADS + h:A_HEADS + h + 1]
                gc_row = gc_rows[SUBLANE * h:SUBLANE * h + 1, :]
                dec = jnp.where(causal, jnp.exp(jnp.where(causal, gc - gc_row, 0.0)), 0.0)
                kb = k * beta
                egc = jnp.exp(gc)
                sc = _dot_nt(jnp.concatenate([kb, q], axis=0), k)
                key = (s, n, h)
                lower[key] = jnp.where(strict, -sc[:c] * dec, 0.0)
                attn[key] = sc[c:] * dec
                rhs2[key] = jnp.concatenate([v * beta, kb * egc], axis=1)
                gc_last = gc[c - 1:c, :]
                kdec[key] = k * jnp.exp(gc_last - gc)
                glast[key] = jnp.exp(gc_last)
                qg[key] = q * egc
        tinv = dict(zip(inst, _unit_lower_inverses([lower[i] for i in inst])))
        uw = {i: _dot(tinv[i], rhs2[i]) for i in inst}
        ao = {i: _dot(attn[i], uw[i]) for i in inst}
        bm = {i: _dot_tn(kdec[i], uw[i]) for i in inst}
        for (s, n) in chunks:
            r0 = s * l + n * c
            for h in heads:
                i = (s, n, h)
                st = state[(s, h)]
                o = _dot(qg[i] - ao[i][:, A_DV:], st) + ao[i][:, :A_DV]
                state[(s, h)] = st * glast[i] - _dot(bm[i][:, A_DV:], st) + bm[i][:, :A_DV]
                o = o * lax.rsqrt(jnp.mean(o * o, axis=-1, keepdims=True) + EPS)
                z = za_ref[r0:r0 + c, A_CONV_CH + h * A_DV:A_CONV_CH + (h + 1) * A_DV]
                y_ref[r0:r0 + c, h * A_DV:(h + 1) * A_DV] = o * ng * _silu(z)

    for g in range(0, len(all_chunks), group):
        advance(all_chunks[g:g + group])
    for s in range(ns):
        for h in heads:
            s_scr[s, h] = state[(s, h)]

    _emit_state(last, sout_ref, prev_s_ref, s_scr)


def _with_prev(kern, prev, shapes, ns, group_of=_group_of_2d):
    if prev is None:
        return functools.partial(kern, *([None] * len(shapes))), [], [], 0
    n_prev = prev[0].shape[0]
    return kern, list(prev), [_stacked_spec(n_prev, (ns,) + shp, group_of) for shp in shapes], n_prev


def _mixer_grid(bsz, t_pad, t_valid, l, ns):
    assert t_pad % l == 0 and bsz % ns == 0 and (ns == 1 or t_pad == l)
    nt = t_pad // l
    valid_rows = t_valid - (nt - 1) * l
    assert 0 < valid_rows <= l and (nt == 1 or valid_rows == l)
    return (bsz // ns, nt), ns * l, valid_rows


def _gdn(za, s0, alog_row, dt_row, ng, *, bsz, t_pad, t_valid, c, nc, ns, group,
         layer, state_layer, prev):
    grid, rows, valid_rows = _mixer_grid(bsz, t_pad, t_valid, nc * c, ns)
    spec = functools.partial(_param_spec, layer=layer)
    nt = grid[1]
    kern = functools.partial(_gdn_kernel, c=c, nc=nc, ns=ns, nt=nt, valid_rows=valid_rows,
                             group=group)
    state_shapes = ((A_HEADS, A_DK, A_DV),)
    kern, prev_args, prev_specs, n_prev = _with_prev(kern, prev, state_shapes, ns)
    return pl.pallas_call(
        kern,
        grid=grid,
        in_specs=prev_specs + [
                  pl.BlockSpec((rows, ZA_W), lambda b, t: (b * nt + t, 0)),
                  _state_spec((ns, A_HEADS, A_DK, A_DV), state_layer),
                  spec((1, LANE)),
                  spec((1, LANE)),
                  spec((1, A_DV))],
        out_specs=[pl.BlockSpec((rows, A_V_W), lambda b, t: (b * nt + t, 0)),
                   ] + [_stacked_spec(n_prev + 1, (ns,) + shp) for shp in state_shapes],
        out_shape=[jax.ShapeDtypeStruct((bsz * t_pad, A_V_W), F32)]
        + [jax.ShapeDtypeStruct((n_prev + 1, bsz) + shp, F32) for shp in state_shapes],
        scratch_shapes=[pltpu.VMEM((ns, A_HEADS, A_DK, A_DV), F32)],
        compiler_params=pltpu.CompilerParams(dimension_semantics=("arbitrary", "arbitrary"),
                                             vmem_limit_bytes=VMEM_LIMIT),
        name="gdn",
    )(*prev_args, za, s0, alog_row, dt_row, ng)


def _rwkv_kernel(prev_s_ref, prev_shift_ref, zb_ref, shift0_ref, s0_ref, mu_ref, w0_ref, wup_ref, a0_ref,
                 aup_ref, gup_ref, kk_ref, ka_ref, rk_ref, lnw_ref, lnb_ref, hsum_ref,
                 y_ref, sout_ref, shiftout_ref, xpad, s_scr, prep, *, c, nc, ns, nt, valid_rows, group):
    t = pl.program_id(1)
    first = t == 0
    last = t == nt - 1
    l = nc * c
    rows = ns * l

    @pl.when(first)
    def _():
        s_scr[...] = s0_ref[...]

    zb = zb_ref[...]
    (prev,) = _with_history(zb, xpad, first, shift0_ref, ns, l, 1)
    _carry_history(xpad, l, valid_rows, last, shiftout_ref, prev_shift_ref, 1)
    zs = zb + (prev - zb) * mu_ref[...]

    r_all = zs[:, 0:B_W]
    k_all = zs[:, B_W:2 * B_W]
    v_all = zs[:, 2 * B_W:3 * B_W]
    xwa = zs[:, 3 * B_W:3 * B_W + B_LORA_WA]
    xg = zs[:, 3 * B_W + B_LORA_WA:B_IN]
    lw = -DECAY_SCALE * _sigmoid(w0_ref[...] + _dot(jnp.tanh(xwa), wup_ref[...]))
    a_all = _sigmoid(a0_ref[...] + _dot(xwa, aup_ref[...]))
    g_all = _dot(_sigmoid(xg), gup_ref[...])
    kk_all = k_all * kk_ref[...]
    k_all = k_all * (1.0 + (a_all - 1.0) * ka_ref[...])
    masked = valid_rows < l
    if masked:
        valid = _rows_in((rows, B_W), l) < valid_rows
        lw = jnp.where(valid, lw, 0.0)
        kk_all = jnp.where(valid, kk_all, 0.0)
        k_all = jnp.where(valid, k_all, 0.0)
        v_all = jnp.where(valid, v_all, 0.0)
    gc_all = _cumsum_chunks(lw, c)

    def head_sums(x, pieces=1):
        return sum(jnp.dot(p, hsum_ref[...], preferred_element_type=F32) for p in _bf16_pieces(x, pieces))

    kkn_all = kk_all * lax.rsqrt(head_sums(kk_all * kk_all, pieces=2) + EPS)
    bonus_all = head_sums(r_all * k_all * rk_ref[...]) * v_all

    rr = _rows((2 * c, 2 * c))
    cc = _cols((2 * c, 2 * c))
    tt = jnp.where(rr >= c, rr - c, rr)
    ss = jnp.where(cc >= c, cc - c, cc)
    mask2 = jnp.logical_or(tt > ss, jnp.logical_and(rr >= c, tt == ss))

    prep_names = ('r', 'k', 'v', 'kkn', 'a', 'lw', 'gc', 'bonus', 'g')
    for idx, val in enumerate((r_all, k_all, v_all, kkn_all, a_all, lw, gc_all, bonus_all, g_all)):
        prep[idx] = val
    heads = range(B_HEADS)
    sls = [slice(h * B_N, (h + 1) * B_N) for h in heads]

    def at_rows(name, rs):
        return prep[prep_names.index(name), rs, :]

    def advance(g, carry):
        members = range(group)
        inst = [(j, h) for j in members for h in heads]
        amat, rhs_uw, r_abs, tails, e_last, vs, rows_of, seq_of = {}, {}, {}, {}, {}, {}, {}, {}
        for j in members:
            q = g * group + j
            rs = pl.ds(pl.multiple_of(q * c, c), c)
            rows_of[j] = rs
            seq_of[j] = 0 if ns == 1 else q
            gc = at_rows('gc', rs)
            lwc = at_rows('lw', rs)
            gmid = gc[c // 2:c // 2 + 1, :]
            glast = gc[c - 1:c, :]
            e_last[j] = jnp.exp(glast)
            r = at_rows('r', rs)
            k = at_rows('k', rs)
            v = at_rows('v', rs)
            alpha = -at_rows('kkn', rs)
            bet = at_rows('kkn', rs) * at_rows('a', rs)
            e_inv = jnp.exp(gmid - gc)
            e_tail = jnp.exp(glast - gc)
            lhs_a = alpha * jnp.exp(gc - lwc - gmid)
            lhs_r = r * jnp.exp(gc - gmid)
            rhs_b = bet * e_inv
            rhs_k = k * e_inv
            abs_a = alpha * jnp.exp(gc - lwc)
            abs_r = r * jnp.exp(gc)
            tail_b = bet * e_tail
            tail_k = k * e_tail
            for h in heads:
                sl = sls[h]
                lhs = jnp.concatenate([lhs_a[:, sl], lhs_r[:, sl]], axis=0)
                rhs = jnp.concatenate([rhs_b[:, sl], rhs_k[:, sl]], axis=0)
                key = (j, h)
                amat[key] = jnp.where(mask2, _dot_nt(lhs, rhs), 0.0)
                rhs_uw[key] = abs_a[:, sl]
                r_abs[key] = abs_r[:, sl]
                tails[key] = jnp.concatenate([tail_b[:, sl], tail_k[:, sl]], axis=0)
                vs[key] = v[:, sl]

        from_v = {i: _dot(amat[i][:, c:], vs[i]) for i in inst}
        uw = dict(zip(inst, _unit_lower_solves(
            [amat[i][:c, :c] for i in inst],
            [jnp.concatenate([from_v[i][:c], rhs_uw[i]], axis=1) for i in inst])))
        ro = {i: _dot(amat[i][c:, :c], uw[i]) for i in inst}
        zeros = jnp.zeros((c, B_N), F32)
        both = {i: _dot_tn(jnp.concatenate([uw[i], jnp.concatenate([vs[i], zeros], axis=1)], axis=0), tails[i])
                for i in inst}
        bb = {i: both[i][:B_N] for i in inst}
        mp = {i: both[i][B_N:] for i in inst}
        state = {}
        outs = []
        for j in members:
            per_head = []
            for h in heads:
                i = (j, h)
                skey = (0 if ns == 1 else j, h)
                st = state[skey] if skey in state else s_scr[seq_of[j], h]
                per_head.append(_dot_nt(r_abs[i] + ro[i][:, B_N:], st) + from_v[i][c:] + ro[i][:, :B_N])
                state[skey] = st * e_last[j][:, sls[h]] + _dot(st, mp[i]) + bb[i]
            outs.append(jnp.concatenate(per_head, axis=1))
        for j, o in zip(members, outs):
            d = o - head_sums(o) * (1.0 / B_N)
            var = head_sums(d * d) * (1.0 / B_N)
            o = d * lax.rsqrt(var + B_LN_EPS) * lnw_ref[...] + lnb_ref[...]
            y_ref[rows_of[j], :] = (o + at_rows('bonus', rows_of[j])) * at_rows('g', rows_of[j])
        for (sk, h), st in state.items():
            s_scr[seq_of[sk], h] = st
        return carry

    assert (ns * nc) % group == 0 and (ns == 1 or nc == 1)
    lax.fori_loop(0, ns * nc // group, advance, 0)

    _emit_state(last, sout_ref, prev_s_ref, s_scr)


def _rwkv(zb, shift0, s0, prm, *, bsz, t_pad, t_valid, c, nc, ns, group, layer, state_layer, prev):
    grid, rows, valid_rows = _mixer_grid(bsz, t_pad, t_valid, nc * c, ns)
    nt = grid[1]
    kern = functools.partial(_rwkv_kernel, c=c, nc=nc, ns=ns, nt=nt, valid_rows=valid_rows,
                             group=group)
    spec = functools.partial(_param_spec, layer=layer)
    row = lambda wd: spec((1, wd))
    state_shapes = ((B_HEADS, B_N, B_N), (1, B_IN))
    kern, prev_args, prev_specs, n_prev = _with_prev(kern, prev, state_shapes, ns)
    return pl.pallas_call(
        kern,
        grid=grid,
        in_specs=prev_specs + [
                  pl.BlockSpec((rows, ZB_W), lambda b, t: (b * nt + t, 0)),
                  _state_spec((ns, 1, B_IN), state_layer),
                  _state_spec((ns, B_HEADS, B_N, B_N), state_layer),
                  row(B_IN), row(B_W), spec((B_LORA_WA, B_W)), row(B_W),
                  spec((B_LORA_WA, B_W)), spec((B_GATE_LORA, B_W)),
                  row(B_W), row(B_W), row(B_W), row(B_W), row(B_W), _param_spec((B_W, B_W), 0)],
        out_specs=[pl.BlockSpec((rows, B_W), lambda b, t: (b * nt + t, 0)),
                   ] + [_stacked_spec(n_prev + 1, (ns,) + shp) for shp in state_shapes],
        out_shape=[jax.ShapeDtypeStruct((bsz * t_pad, B_W), F32)]
        + [jax.ShapeDtypeStruct((n_prev + 1, bsz) + shp, F32) for shp in state_shapes],
        scratch_shapes=[pltpu.VMEM((ns, CONV_PAD + nc * c, B_IN), F32),
                        pltpu.VMEM((ns, B_HEADS, B_N, B_N), F32),
                        pltpu.VMEM((9, rows, B_W), F32)],
        compiler_params=pltpu.CompilerParams(dimension_semantics=("arbitrary", "arbitrary"),
                                             vmem_limit_bytes=VMEM_LIMIT),
        name="rwkv",
    )(*prev_args, zb, shift0, s0, *prm)


FF_CHUNK = 1024


def _merge_ffn_kernel(x_ref, ya_ref, yb_ref, yc_ref, zg_ref, wbr_ref, wout_ref, g2_ref,
                      wup_ref, wdown_ref, gf_ref, o_ref, *, final):
    merged = None
    for n, y_ref in enumerate((ya_ref, yb_ref, yc_ref)):
        proj = _dot(y_ref[...], wbr_ref[n])
        term = _sigmoid(zg_ref[:, n * D_MODEL:(n + 1) * D_MODEL]) * proj
        merged = term if merged is None else merged + term
    x = x_ref[...] + _dot(merged, wout_ref[...])
    h = x * lax.rsqrt(jnp.mean(x * x, axis=-1, keepdims=True) + EPS) * g2_ref[...]
    h = h.astype(BF16)
    acc = x
    for j in range(0, D_FF, FF_CHUNK):
        up = jnp.dot(h, wup_ref[:, j:j + FF_CHUNK], preferred_element_type=F32)
        acc = acc + _dot(jnp.square(jnp.maximum(up, 0.0)), wdown_ref[j:j + FF_CHUNK, :])
    if final:
        acc = acc * lax.rsqrt(jnp.mean(acc * acc, axis=-1, keepdims=True) + EPS) * gf_ref[...]
    o_ref[...] = acc


def _merge_ffn(x, ya, yb, yc, zg, wbr, wout, g2, wup, wdown, gf, tm, final, layer):
    m = x.shape[0]
    spec = functools.partial(_param_spec, layer=layer)
    tile = lambda wd: pl.BlockSpec((tm, wd), lambda i: (i, 0))
    return pl.pallas_call(
        functools.partial(_merge_ffn_kernel, final=final),
        grid=(m // tm,),
        in_specs=[tile(D_MODEL), tile(A_V_W), tile(B_W), tile(C_WIDTH), tile(ZG_W),
                  spec((N_BRANCH, A_V_W, D_MODEL)), spec((D_MODEL, D_MODEL)),
                  spec((1, D_MODEL)), spec((D_MODEL, D_FF)),
                  spec((D_FF, D_MODEL)), _param_spec((1, D_MODEL), 0)],
        out_specs=tile(D_MODEL),
        out_shape=jax.ShapeDtypeStruct((m, D_MODEL), F32),
        compiler_params=pltpu.CompilerParams(dimension_semantics=("arbitrary",),
                                             vmem_limit_bytes=VMEM_LIMIT),
        name="merge_ffn",
    )(x, ya, yb, yc, zg, wbr, wout, g2, wup, wdown, gf)


def _block_diag(w):
    g, n, _ = w.shape
    eye = jnp.eye(g, dtype=w.dtype)
    return (eye[:, None, :, None] * w[:, :, None, :]).reshape(g * n, g * n)


def _prepare_params(p):
    depth = p['w_in'].shape[0]
    row = lambda a: a.reshape(depth, 1, -1).astype(F32)
    w_in = p['w_in'].astype(BF16)
    lane_row = lambda a: jnp.zeros((depth, 1, LANE), F32).at[:, 0, A_HEADS:2 * A_HEADS].set(a)
    zeros_lora = jnp.zeros((depth, B_LORA_WA // 2, B_W), F32)
    block_diag = jax.vmap(_block_diag)
    gdn = (p['a_conv_w'], lane_row(p['a_A_log']), lane_row(p['a_dt_bias']), row(p['a_norm_g']))
    rwkv = (row(p['b_mu']), row(p['b_w0']),
            jnp.concatenate([p['b_w_up'], zeros_lora], axis=1).astype(BF16),
            row(p['b_a0']),
            jnp.concatenate([zeros_lora, p['b_a_up']], axis=1).astype(BF16),
            p['b_g_up'].astype(BF16),
            row(p['b_k_k']), row(p['b_k_a']), row(p['b_r_k']), row(p['b_ln_w']), row(p['b_ln_b']),
            jnp.kron(jnp.eye(B_HEADS, dtype=F32), jnp.ones((B_N, B_N), F32)).astype(BF16)[None])
    rglru = (p['c_conv_w'], row(p['c_conv_b']),
             jnp.concatenate([block_diag(p['c_wa']), block_diag(p['c_wx'])], axis=2).astype(BF16),
             jnp.concatenate([row(p['c_ba']), row(p['c_bx'])], axis=2), row(p['c_L']))
    ffn = (p['w_branch'].astype(BF16), p['w_out'].astype(BF16), row(p['norm2_g']),
           p['w_up'].astype(BF16), p['w_down'].astype(BF16))
    return dict(norm1_g=row(p['norm1_g']), w_in=w_in, gdn=gdn, rwkv=rwkv, rglru=rglru, ffn=ffn)


PROMPT_BLOCKING = dict(c=64, nc=16, ns=1, gdn_group=4, rwkv_group=4, in_rows=256)
SAMPLE_BLOCKING = dict(c=SUBLANE, nc=1, ns=8, gdn_group=8, rwkv_group=4, in_rows=256)
FFN_TILE = 512


def _layer(x, states, new_states, lp, gf, *, layer, state_layer, bsz, t_pad, t_valid, blocking, final):
    a_s, a_conv, b_s, b_shift, c_h, c_conv = states
    prev = (lambda *idx: None) if new_states is None else (lambda *idx: tuple(new_states[i] for i in idx))
    kw = dict(bsz=bsz, t_pad=t_pad, t_valid=t_valid, layer=layer, state_layer=state_layer)
    chunked = dict(c=blocking['c'], nc=blocking['nc'], ns=blocking['ns'], **kw)
    a_convw, *gdn_prm = lp['gdn']
    za, zb, yc, zg, c_h, c_conv, a_conv = _in_proj(
        x, lp['norm1_g'], lp['w_in'], c_conv, c_h.reshape(-1, bsz, 1, C_WIDTH), a_conv, lp['rglru'], a_convw,
        tm=blocking['in_rows'], prev=prev(4, 5, 1), **kw)
    ya, a_s = _gdn(za, a_s, *gdn_prm, group=blocking['gdn_group'], prev=prev(0), **chunked)
    yb, b_s, b_shift = _rwkv(zb, b_shift, b_s, lp['rwkv'], group=blocking['rwkv_group'], prev=prev(2, 3),
                             **chunked)
    x = _merge_ffn(x, ya, yb, yc, zg, *lp['ffn'], gf, min(FFN_TILE, x.shape[0]), final, layer)
    return x, (a_s, a_conv, b_s, b_shift, c_h, c_conv)


def kernel(x_prompt, x_sample, state_a_S, state_a_conv, state_b_S, state_b_shift, state_c_h, state_c_conv, norm1_g, w_in, a_conv_w, a_A_log, a_dt_bias, a_norm_g, b_mu, b_w0, b_w_up, b_a0, b_a_up, b_g_up, b_k_k, b_k_a, b_r_k, b_ln_w, b_ln_b, c_conv_w, c_conv_b, c_wa, c_ba, c_wx, c_bx, c_L, w_branch, w_out, norm2_g, w_up, w_down, final_norm_g):
    p = dict(norm1_g=norm1_g, w_in=w_in, a_conv_w=a_conv_w, a_A_log=a_A_log, a_dt_bias=a_dt_bias,
             a_norm_g=a_norm_g, b_mu=b_mu, b_w0=b_w0, b_w_up=b_w_up, b_a0=b_a0, b_a_up=b_a_up,
             b_g_up=b_g_up, b_k_k=b_k_k, b_k_a=b_k_a, b_r_k=b_r_k, b_ln_w=b_ln_w, b_ln_b=b_ln_b,
             c_conv_w=c_conv_w, c_conv_b=c_conv_b, c_wa=c_wa, c_ba=c_ba, c_wx=c_wx, c_bx=c_bx,
             c_L=c_L, w_branch=w_branch, w_out=w_out, norm2_g=norm2_g, w_up=w_up, w_down=w_down)
    bp, tp, _ = x_prompt.shape
    bs, ts, _ = x_sample.shape
    ts_pad = -(-ts // SUBLANE) * SUBLANE
    assert ts_pad == SAMPLE_BLOCKING['c'] and tp % (PROMPT_BLOCKING['c'] * PROMPT_BLOCKING['nc']) == 0
    gf = final_norm_g.reshape(1, 1, D_MODEL)

    xp = x_prompt.reshape(bp * tp, D_MODEL)
    xs = jnp.pad(x_sample, ((0, 0), (0, ts_pad - ts), (0, 0))).reshape(bs * ts_pad, D_MODEL)
    zero = lambda *shape: jnp.zeros((1,) + shape, F32)
    p_states = (zero(bp, A_HEADS, A_DK, A_DV), zero(bp, 3, A_CONV_CH),
                zero(bp, B_HEADS, B_N, B_N), zero(bp, 1, B_IN),
                zero(bp, C_WIDTH), zero(bp, 3, C_WIDTH))
    s_states = (state_a_S, state_a_conv, state_b_S, state_b_shift, state_c_h, state_c_conv)
    lp = _prepare_params(p)
    p_new = s_new = None
    for l in range(DEPTH):
        final = l == DEPTH - 1
        xp, p_new = _layer(xp, p_states, p_new, lp, gf, layer=l, state_layer=0, bsz=bp, t_pad=tp,
                           t_valid=tp, blocking=PROMPT_BLOCKING, final=final)
        xs, s_new = _layer(xs, s_states, s_new, lp, gf, layer=l, state_layer=l, bsz=bs, t_pad=ts_pad,
                           t_valid=ts, blocking=SAMPLE_BLOCKING, final=final)
    squeeze_h = lambda st: st[:4] + (st[4].reshape(DEPTH, -1, C_WIDTH), st[5])
    y_prompt = xp.reshape(bp, tp, D_MODEL)
    y_sample = xs.reshape(bs, ts_pad, D_MODEL)[:, :ts]
    return (y_prompt, y_sample) + squeeze_h(p_new) + squeeze_h(s_new)
```

```python
import functools

import jax
import jax.numpy as jnp
from jax import lax
from jax.experimental import pallas as pl
from jax.experimental.pallas import tpu as pltpu

F32 = jnp.float32
BF16 = jnp.bfloat16

EPS = 1e-6
D_MODEL = 1024
DEPTH = 2
A_HEADS = 4
A_DK = 128
A_DV = 128
A_QK_W = A_HEADS * A_DK
A_V_W = A_HEADS * A_DV
A_CONV_CH = 2 * A_QK_W + A_V_W
B_N = 64
B_HEADS = 8
B_W = B_HEADS * B_N
B_LORA_WA = 128
B_GATE_LORA = 128
B_IN = 3 * B_W + B_LORA_WA + B_GATE_LORA
B_LN_EPS = 64e-5
DECAY_SCALE = 0.6065306597126334
C_WIDTH = 512
C_BLOCKS = 8
C_POW = 8.0
N_BRANCH = 3
D_FF = 4 * D_MODEL

LANE = 128
SUBLANE = 8
CONV_TAPS = 4
CONV_PAD = SUBLANE

ZA_W = A_CONV_CH + A_V_W + LANE
ZA_BA = A_CONV_CH + A_V_W
ZB_W = B_IN
ZC_W = 2 * C_WIDTH
ZG_W = N_BRANCH * D_MODEL
Z_WIDTHS = (ZA_W, ZB_W, ZC_W, ZG_W)
A_IN = A_CONV_CH + A_V_W + 2 * A_HEADS
N_IN = A_IN + ZB_W + ZC_W + ZG_W

VMEM_LIMIT = 60 * 1024 * 1024
PROJ_PIECE = 256
RGLRU_PIECE = 128


def _dot(a, b):
    return jnp.dot(a.astype(BF16), b.astype(BF16), preferred_element_type=F32)


def _dot_nt(a, b):
    return lax.dot_general(a.astype(BF16), b.astype(BF16), (((1,), (1,)), ((), ())),
                           preferred_element_type=F32)


def _dot_tn(a, b):
    return lax.dot_general(a.astype(BF16), b.astype(BF16), (((0,), (0,)), ((), ())),
                           preferred_element_type=F32)


def _sigmoid(x):
    return 1.0 / (1.0 + jnp.exp(-x))


def _silu(x):
    return x * _sigmoid(x)


def _softplus(x):
    return jnp.maximum(x, 0.0) + jnp.log1p(jnp.exp(-jnp.abs(x)))


def _gelu_tanh(x):
    return 0.5 * x * (1.0 + jnp.tanh(0.7978845608028654 * (x + 0.044715 * (x * x * x))))


def _rows(shape):
    return lax.broadcasted_iota(jnp.int32, shape, 0)


def _cols(shape):
    return lax.broadcasted_iota(jnp.int32, shape, 1)


def _rows_in(shape, period):
    r = _rows(shape)
    return r if period >= shape[0] else r & (period - 1)


def _shift_rows(x, s, fill, period):
    return jnp.where(_rows_in(x.shape, period) >= s, pltpu.roll(x, s, 0), fill)


def _bf16_pieces(x, n):
    pieces = []
    for _ in range(n):
        p = x.astype(BF16)
        pieces.append(p)
        x = x - p.astype(F32)
    return pieces


def _cumsum_chunks(x, c):
    tri = (_rows((c, c)) >= _cols((c, c))).astype(BF16)
    pieces = _bf16_pieces(x, 3)
    chunks = [sum(jnp.dot(tri, p[r0:r0 + c], preferred_element_type=F32) for p in pieces)
              for r0 in range(0, x.shape[0], c)]
    return chunks[0] if len(chunks) == 1 else jnp.concatenate(chunks, axis=0)


def _unit_lower_inverses(ns):
    c = ns[0].shape[0]
    eye = (_rows(ns[0].shape) == _cols(ns[0].shape)).astype(F32)
    ts = [eye + n for n in ns]
    ps = list(ns)
    s = 2
    while s < c:
        ps = [_dot(p, p) for p in ps]
        ts = [t + _dot(t, p) for t, p in zip(ts, ps)]
        s *= 2
    return ts


def _unit_lower_solves(ns, xs):
    c = ns[0].shape[0]
    xs = [x + _dot(n, x) for n, x in zip(ns, xs)]
    ps = list(ns)
    s = 2
    while s < c:
        ps = [_dot(p, p) for p in ps]
        xs = [x + _dot(p, x) for p, x in zip(ps, xs)]
        s *= 2
    return xs


def _param_spec(shape, layer):
    nd = len(shape)
    return pl.BlockSpec((None,) + shape, lambda *_: (layer,) + (0,) * nd, pipeline_mode=pl.Buffered(1))


def _group_of_2d(b, t):
    return b


def _stacked_spec(layers, shape, group_of=_group_of_2d):
    nd = len(shape) - 1
    return pl.BlockSpec((layers,) + shape, lambda *g: (0, group_of(*g)) + (0,) * nd)


def _emit_state(last, out_ref, prev_ref, new_ref):
    @pl.when(last)
    def _():
        n_prev = out_ref.shape[0] - 1
        if n_prev:
            out_ref[0:n_prev] = prev_ref[...]
        out_ref[n_prev] = new_ref[...]


def _state_spec(shape, layer, group_of=_group_of_2d):
    nd = len(shape) - 1
    return pl.BlockSpec((None,) + shape, lambda *g: (layer, group_of(*g)) + (0,) * nd)


def _in_proj_kernel(prev_h_ref, prev_conv_ref, prev_aconv_ref, x_ref, g_ref, w_ref, conv0_ref, h0_ref,
                    aconv0_ref, convw_ref, convb_ref, wg_ref, bg_ref, l_ref, aconvw_ref,
                    za_ref, zb_ref, yc_ref, zg_ref, hout_ref, convout_ref, aconvout_ref,
                    w_rest, xpad, h_scr, xpad_a, *, l, ns, nt, valid_rows):
    step = pl.program_id(0)
    t = step % nt
    first = t == 0
    last = t == nt - 1
    @pl.when(step == 0)
    def _():
        w_rest[...] = w_ref[:, A_IN:]

    x = x_ref[...]
    h = x * lax.rsqrt(jnp.mean(x * x, axis=-1, keepdims=True) + EPS) * g_ref[...]
    h = h.astype(BF16)
    proj = lambda ref, off, width: jnp.dot(h, ref[:, off:off + width], preferred_element_type=F32)

    @pl.when(first)
    def _():
        xpad_a[:, CONV_PAD - CONV_TAPS + 1:CONV_PAD, :] = aconv0_ref[...]

    @pl.when(first)
    def _():
        h_scr[...] = h0_ref[...]
        xpad[:, CONV_PAD - CONV_TAPS + 1:CONV_PAD, :] = conv0_ref[...]

    def causal_conv(z, pad, w, cols):
        y = z * w[CONV_TAPS - 1:CONV_TAPS, cols]
        for q in range(ns):
            pad[q, CONV_PAD:CONV_PAD + l, cols] = z[q * l:(q + 1) * l]
        for j in range(CONV_TAPS - 1):
            lo_row = CONV_PAD - CONV_TAPS + 1 + j
            parts = [pad[q, lo_row:lo_row + l, cols] for q in range(ns)]
            view = parts[0] if ns == 1 else jnp.concatenate(parts, axis=0)
            y = y + view * w[j:j + 1, cols]
        return y

    def conv_piece(lo):
        cols = slice(lo, lo + PROJ_PIECE)
        za_ref[:, cols] = _silu(causal_conv(proj(w_ref, lo, PROJ_PIECE), xpad_a, aconvw_ref, cols))

    def gate_piece():
        za_ref[:, A_CONV_CH:] = proj(w_ref, A_CONV_CH, ZA_W - A_CONV_CH)

    def rglru_piece(lo):
        cols = slice(lo, lo + RGLRU_PIECE)
        xc = (causal_conv(proj(w_rest, ZB_W + lo, RGLRU_PIECE), xpad, convw_ref, cols) + convb_ref[:, cols])
        gate_cols = lambda off: slice(off + lo, off + lo + RGLRU_PIECE)
        r = _sigmoid(_dot(xc, wg_ref[cols, gate_cols(0)]) + bg_ref[:, gate_cols(0)])
        i = _sigmoid(_dot(xc, wg_ref[cols, gate_cols(C_WIDTH)]) + bg_ref[:, gate_cols(C_WIDTH)])
        log_a = -C_POW * r * _softplus(-l_ref[:, cols])
        a = jnp.exp(log_a)
        var = -jnp.tanh(log_a) * (a * a + 1.0)
        b = jnp.where(var > 0.0, var * lax.rsqrt(var), 0.0) * (i * xc)
        if valid_rows < l:
            valid = _rows_in(a.shape, l) < valid_rows
            a = jnp.where(valid, a, 1.0)
            b = jnp.where(valid, b, 0.0)
        s = 1
        while s < SUBLANE:
            b = a * _shift_rows(b, s, 0.0, SUBLANE) + b
            a = a * _shift_rows(a, s, 1.0, SUBLANE)
            s *= 2
        gate = _gelu_tanh(proj(w_rest, ZB_W + C_WIDTH + lo, RGLRU_PIECE))
        for q in range(ns):
            carry = h_scr[q, :, cols]
            for r0 in range(q * l, (q + 1) * l, SUBLANE):
                rs = slice(r0, r0 + SUBLANE)
                hseq = a[rs] * carry + b[rs]
                carry = hseq[SUBLANE - 1:SUBLANE, :]
                yc_ref[rs, cols] = hseq * gate[rs]
            h_scr[q, :, cols] = carry

    def project(out_ref, w, off, width):
        def piece(lo):
            hi = min(lo + PROJ_PIECE, width)
            out_ref[:, lo:hi] = proj(w, off + lo, hi - lo)
        return [functools.partial(piece, lo) for lo in range(0, width, PROJ_PIECE)]

    assert A_CONV_CH % PROJ_PIECE == 0 and C_WIDTH % RGLRU_PIECE == 0
    busy = ([functools.partial(rglru_piece, lo) for lo in range(0, C_WIDTH, RGLRU_PIECE)]
            + [functools.partial(conv_piece, lo) for lo in range(0, A_CONV_CH, PROJ_PIECE)])
    plain = ([gate_piece] + project(zb_ref, w_rest, 0, ZB_W) + project(zg_ref, w_rest, ZB_W + ZC_W, ZG_W))
    per_busy = -(-len(plain) // len(busy))
    for k, busy_piece in enumerate(busy):
        busy_piece()
        for plain_piece in plain[k * per_busy:(k + 1) * per_busy]:
            plain_piece()
    for plain_piece in plain[len(busy) * per_busy:]:
        plain_piece()
    _carry_history(xpad, l, valid_rows, last, convout_ref, prev_conv_ref, CONV_TAPS - 1)
    _emit_state(last, hout_ref, prev_h_ref, h_scr)
    _carry_history(xpad_a, l, valid_rows, last, aconvout_ref, prev_aconv_ref, CONV_TAPS - 1)


def _in_proj(x, g, w, conv0, h0, aconv0, prm, aconvw, *, tm, bsz, t_pad, t_valid, layer, state_layer, prev):
    m = x.shape[0]
    l = min(tm, t_pad)
    ns = tm // l
    _, _, valid_rows = _mixer_grid(bsz, t_pad, t_valid, l, ns)
    nt = t_pad // l
    group_of = lambda i: i // nt
    spec = functools.partial(_param_spec, layer=layer)
    kern = functools.partial(_in_proj_kernel, l=l, ns=ns, nt=nt, valid_rows=valid_rows)
    state_shapes = ((1, C_WIDTH), (3, C_WIDTH), (3, A_CONV_CH))
    kern, prev_args, prev_specs, n_prev = _with_prev(kern, prev, state_shapes, ns, group_of)
    tile = lambda wd: pl.BlockSpec((tm, wd), lambda i: (i, 0))
    widths = (ZA_W, ZB_W, C_WIDTH, ZG_W)
    return pl.pallas_call(
        kern,
        grid=(m // tm,),
        in_specs=prev_specs + [
                  tile(D_MODEL), spec((1, D_MODEL)), spec((D_MODEL, N_IN)),
                  _state_spec((ns, 3, C_WIDTH), state_layer, group_of),
                  _state_spec((ns, 1, C_WIDTH), state_layer, group_of),
                  _state_spec((ns, 3, A_CONV_CH), state_layer, group_of),
                  spec((CONV_TAPS, C_WIDTH)), spec((1, C_WIDTH)),
                  spec((C_WIDTH, 2 * C_WIDTH)), spec((1, 2 * C_WIDTH)),
                  spec((1, C_WIDTH)), spec((CONV_TAPS, A_CONV_CH))],
        out_specs=[tile(wd) for wd in widths]
        + [_stacked_spec(n_prev + 1, (ns,) + shp, group_of) for shp in state_shapes],
        out_shape=[jax.ShapeDtypeStruct((m, wd), F32) for wd in widths]
        + [jax.ShapeDtypeStruct((n_prev + 1, bsz) + shp, F32) for shp in state_shapes],
        scratch_shapes=[pltpu.VMEM((D_MODEL, N_IN - A_IN), BF16),
                        pltpu.VMEM((ns, CONV_PAD + l, C_WIDTH), F32),
                        pltpu.VMEM((ns, 1, C_WIDTH), F32),
                        pltpu.VMEM((ns, CONV_PAD + l, A_CONV_CH), F32)],
        compiler_params=pltpu.CompilerParams(dimension_semantics=("arbitrary",),
                                             vmem_limit_bytes=VMEM_LIMIT),
        name="in_proj",
    )(*prev_args, x, g, w, conv0, h0, aconv0, *prm, aconvw)


def _with_history(x, xpad, first, hist_ref, ns, l, depth):
    @pl.when(first)
    def _():
        xpad[:, CONV_PAD - depth:CONV_PAD, :] = hist_ref[...]

    for s in range(ns):
        xpad[s, CONV_PAD:CONV_PAD + l, :] = x[s * l:(s + 1) * l]
    views = []
    for j in range(depth, 0, -1):
        parts = [xpad[s, CONV_PAD - j:CONV_PAD - j + l, :] for s in range(ns)]
        views.append(parts[0] if ns == 1 else jnp.concatenate(parts, axis=0))
    return views


def _carry_history(xpad, l, valid_rows, last, out_ref, prev_ref, depth):
    _emit_state(last, out_ref, prev_ref,
                xpad.at[:, CONV_PAD - depth + valid_rows:CONV_PAD + valid_rows, :])

    nxt = xpad[:, CONV_PAD - depth + l:CONV_PAD + l, :]
    xpad[:, CONV_PAD - depth:CONV_PAD, :] = nxt


def _gdn_kernel(prev_s_ref, za_ref, s0_ref, alog_ref, dt_ref, ng_ref,
                y_ref, sout_ref, s_scr, *, c, nc, ns, nt, valid_rows, group):
    t = pl.program_id(1)
    first = t == 0
    last = t == nt - 1
    l = nc * c
    rows = ns * l

    @pl.when(first)
    def _():
        s_scr[...] = s0_ref[...]

    qkv = za_ref[:, 0:A_CONV_CH]

    ba = za_ref[:, ZA_BA:ZA_BA + LANE]
    ba = jnp.where(_cols(ba.shape) < 2 * A_HEADS, ba, 0.0)
    beta_all = _sigmoid(ba)
    g_all = -jnp.exp(alog_ref[...]) * _softplus(ba + dt_ref[...])
    masked = valid_rows < l
    if masked:
        valid = _rows_in((rows, LANE), l) < valid_rows
        beta_all = jnp.where(valid, beta_all, 0.0)
        g_all = jnp.where(valid, g_all, 0.0)
    gc_all = g_all
    s = 1
    while s < c:
        gc_all = gc_all + _shift_rows(gc_all, s, 0.0, c)
        s *= 2

    rr = _rows((c, c))
    cc = _cols((c, c))
    causal = rr >= cc
    strict = rr > cc
    ng = ng_ref[...]
    pick = (_cols((SUBLANE * A_HEADS, LANE))
            == A_HEADS + _rows((SUBLANE * A_HEADS, LANE)) // SUBLANE).astype(BF16)

    all_chunks = [(s, n) for n in range(nc) for s in range(ns)]
    heads = range(A_HEADS)
    state = {(s, h): s_scr[s, h] for s in range(ns) for h in heads}

    def advance(chunks):
        inst = [(s, n, h) for (s, n) in chunks for h in heads]
        lower, attn, rhs2, kdec, qg, glast = {}, {}, {}, {}, {}, {}
        for (s, n) in chunks:
            r0 = s * l + n * c
            rs = slice(r0, r0 + c)
            gc_rows = sum(lax.dot_general(pick, piece, (((1,), (1,)), ((), ())), preferred_element_type=F32)
                          for piece in _bf16_pieces(gc_all[rs], 3))
            for h in heads:
                q = qkv[rs, h * A_DK:(h + 1) * A_DK]
                k = qkv[rs, A_QK_W + h * A_DK:A_QK_W + (h + 1) * A_DK]
                v = qkv[rs, 2 * A_QK_W + h * A_DV:2 * A_QK_W + (h + 1) * A_DV]
                q = q * lax.rsqrt(jnp.sum(q * q, axis=-1, keepdims=True) + EPS) * (A_DK ** -0.5)
                k = k * lax.rsqrt(jnp.sum(k * k, axis=-1, keepdims=True) + EPS)
                if masked:
                    valid_c = _rows((c, A_DK)) < valid_rows - n * c
                    k = jnp.where(valid_c, k, 0.0)
                    v = jnp.where(valid_c, v, 0.0)
                beta = beta_all[rs, h:h + 1]
                gc = gc_all[rs, A_HEADS + h:A_HEADS + h + 1]
                gc_row = gc_rows[SUBLANE * h:SUBLANE * h + 1, :]
                dec = jnp.where(causal, jnp.exp(jnp.where(causal, gc - gc_row, 0.0)), 0.0)
                kb = k * beta
                egc = jnp.exp(gc)
                sc = _dot_nt(jnp.concatenate([kb, q], axis=0), k)
                key = (s, n, h)
                lower[key] = jnp.where(strict, -sc[:c] * dec, 0.0)
                attn[key] = sc[c:] * dec
                rhs2[key] = jnp.concatenate([v * beta, kb * egc], axis=1)
                gc_last = gc[c - 1:c, :]
                kdec[key] = k * jnp.exp(gc_last - gc)
                glast[key] = jnp.exp(gc_last)
                qg[key] = q * egc
        tinv = dict(zip(inst, _unit_lower_inverses([lower[i] for i in inst])))
        uw = {i: _dot(tinv[i], rhs2[i]) for i in inst}
        ao = {i: _dot(attn[i], uw[i]) for i in inst}
        bm = {i: _dot_tn(kdec[i], uw[i]) for i in inst}
        for (s, n) in chunks:
            r0 = s * l + n * c
            for h in heads:
                i = (s, n, h)
                st = state[(s, h)]
                o = _dot(qg[i] - ao[i][:, A_DV:], st) + ao[i][:, :A_DV]
                state[(s, h)] = st * glast[i] - _dot(bm[i][:, A_DV:], st) + bm[i][:, :A_DV]
                o = o * lax.rsqrt(jnp.mean(o * o, axis=-1, keepdims=True) + EPS)
                z = za_ref[r0:r0 + c, A_CONV_CH + h * A_DV:A_CONV_CH + (h + 1) * A_DV]
                y_ref[r0:r0 + c, h * A_DV:(h + 1) * A_DV] = o * ng * _silu(z)

    for g in range(0, len(all_chunks), group):
        advance(all_chunks[g:g + group])
    for s in range(ns):
        for h in heads:
            s_scr[s, h] = state[(s, h)]

    _emit_state(last, sout_ref, prev_s_ref, s_scr)


def _with_prev(kern, prev, shapes, ns, group_of=_group_of_2d):
    if prev is None:
        return functools.partial(kern, *([None] * len(shapes))), [], [], 0
    n_prev = prev[0].shape[0]
    return kern, list(prev), [_stacked_spec(n_prev, (ns,) + shp, group_of) for shp in shapes], n_prev


def _mixer_grid(bsz, t_pad, t_valid, l, ns):
    assert t_pad % l == 0 and bsz % ns == 0 and (ns == 1 or t_pad == l)
    nt = t_pad // l
    valid_rows = t_valid - (nt - 1) * l
    assert 0 < valid_rows <= l and (nt == 1 or valid_rows == l)
    return (bsz // ns, nt), ns * l, valid_rows


def _gdn(za, s0, alog_row, dt_row, ng, *, bsz, t_pad, t_valid, c, nc, ns, group,
         layer, state_layer, prev):
    grid, rows, valid_rows = _mixer_grid(bsz, t_pad, t_valid, nc * c, ns)
    spec = functools.partial(_param_spec, layer=layer)
    nt = grid[1]
    kern = functools.partial(_gdn_kernel, c=c, nc=nc, ns=ns, nt=nt, valid_rows=valid_rows,
                             group=group)
    state_shapes = ((A_HEADS, A_DK, A_DV),)
    kern, prev_args, prev_specs, n_prev = _with_prev(kern, prev, state_shapes, ns)
    return pl.pallas_call(
        kern,
        grid=grid,
        in_specs=prev_specs + [
                  pl.BlockSpec((rows, ZA_W), lambda b, t: (b * nt + t, 0)),
                  _state_spec((ns, A_HEADS, A_DK, A_DV), state_layer),
                  spec((1, LANE)),
                  spec((1, LANE)),
                  spec((1, A_DV))],
        out_specs=[pl.BlockSpec((rows, A_V_W), lambda b, t: (b * nt + t, 0)),
                   ] + [_stacked_spec(n_prev + 1, (ns,) + shp) for shp in state_shapes],
        out_shape=[jax.ShapeDtypeStruct((bsz * t_pad, A_V_W), F32)]
        + [jax.ShapeDtypeStruct((n_prev + 1, bsz) + shp, F32) for shp in state_shapes],
        scratch_shapes=[pltpu.VMEM((ns, A_HEADS, A_DK, A_DV), F32)],
        compiler_params=pltpu.CompilerParams(dimension_semantics=("arbitrary", "arbitrary"),
                                             vmem_limit_bytes=VMEM_LIMIT),
        name="gdn",
    )(*prev_args, za, s0, alog_row, dt_row, ng)


def _rwkv_kernel(prev_s_ref, prev_shift_ref, zb_ref, shift0_ref, s0_ref, mu_ref, w0_ref, wup_ref, a0_ref,
                 aup_ref, gup_ref, kk_ref, ka_ref, rk_ref, lnw_ref, lnb_ref, hsum_ref,
                 y_ref, sout_ref, shiftout_ref, xpad, s_scr, prep, *, c, nc, ns, nt, valid_rows, group):
    t = pl.program_id(1)
    first = t == 0
    last = t == nt - 1
    l = nc * c
    rows = ns * l

    @pl.when(first)
    def _():
        s_scr[...] = s0_ref[...]

    zb = zb_ref[...]
    (prev,) = _with_history(zb, xpad, first, shift0_ref, ns, l, 1)
    _carry_history(xpad, l, valid_rows, last, shiftout_ref, prev_shift_ref, 1)
    zs = zb + (prev - zb) * mu_ref[...]

    r_all = zs[:, 0:B_W]
    k_all = zs[:, B_W:2 * B_W]
    v_all = zs[:, 2 * B_W:3 * B_W]
    xwa = zs[:, 3 * B_W:3 * B_W + B_LORA_WA]
    xg = zs[:, 3 * B_W + B_LORA_WA:B_IN]
    lw = -DECAY_SCALE * _sigmoid(w0_ref[...] + _dot(jnp.tanh(xwa), wup_ref[...]))
    a_all = _sigmoid(a0_ref[...] + _dot(xwa, aup_ref[...]))
    g_all = _dot(_sigmoid(xg), gup_ref[...])
    kk_all = k_all * kk_ref[...]
    k_all = k_all * (1.0 + (a_all - 1.0) * ka_ref[...])
    masked = valid_rows < l
    if masked:
        valid = _rows_in((rows, B_W), l) < valid_rows
        lw = jnp.where(valid, lw, 0.0)
        kk_all = jnp.where(valid, kk_all, 0.0)
        k_all = jnp.where(valid, k_all, 0.0)
        v_all = jnp.where(valid, v_all, 0.0)
    gc_all = _cumsum_chunks(lw, c)

    def head_sums(x, pieces=1):
        return sum(jnp.dot(p, hsum_ref[...], preferred_element_type=F32) for p in _bf16_pieces(x, pieces))

    kkn_all = kk_all * lax.rsqrt(head_sums(kk_all * kk_all, pieces=2) + EPS)
    bonus_all = head_sums(r_all * k_all * rk_ref[...]) * v_all

    rr = _rows((2 * c, 2 * c))
    cc = _cols((2 * c, 2 * c))
    tt = jnp.where(rr >= c, rr - c, rr)
    ss = jnp.where(cc >= c, cc - c, cc)
    mask2 = jnp.logical_or(tt > ss, jnp.logical_and(rr >= c, tt == ss))

    prep_names = ('r', 'k', 'v', 'kkn', 'a', 'lw', 'gc', 'bonus', 'g')
    for idx, val in enumerate((r_all, k_all, v_all, kkn_all, a_all, lw, gc_all, bonus_all, g_all)):
        prep[idx] = val
    heads = range(B_HEADS)
    sls = [slice(h * B_N, (h + 1) * B_N) for h in heads]

    def at_rows(name, rs):
        return prep[prep_names.index(name), rs, :]

    def advance(g, carry):
        members = range(group)
        inst = [(j, h) for j in members for h in heads]
        amat, rhs_uw, r_abs, tails, e_last, vs, rows_of, seq_of = {}, {}, {}, {}, {}, {}, {}, {}
        for j in members:
            q = g * group + j
            rs = pl.ds(pl.multiple_of(q * c, c), c)
            rows_of[j] = rs
            seq_of[j] = 0 if ns == 1 else q
            gc = at_rows('gc', rs)
            lwc = at_rows('lw', rs)
            gmid = gc[c // 2:c // 2 + 1, :]
            glast = gc[c - 1:c, :]
            e_last[j] = jnp.exp(glast)
            r = at_rows('r', rs)
            k = at_rows('k', rs)
            v = at_rows('v', rs)
            alpha = -at_rows('kkn', rs)
            bet = at_rows('kkn', rs) * at_rows('a', rs)
            e_inv = jnp.exp(gmid - gc)
            e_tail = jnp.exp(glast - gc)
            lhs_a = alpha * jnp.exp(gc - lwc - gmid)
            lhs_r = r * jnp.exp(gc - gmid)
            rhs_b = bet * e_inv
            rhs_k = k * e_inv
            abs_a = alpha * jnp.exp(gc - lwc)
            abs_r = r * jnp.exp(gc)
            tail_b = bet * e_tail
            tail_k = k * e_tail
            for h in heads:
                sl = sls[h]
                lhs = jnp.concatenate([lhs_a[:, sl], lhs_r[:, sl]], axis=0)
                rhs = jnp.concatenate([rhs_b[:, sl], rhs_k[:, sl]], axis=0)
                key = (j, h)
                amat[key] = jnp.where(mask2, _dot_nt(lhs, rhs), 0.0)
                rhs_uw[key] = abs_a[:, sl]
                r_abs[key] = abs_r[:, sl]
                tails[key] = jnp.concatenate([tail_b[:, sl], tail_k[:, sl]], axis=0)
                vs[key] = v[:, sl]

        from_v = {i: _dot(amat[i][:, c:], vs[i]) for i in inst}
        uw = dict(zip(inst, _unit_lower_solves(
            [amat[i][:c, :c] for i in inst],
            [jnp.concatenate([from_v[i][:c], rhs_uw[i]], axis=1) for i in inst])))
        ro = {i: _dot(amat[i][c:, :c], uw[i]) for i in inst}
        zeros = jnp.zeros((c, B_N), F32)
        both = {i: _dot_tn(jnp.concatenate([uw[i], jnp.concatenate([vs[i], zeros], axis=1)], axis=0), tails[i])
                for i in inst}
        bb = {i: both[i][:B_N] for i in inst}
        mp = {i: both[i][B_N:] for i in inst}
        state = {}
        outs = []
        for j in members:
            per_head = []
            for h in heads:
                i = (j, h)
                skey = (0 if ns == 1 else j, h)
                st = state[skey] if skey in state else s_scr[seq_of[j], h]
                per_head.append(_dot_nt(r_abs[i] + ro[i][:, B_N:], st) + from_v[i][c:] + ro[i][:, :B_N])
                state[skey] = st * e_last[j][:, sls[h]] + _dot(st, mp[i]) + bb[i]
            outs.append(jnp.concatenate(per_head, axis=1))
        for j, o in zip(members, outs):
            d = o - head_sums(o) * (1.0 / B_N)
            var = head_sums(d * d) * (1.0 / B_N)
            o = d * lax.rsqrt(var + B_LN_EPS) * lnw_ref[...] + lnb_ref[...]
            y_ref[rows_of[j], :] = (o + at_rows('bonus', rows_of[j])) * at_rows('g', rows_of[j])
        for (sk, h), st in state.items():
            s_scr[seq_of[sk], h] = st
        return carry

    assert (ns * nc) % group == 0 and (ns == 1 or nc == 1)
    lax.fori_loop(0, ns * nc // group, advance, 0)

    _emit_state(last, sout_ref, prev_s_ref, s_scr)


def _rwkv(zb, shift0, s0, prm, *, bsz, t_pad, t_valid, c, nc, ns, group, layer, state_layer, prev):
    grid, rows, valid_rows = _mixer_grid(bsz, t_pad, t_valid, nc * c, ns)
    nt = grid[1]
    kern = functools.partial(_rwkv_kernel, c=c, nc=nc, ns=ns, nt=nt, valid_rows=valid_rows,
                             group=group)
    spec = functools.partial(_param_spec, layer=layer)
    row = lambda wd: spec((1, wd))
    state_shapes = ((B_HEADS, B_N, B_N), (1, B_IN))
    kern, prev_args, prev_specs, n_prev = _with_prev(kern, prev, state_shapes, ns)
    return pl.pallas_call(
        kern,
        grid=grid,
        in_specs=prev_specs + [
                  pl.BlockSpec((rows, ZB_W), lambda b, t: (b * nt + t, 0)),
                  _state_spec((ns, 1, B_IN), state_layer),
                  _state_spec((ns, B_HEADS, B_N, B_N), state_layer),
                  row(B_IN), row(B_W), spec((B_LORA_WA, B_W)), row(B_W),
                  spec((B_LORA_WA, B_W)), spec((B_GATE_LORA, B_W)),
                  row(B_W), row(B_W), row(B_W), row(B_W), row(B_W), _param_spec((B_W, B_W), 0)],
        out_specs=[pl.BlockSpec((rows, B_W), lambda b, t: (b * nt + t, 0)),
                   ] + [_stacked_spec(n_prev + 1, (ns,) + shp) for shp in state_shapes],
        out_shape=[jax.ShapeDtypeStruct((bsz * t_pad, B_W), F32)]
        + [jax.ShapeDtypeStruct((n_prev + 1, bsz) + shp, F32) for shp in state_shapes],
        scratch_shapes=[pltpu.VMEM((ns, CONV_PAD + nc * c, B_IN), F32),
                        pltpu.VMEM((ns, B_HEADS, B_N, B_N), F32),
                        pltpu.VMEM((9, rows, B_W), F32)],
        compiler_params=pltpu.CompilerParams(dimension_semantics=("arbitrary", "arbitrary"),
                                             vmem_limit_bytes=VMEM_LIMIT),
        name="rwkv",
    )(*prev_args, zb, shift0, s0, *prm)


FF_CHUNK = 1024


def _merge_ffn_kernel(x_ref, ya_ref, yb_ref, yc_ref, zg_ref, wbr_ref, wout_ref, g2_ref,
                      wup_ref, wdown_ref, gf_ref, o_ref, *, final):
    merged = None
    for n, y_ref in enumerate((ya_ref, yb_ref, yc_ref)):
        proj = _dot(y_ref[...], wbr_ref[n])
        term = _sigmoid(zg_ref[:, n * D_MODEL:(n + 1) * D_MODEL]) * proj
        merged = term if merged is None else merged + term
    x = x_ref[...] + _dot(merged, wout_ref[...])
    h = x * lax.rsqrt(jnp.mean(x * x, axis=-1, keepdims=True) + EPS) * g2_ref[...]
    h = h.astype(BF16)
    acc = x
    for j in range(0, D_FF, FF_CHUNK):
        up = jnp.dot(h, wup_ref[:, j:j + FF_CHUNK], preferred_element_type=F32)
        acc = acc + _dot(jnp.square(jnp.maximum(up, 0.0)), wdown_ref[j:j + FF_CHUNK, :])
    if final:
        acc = acc * lax.rsqrt(jnp.mean(acc * acc, axis=-1, keepdims=True) + EPS) * gf_ref[...]
    o_ref[...] = acc


def _merge_ffn(x, ya, yb, yc, zg, wbr, wout, g2, wup, wdown, gf, tm, final, layer):
    m = x.shape[0]
    spec = functools.partial(_param_spec, layer=layer)
    tile = lambda wd: pl.BlockSpec((tm, wd), lambda i: (i, 0))
    return pl.pallas_call(
        functools.partial(_merge_ffn_kernel, final=final),
        grid=(m // tm,),
        in_specs=[tile(D_MODEL), tile(A_V_W), tile(B_W), tile(C_WIDTH), tile(ZG_W),
                  spec((N_BRANCH, A_V_W, D_MODEL)), spec((D_MODEL, D_MODEL)),
                  spec((1, D_MODEL)), spec((D_MODEL, D_FF)),
                  spec((D_FF, D_MODEL)), _param_spec((1, D_MODEL), 0)],
        out_specs=tile(D_MODEL),
        out_shape=jax.ShapeDtypeStruct((m, D_MODEL), F32),
        compiler_params=pltpu.CompilerParams(dimension_semantics=("arbitrary",),
                                             vmem_limit_bytes=VMEM_LIMIT),
        name="merge_ffn",
    )(x, ya, yb, yc, zg, wbr, wout, g2, wup, wdown, gf)


def _block_diag(w):
    g, n, _ = w.shape
    eye = jnp.eye(g, dtype=w.dtype)
    return (eye[:, None, :, None] * w[:, :, None, :]).reshape(g * n, g * n)


def _prepare_params(p):
    depth = p['w_in'].shape[0]
    row = lambda a: a.reshape(depth, 1, -1).astype(F32)
    w_in = p['w_in'].astype(BF16)
    lane_row = lambda a: jnp.zeros((depth, 1, LANE), F32).at[:, 0, A_HEADS:2 * A_HEADS].set(a)
    zeros_lora = jnp.zeros((depth, B_LORA_WA // 2, B_W), F32)
    block_diag = jax.vmap(_block_diag)
    gdn = (p['a_conv_w'], lane_row(p['a_A_log']), lane_row(p['a_dt_bias']), row(p['a_norm_g']))
    rwkv = (row(p['b_mu']), row(p['b_w0']),
            jnp.concatenate([p['b_w_up'], zeros_lora], axis=1).astype(BF16),
            row(p['b_a0']),
            jnp.concatenate([zeros_lora, p['b_a_up']], axis=1).astype(BF16),
            p['b_g_up'].astype(BF16),
            row(p['b_k_k']), row(p['b_k_a']), row(p['b_r_k']), row(p['b_ln_w']), row(p['b_ln_b']),
            jnp.kron(jnp.eye(B_HEADS, dtype=F32), jnp.ones((B_N, B_N), F32)).astype(BF16)[None])
    rglru = (p['c_conv_w'], row(p['c_conv_b']),
             jnp.concatenate([block_diag(p['c_wa']), block_diag(p['c_wx'])], axis=2).astype(BF16),
             jnp.concatenate([row(p['c_ba']), row(p['c_bx'])], axis=2), row(p['c_L']))
    ffn = (p['w_branch'].astype(BF16), p['w_out'].astype(BF16), row(p['norm2_g']),
           p['w_up'].astype(BF16), p['w_down'].astype(BF16))
    return dict(norm1_g=row(p['norm1_g']), w_in=w_in, gdn=gdn, rwkv=rwkv, rglru=rglru, ffn=ffn)


PROMPT_BLOCKING = dict(c=64, nc=8, ns=1, gdn_group=8, rwkv_group=4, in_rows=256)
SAMPLE_BLOCKING = dict(c=SUBLANE, nc=1, ns=8, gdn_group=8, rwkv_group=4, in_rows=256)
FFN_TILE = 512


def _layer(x, states, new_states, lp, gf, *, layer, state_layer, bsz, t_pad, t_valid, blocking, final):
    a_s, a_conv, b_s, b_shift, c_h, c_conv = states
    prev = (lambda *idx: None) if new_states is None else (lambda *idx: tuple(new_states[i] for i in idx))
    kw = dict(bsz=bsz, t_pad=t_pad, t_valid=t_valid, layer=layer, state_layer=state_layer)
    chunked = dict(c=blocking['c'], nc=blocking['nc'], ns=blocking['ns'], **kw)
    a_convw, *gdn_prm = lp['gdn']
    za, zb, yc, zg, c_h, c_conv, a_conv = _in_proj(
        x, lp['norm1_g'], lp['w_in'], c_conv, c_h.reshape(-1, bsz, 1, C_WIDTH), a_conv, lp['rglru'], a_convw,
        tm=blocking['in_rows'], prev=prev(4, 5, 1), **kw)
    ya, a_s = _gdn(za, a_s, *gdn_prm, group=blocking['gdn_group'], prev=prev(0), **chunked)
    yb, b_s, b_shift = _rwkv(zb, b_shift, b_s, lp['rwkv'], group=blocking['rwkv_group'], prev=prev(2, 3),
                             **chunked)
    x = _merge_ffn(x, ya, yb, yc, zg, *lp['ffn'], gf, min(FFN_TILE, x.shape[0]), final, layer)
    return x, (a_s, a_conv, b_s, b_shift, c_h, c_conv)


def kernel(x_prompt, x_sample, state_a_S, state_a_conv, state_b_S, state_b_shift, state_c_h, state_c_conv, norm1_g, w_in, a_conv_w, a_A_log, a_dt_bias, a_norm_g, b_mu, b_w0, b_w_up, b_a0, b_a_up, b_g_up, b_k_k, b_k_a, b_r_k, b_ln_w, b_ln_b, c_conv_w, c_conv_b, c_wa, c_ba, c_wx, c_bx, c_L, w_branch, w_out, norm2_g, w_up, w_down, final_norm_g):
    p = dict(norm1_g=norm1_g, w_in=w_in, a_conv_w=a_conv_w, a_A_log=a_A_log, a_dt_bias=a_dt_bias,
             a_norm_g=a_norm_g, b_mu=b_mu, b_w0=b_w0, b_w_up=b_w_up, b_a0=b_a0, b_a_up=b_a_up,
             b_g_up=b_g_up, b_k_k=b_k_k, b_k_a=b_k_a, b_r_k=b_r_k, b_ln_w=b_ln_w, b_ln_b=b_ln_b,
             c_conv_w=c_conv_w, c_conv_b=c_conv_b, c_wa=c_wa, c_ba=c_ba, c_wx=c_wx, c_bx=c_bx,
             c_L=c_L, w_branch=w_branch, w_out=w_out, norm2_g=norm2_g, w_up=w_up, w_down=w_down)
    bp, tp, _ = x_prompt.shape
    bs, ts, _ = x_sample.shape
    ts_pad = -(-ts // SUBLANE) * SUBLANE
    assert ts_pad == SAMPLE_BLOCKING['c'] and tp % (PROMPT_BLOCKING['c'] * PROMPT_BLOCKING['nc']) == 0
    gf = final_norm_g.reshape(1, 1, D_MODEL)

    xp = x_prompt.reshape(bp * tp, D_MODEL)
    xs = jnp.pad(x_sample, ((0, 0), (0, ts_pad - ts), (0, 0))).reshape(bs * ts_pad, D_MODEL)
    zero = lambda *shape: jnp.zeros((1,) + shape, F32)
    p_states = (zero(bp, A_HEADS, A_DK, A_DV), zero(bp, 3, A_CONV_CH),
                zero(bp, B_HEADS, B_N, B_N), zero(bp, 1, B_IN),
                zero(bp, C_WIDTH), zero(bp, 3, C_WIDTH))
    s_states = (state_a_S, state_a_conv, state_b_S, state_b_shift, state_c_h, state_c_conv)
    lp = _prepare_params(p)
    p_new = s_new = None
    for l in range(DEPTH):
        final = l == DEPTH - 1
        xp, p_new = _layer(xp, p_states, p_new, lp, gf, layer=l, state_layer=0, bsz=bp, t_pad=tp,
                           t_valid=tp, blocking=PROMPT_BLOCKING, final=final)
        xs, s_new = _layer(xs, s_states, s_new, lp, gf, layer=l, state_layer=l, bsz=bs, t_pad=ts_pad,
                           t_valid=ts, blocking=SAMPLE_BLOCKING, final=final)
    squeeze_h = lambda st: st[:4] + (st[4].reshape(DEPTH, -1, C_WIDTH), st[5])
    y_prompt = xp.reshape(bp, tp, D_MODEL)
    y_sample = xs.reshape(bs, ts_pad, D_MODEL)[:, :ts]
    return (y_prompt, y_sample) + squeeze_h(p_new) + squeeze_h(s_new)
```
